```python
import math
import jax
import jax.numpy as jnp
from jax import lax
import numpy as np

D_MODEL = 1024
BATCH = 8
SEQ = 2048
DEPTH = 1

HEAD_DIM = 64
N_Q_HEADS = 16
N_KV_HEADS = 4
Q_PER_KV = N_Q_HEADS // N_KV_HEADS
ATTN_WIDTH = N_Q_HEADS * HEAD_DIM
KV_WIDTH = N_KV_HEADS * HEAD_DIM
DILATED_PATTERNS = ((128, 1), (512, 4), (2048, 16))
ATTN_BLOCK = 128
ROPE_THETA = 10000.0

SSM_HEADS = 16
SSM_HEAD_DIM = 64
SSM_WIDTH = SSM_HEADS * SSM_HEAD_DIM
SSM_STATE = 128
SSM_GROUPS = 2
SSM_HEADS_PER_GROUP = SSM_HEADS // SSM_GROUPS
BC_WIDTH = SSM_GROUPS * SSM_STATE
CONV_WIDTH = 4
CONV_CHANNELS = SSM_WIDTH + 2 * BC_WIDTH
SSD_CHUNK = 128
DT_MIN = 0.001
DT_MAX = 0.1
A_INIT_MIN = 1.0
A_INIT_MAX = 16.0

MIX_WIDTH = ATTN_WIDTH + SSM_WIDTH
IN_SPLITS = (
    ATTN_WIDTH,
    ATTN_WIDTH + KV_WIDTH,
    ATTN_WIDTH + 2 * KV_WIDTH,
    ATTN_WIDTH + 2 * KV_WIDTH + CONV_CHANNELS,
    ATTN_WIDTH + 2 * KV_WIDTH + CONV_CHANNELS + SSM_WIDTH,
)
IN_PROJ_COLS = IN_SPLITS[-1] + SSM_HEADS

D_FF = 2816
MACARON_WEIGHT = 0.5
NORM_EPS = 1e-6
POS_OFFSET_MAX = 1024

kernel_name = 'hymba_dilated_attn_mamba2_macaron_sandwich'


def rmsnorm(x, g):
    xf = x.astype(jnp.float32)
    y = xf * lax.rsqrt(jnp.mean(xf * xf, axis=-1, keepdims=True) + NORM_EPS)
    return (y * g.astype(jnp.float32)).astype(x.dtype)


def swiglu(x, w_gate, w_up, w_down):
    return (jax.nn.silu(x @ w_gate) * (x @ w_up)) @ w_down


def rope_cos_sin(positions):
    inv_freq = ROPE_THETA ** (-jnp.arange(0, HEAD_DIM, 2, dtype=jnp.float32) / HEAD_DIM)
    ang = positions.astype(jnp.float32)[..., None] * inv_freq
    ang = jnp.concatenate([ang, ang], axis=-1)
    return jnp.cos(ang), jnp.sin(ang)


def apply_rope(t, cos, sin):
    shape = cos.shape[:2] + (1,) * (t.ndim - 3) + (HEAD_DIM,)
    c = cos.reshape(shape)
    s = sin.reshape(shape)
    tf = t.astype(jnp.float32)
    t1, t2 = jnp.split(tf, 2, axis=-1)
    rot = jnp.concatenate([-t2, t1], axis=-1)
    return (tf * c + rot * s).astype(t.dtype)


def _to_strided_blocks(t, dilation, n_blocks):
    b, s = t.shape[:2]
    sub_len = s // dilation
    t = t.reshape((b, sub_len, dilation) + t.shape[2:])
    pad = n_blocks * ATTN_BLOCK - sub_len
    t = jnp.pad(t, [(0, 0), (0, pad)] + [(0, 0)] * (t.ndim - 2))
    return t.reshape((b, n_blocks, ATTN_BLOCK, dilation) + t.shape[3:])


def _from_strided_blocks(t, seq):
    b, n_blocks, _, dilation = t.shape[:4]
    sub_len = seq // dilation
    t = t.reshape((b, n_blocks * ATTN_BLOCK, dilation) + t.shape[4:])[:, :sub_len]
    return t.reshape((b, seq) + t.shape[3:])


def _with_prev_block(t):
    prev = jnp.concatenate([jnp.zeros_like(t[:, :1]), t[:, :-1]], axis=1)
    return jnp.concatenate([prev, t], axis=2)


def dilated_window_attention(q, k, v, window, dilation):
    b, s = q.shape[:2]
    sub_len = s // dilation
    span = window // dilation
    n_blocks = -(-sub_len // ATTN_BLOCK)
    qb = _to_strided_blocks(q, dilation, n_blocks)
    kk = _with_prev_block(_to_strided_blocks(k, dilation, n_blocks))
    vv = _with_prev_block(_to_strided_blocks(v, dilation, n_blocks))
    scores = jnp.einsum('bnqrkgh,bnsrkh->bnrkgqs', qb, kk, preferred_element_type=jnp.float32)
    qi = jnp.arange(ATTN_BLOCK)[:, None]
    si = jnp.arange(2 * ATTN_BLOCK)[None, :]
    dist = qi + ATTN_BLOCK - si
    band = (dist >= 0) & (dist <= span)
    key_idx = jnp.arange(n_blocks)[:, None, None] * ATTN_BLOCK + si[None] - ATTN_BLOCK
    mask = band[None] & (key_idx >= 0)
    scores = jnp.where(mask[None, :, None, None, None], scores, -jnp.inf)
    m = jnp.max(scores, axis=-1)
    p = jnp.exp(scores - m[..., None])
    l = jnp.sum(p, axis=-1)
    o = jnp.einsum('bnrkgqs,bnsrkh->bnqrkgh', p.astype(vv.dtype), vv, preferred_element_type=jnp.float32)
    l_t = jnp.moveaxis(l, -1, 2)
    o = o / l_t[..., None]
    return (_from_strided_blocks(o, s),
            _from_strided_blocks(jnp.moveaxis(m, -1, 2), s),
            _from_strided_blocks(l_t, s))


def mixture_of_dilations(q, k, v):
    outs, maxes, dens = [], [], []
    for window, dilation in DILATED_PATTERNS:
        o, m, l = dilated_window_attention(q, k, v, window, dilation)
        outs.append(o)
        maxes.append(m)
        dens.append(l)
    m_all = jnp.stack(maxes)
    w = jnp.stack(dens) * jnp.exp(m_all - jnp.max(m_all, axis=0, keepdims=True))
    o_all = jnp.stack(outs)
    return jnp.sum(w[..., None] * o_all, axis=0) / jnp.sum(w, axis=0)[..., None]


def causal_depthwise_conv(u, w, bias):
    y = lax.conv_general_dilated(u, w[:, None, :].astype(u.dtype), window_strides=(1,),
                                 padding=[(CONV_WIDTH - 1, 0)],
                                 dimension_numbers=('NWC', 'WIO', 'NWC'),
                                 feature_group_count=u.shape[-1])
    return y + bias


def ssd_chunked_scan(x, dt, a, b_in, c_in):
    bsz, s = x.shape[:2]
    n_chunks = s // SSD_CHUNK
    g, e, p, n = SSM_GROUPS, SSM_HEADS_PER_GROUP, SSM_HEAD_DIM, SSM_STATE
    xs = (x.astype(jnp.float32) * dt[..., None]).reshape(bsz, n_chunks, SSD_CHUNK, g, e, p)
    a_dt = jnp.moveaxis((dt * a).reshape(bsz, n_chunks, SSD_CHUNK, g, e), 2, -1)
    a_cs = jnp.cumsum(a_dt, axis=-1)
    bc = b_in.astype(jnp.float32).reshape(bsz, n_chunks, SSD_CHUNK, g, n)
    cc = c_in.astype(jnp.float32).reshape(bsz, n_chunks, SSD_CHUNK, g, n)
    idx = jnp.arange(SSD_CHUNK)
    causal = idx[:, None] >= idx[None, :]
    seg = a_cs[..., :, None] - a_cs[..., None, :]
    decay_in = jnp.exp(jnp.where(causal, seg, -jnp.inf))
    cb = jnp.einsum('bclgn,bcsgn->bcgls', cc, bc)
    y_diag = jnp.einsum('bcgels,bcsgep->bclgep', cb[:, :, :, None] * decay_in, xs)
    decay_to_end = jnp.exp(a_cs[..., -1:] - a_cs)
    states = jnp.einsum('bclgn,bcgel,bclgep->bcgepn', bc, decay_to_end, xs)
    chunk_decay = jnp.exp(a_cs[..., -1])

    def step(h, inp):
        st, dec = inp
        return h * dec[..., None, None] + st, h

    h0 = jnp.zeros((bsz, g, e, p, n), jnp.float32)
    _, prev_states = lax.scan(step, h0, (jnp.moveaxis(states, 1, 0), jnp.moveaxis(chunk_decay, 1, 0)))
    prev_states = jnp.moveaxis(prev_states, 0, 1)
    y_off = jnp.einsum('bclgn,bcgepn,bcgel->bclgep', cc, prev_states, jnp.exp(a_cs))
    return (y_diag + y_off).reshape(bsz, s, SSM_HEADS, p)


def hybrid_mixer(h, cos, sin, w_in, conv_w, conv_b, dt_bias, a_log, d_skip, ssm_norm, w_out):
    bsz, s, _ = h.shape
    proj = h @ w_in
    q, k, v, xbc, z, dt_raw = jnp.split(proj, list(IN_SPLITS), axis=-1)

    q = apply_rope(q.reshape(bsz, s, N_KV_HEADS, Q_PER_KV, HEAD_DIM), cos, sin) * (HEAD_DIM ** -0.5)
    k = apply_rope(k.reshape(bsz, s, N_KV_HEADS, HEAD_DIM), cos, sin)
    v = v.reshape(bsz, s, N_KV_HEADS, HEAD_DIM)
    attn_out = mixture_of_dilations(q, k, v).reshape(bsz, s, ATTN_WIDTH)

    xbc = jax.nn.silu(causal_depthwise_conv(xbc, conv_w, conv_b))
    xs, b_in, c_in = jnp.split(xbc, [SSM_WIDTH, SSM_WIDTH + BC_WIDTH], axis=-1)
    xs = xs.reshape(bsz, s, SSM_HEADS, SSM_HEAD_DIM)
    b_in = b_in.reshape(bsz, s, SSM_GROUPS, SSM_STATE)
    c_in = c_in.reshape(bsz, s, SSM_GROUPS, SSM_STATE)
    dt = jax.nn.softplus(dt_raw.astype(jnp.float32) + dt_bias.astype(jnp.float32))
    a = -jnp.exp(a_log.astype(jnp.float32))
    y = ssd_chunked_scan(xs, dt, a, b_in, c_in)
    y = y + d_skip.astype(jnp.float32)[:, None] * xs.astype(jnp.float32)
    y = y.reshape(bsz, s, SSM_WIDTH) * jax.nn.silu(z.astype(jnp.float32))
    y = y.reshape(bsz, s, SSM_GROUPS, SSM_WIDTH // SSM_GROUPS)
    y = y * lax.rsqrt(jnp.mean(y * y, axis=-1, keepdims=True) + NORM_EPS)
    y = y.reshape(bsz, s, SSM_WIDTH) * ssm_norm.astype(jnp.float32)

    mixed = jnp.concatenate([attn_out.astype(h.dtype), y.astype(h.dtype)], axis=-1)
    return mixed @ w_out


def setup_inputs(seed: int = 0) -> dict:
    key = jax.random.key(seed)
    ks = jax.random.split(key, 24)
    f32 = jnp.float32

    def dense(k, shape, fan_in):
        return jax.random.normal(k, shape, f32) * fan_in ** -0.5

    def gain(k, shape):
        return 1.0 + 0.05 * jax.random.normal(k, shape, f32)

    x = jax.random.normal(ks[0], (BATCH, SEQ, D_MODEL), f32)
    positions = (jnp.arange(SEQ, dtype=jnp.int32)[None, :]
                 + jax.random.randint(ks[1], (BATCH, 1), 0, POS_OFFSET_MAX, dtype=jnp.int32))
    dt = jnp.exp(jax.random.uniform(ks[11], (DEPTH, SSM_HEADS), f32)
                 * (math.log(DT_MAX) - math.log(DT_MIN)) + math.log(DT_MIN))
    dt_bias = dt + jnp.log(-jnp.expm1(-dt))
    a_log = jnp.log(jax.random.uniform(ks[12], (DEPTH, SSM_HEADS), f32, A_INIT_MIN, A_INIT_MAX))
    return {
        'x': x,
        'positions': positions,
        'ffn1_pre_norm': gain(ks[2], (DEPTH, D_MODEL)),
        'ffn1_w_gate': dense(ks[3], (DEPTH, D_MODEL, D_FF), D_MODEL),
        'ffn1_w_up': dense(ks[4], (DEPTH, D_MODEL, D_FF), D_MODEL),
        'ffn1_w_down': dense(ks[5], (DEPTH, D_FF, D_MODEL), D_FF),
        'ffn1_post_norm': gain(ks[6], (DEPTH, D_MODEL)),
        'mix_pre_norm': gain(ks[7], (DEPTH, D_MODEL)),
        'w_in': dense(ks[8], (DEPTH, D_MODEL, IN_PROJ_COLS), D_MODEL),
        'conv_w': dense(ks[9], (DEPTH, CONV_WIDTH, CONV_CHANNELS), CONV_WIDTH),
        'conv_b': 0.02 * jax.random.normal(ks[10], (DEPTH, CONV_CHANNELS), f32),
        'dt_bias': dt_bias,
        'a_log': a_log,
        'd_skip': 1.0 + 0.1 * jax.random.normal(ks[13], (DEPTH, SSM_HEADS), f32),
        'ssm_norm': gain(ks[14], (DEPTH, SSM_WIDTH)),
        'w_out': dense(ks[15], (DEPTH, MIX_WIDTH, D_MODEL), MIX_WIDTH),
        'mix_post_norm': gain(ks[16], (DEPTH, D_MODEL)),
        'ffn2_pre_norm': gain(ks[17], (DEPTH, D_MODEL)),
        'ffn2_w_gate': dense(ks[18], (DEPTH, D_MODEL, D_FF), D_MODEL),
        'ffn2_w_up': dense(ks[19], (DEPTH, D_MODEL, D_FF), D_MODEL),
        'ffn2_w_down': dense(ks[20], (DEPTH, D_FF, D_MODEL), D_FF),
        'ffn2_post_norm': gain(ks[21], (DEPTH, D_MODEL)),
    }


def reference(x, positions, ffn1_pre_norm, ffn1_w_gate, ffn1_w_up, ffn1_w_down, ffn1_post_norm,
              mix_pre_norm, w_in, conv_w, conv_b, dt_bias, a_log, d_skip, ssm_norm, w_out, mix_post_norm,
              ffn2_pre_norm, ffn2_w_gate, ffn2_w_up, ffn2_w_down, ffn2_post_norm):
    cos, sin = rope_cos_sin(positions)
    for i in range(DEPTH):
        h = swiglu(rmsnorm(x, ffn1_pre_norm[i]), ffn1_w_gate[i], ffn1_w_up[i], ffn1_w_down[i])
        x = x + MACARON_WEIGHT * rmsnorm(h, ffn1_post_norm[i])
        h = hybrid_mixer(rmsnorm(x, mix_pre_norm[i]), cos, sin, w_in[i], conv_w[i], conv_b[i],
                         dt_bias[i], a_log[i], d_skip[i], ssm_norm[i], w_out[i])
        x = x + rmsnorm(h, mix_post_norm[i])
        h = swiglu(rmsnorm(x, ffn2_pre_norm[i]), ffn2_w_gate[i], ffn2_w_up[i], ffn2_w_down[i])
        x = x + MACARON_WEIGHT * rmsnorm(h, ffn2_post_norm[i])
    return x
```

```python
import functools
import math

import jax
import jax.numpy as jnp
from jax import lax
from jax.experimental import pallas as pl
from jax.experimental.pallas import tpu as pltpu

F32 = jnp.float32
BF16 = jnp.bfloat16

D_MODEL = 1024
D_FF = 2816
HEAD_DIM = 64
N_Q_HEADS = 16
N_KV_HEADS = 4
ATTN_WIDTH = N_Q_HEADS * HEAD_DIM
KV_WIDTH = N_KV_HEADS * HEAD_DIM
DILATIONS = (1, 4, 16)
ATTN_BLOCK = 128
ROPE_THETA = 10000.0
SSM_HEADS = 16
SSM_HEAD_DIM = 64
SSM_WIDTH = SSM_HEADS * SSM_HEAD_DIM
SSM_STATE = 128
SSM_GROUPS = 2
GROUP_WIDTH = SSM_WIDTH // SSM_GROUPS
BC_WIDTH = SSM_GROUPS * SSM_STATE
CONV_WIDTH = 4
CONV_CHANNELS = SSM_WIDTH + 2 * BC_WIDTH
SSD_CHUNK = 128
MACARON_WEIGHT = 0.5
NORM_EPS = 1e-6

LANES = 128
CARRY_ROWS = 8
MIB = 1024 * 1024

FFN_ROWS = 512
FF_CHUNK = 256
PROJ_ROWS = 512


def _rmsnorm(x, gain):
    return x * lax.rsqrt(jnp.mean(x * x, axis=-1, keepdims=True) + NORM_EPS) * gain


def _resident(shape):
    return pl.BlockSpec(shape, lambda *_: (0,) * len(shape), pipeline_mode=pl.Buffered(1))


def _swiglu_half_step(x, pre, wg_ref, wu_ref, wd_ref, post):
    xn = _rmsnorm(x, pre).astype(BF16)
    acc = jnp.zeros(x.shape, F32)
    for c in range(D_FF // FF_CHUNK):
        cols = slice(c * FF_CHUNK, (c + 1) * FF_CHUNK)
        gate = jnp.dot(xn, wg_ref[:, cols], preferred_element_type=F32)
        up = jnp.dot(xn, wu_ref[:, cols], preferred_element_type=F32)
        hidden = (jax.nn.silu(gate) * up).astype(BF16)
        acc = acc + jnp.dot(hidden, wd_ref[cols, :], preferred_element_type=F32)
    return x + MACARON_WEIGHT * _rmsnorm(acc, post)


def _ffn_kernel(x_ref, pre_ref, wg_ref, wu_ref, wd_ref, post_ref, o_ref):
    o_ref[...] = _swiglu_half_step(x_ref[...], pre_ref[...], wg_ref, wu_ref, wd_ref, post_ref[...])


def _ffn(x2d, pre, wg, wu, wd, post):
    tokens = x2d.shape[0]
    row_spec = pl.BlockSpec((FFN_ROWS, D_MODEL), lambda i: (i, 0))
    return pl.pallas_call(
        _ffn_kernel,
        grid=(tokens // FFN_ROWS,),
        in_specs=[row_spec, _resident((1, D_MODEL)), _resident((D_MODEL, D_FF)), _resident((D_MODEL, D_FF)),
                  _resident((D_FF, D_MODEL)), _resident((1, D_MODEL))],
        out_specs=row_spec,
        out_shape=jax.ShapeDtypeStruct((tokens, D_MODEL), F32),
        compiler_params=pltpu.CompilerParams(dimension_semantics=("parallel",), vmem_limit_bytes=48 * MIB),
        name="ffn",
    )(x2d, pre, wg, wu, wd, post)


def _softplus(x):
    return jnp.maximum(x, 0.0) + jnp.log1p(jnp.exp(-jnp.abs(x)))


def _inproj_kernel(x_ref, pos_ref, invf_ref, gain_ref, wq_ref, wk_ref, wv_ref, wx_ref, wz_ref, wdt_ref,
                   convw_ref, convb_ref, dtb_ref,
                   q_ref, k_ref, v_ref, xs_ref, b_ref, c_ref, z_ref, dt_ref, ubuf_ref):
    rows = x_ref.shape[1]
    xn = _rmsnorm(x_ref[0], gain_ref[...]).astype(BF16)

    ang = pos_ref[0].astype(F32) * invf_ref[...]
    lane = lax.broadcasted_iota(jnp.int32, (rows, LANES), 1)
    first_half = (lane % HEAD_DIM) < (HEAD_DIM // 2)
    cos = jnp.cos(ang)
    sin = jnp.where(first_half, -jnp.sin(ang), jnp.sin(ang))

    def rope(t):
        back = pltpu.roll(t, HEAD_DIM // 2, 1)
        fwd = pltpu.roll(t, LANES - HEAD_DIM // 2, 1)
        return t * cos + jnp.where(first_half, fwd, back) * sin

    q = jnp.dot(xn, wq_ref[...], preferred_element_type=F32)
    for j in range(ATTN_WIDTH // LANES):
        cols = slice(j * LANES, (j + 1) * LANES)
        q_ref[0, :, cols] = (rope(q[:, cols]) * (HEAD_DIM ** -0.5)).astype(BF16)
    k = jnp.dot(xn, wk_ref[...], preferred_element_type=F32)
    for j in range(N_KV_HEADS):
        cols = slice(j * LANES, (j + 1) * LANES)
        k_ref[0, :, cols] = rope(k[:, cols]).astype(BF16)
    v_ref[0] = jnp.dot(xn, wv_ref[...], preferred_element_type=F32).astype(BF16)
    z_ref[0] = jnp.dot(xn, wz_ref[...], preferred_element_type=F32)
    dt_ref[0] = _softplus(jnp.dot(xn, wdt_ref[...], preferred_element_type=F32) + dtb_ref[...])

    @pl.when(pl.program_id(1) == 0)
    def _():
        ubuf_ref[0:CARRY_ROWS, :] = jnp.zeros((CARRY_ROWS, CONV_CHANNELS), F32)

    ubuf_ref[CARRY_ROWS:CARRY_ROWS + rows, :] = jnp.dot(xn, wx_ref[...], preferred_element_type=F32)
    conv = convb_ref[...] + jnp.zeros((rows, CONV_CHANNELS), F32)
    for tap in range(CONV_WIDTH):
        start = CARRY_ROWS - (CONV_WIDTH - 1) + tap
        conv = conv + ubuf_ref[start:start + rows, :] * convw_ref[tap:tap + 1, :]
    act = jax.nn.silu(conv)
    xs_ref[0] = act[:, :SSM_WIDTH]
    b_ref[0] = act[:, SSM_WIDTH:SSM_WIDTH + BC_WIDTH].astype(BF16)
    c_ref[0] = act[:, SSM_WIDTH + BC_WIDTH:].astype(BF16)
    ubuf_ref[0:CARRY_ROWS, :] = ubuf_ref[rows:rows + CARRY_ROWS, :]


def _in_proj(x3d, pos3d, invf, gain, wq, wk, wv, wx, wz, wdt, convw, convb, dtb):
    batch, seq, _ = x3d.shape
    rows = PROJ_ROWS

    def tile(width):
        return pl.BlockSpec((1, rows, width), lambda b, j: (b, j, 0))

    def out(width, dtype):
        return jax.ShapeDtypeStruct((batch, seq, width), dtype)

    dup_kv = 2 * KV_WIDTH
    return pl.pallas_call(
        _inproj_kernel,
        grid=(batch, seq // rows),
        in_specs=[tile(D_MODEL), tile(1), _resident((1, LANES)), _resident((1, D_MODEL)),
                  _resident((D_MODEL, ATTN_WIDTH)), _resident((D_MODEL, dup_kv)), _resident((D_MODEL, dup_kv)),
                  _resident((D_MODEL, CONV_CHANNELS)), _resident((D_MODEL, SSM_WIDTH)),
                  _resident((D_MODEL, LANES)), _resident((CONV_WIDTH, CONV_CHANNELS)),
                  _resident((1, CONV_CHANNELS)), _resident((1, LANES))],
        out_specs=[tile(ATTN_WIDTH), tile(dup_kv), tile(dup_kv), tile(SSM_WIDTH), tile(BC_WIDTH), tile(BC_WIDTH),
                   tile(SSM_WIDTH), tile(LANES)],
        out_shape=[out(ATTN_WIDTH, BF16), out(dup_kv, BF16), out(dup_kv, BF16), out(SSM_WIDTH, F32),
                   out(BC_WIDTH, BF16), out(BC_WIDTH, BF16), out(SSM_WIDTH, F32), out(LANES, F32)],
        scratch_shapes=[pltpu.VMEM((rows + CARRY_ROWS, CONV_CHANNELS), F32)],
        compiler_params=pltpu.CompilerParams(dimension_semantics=("parallel", "arbitrary"),
                                             vmem_limit_bytes=56 * MIB),
        name="in_proj",
    )(x3d, pos3d, invf, gain, wq, wk, wv, wx, wz, wdt, convw, convb, dtb)


STAT_L_OFFSET = N_Q_HEADS


def _attn_kernel(*refs, n_blocks, first, last):
    if first:
        q_ref, k_ref, v_ref, acc_out, stat_out = refs
        acc_in = stat_in = None
    elif last:
        q_ref, k_ref, v_ref, acc_in, stat_in, acc_out = refs
        stat_out = None
    else:
        q_ref, k_ref, v_ref, acc_in, stat_in, acc_out, stat_out = refs

    blk = ATTN_BLOCK
    if n_blocks == 1:
        key_start, n_keys, offset = 0, blk, 0
    else:
        n = pl.program_id(2)
        key_start = pl.multiple_of(jnp.maximum(n - 1, 0) * blk, blk)
        n_keys = 2 * blk
        offset = n * blk - key_start
    qi = lax.broadcasted_iota(jnp.int32, (blk, n_keys), 0)
    si = lax.broadcasted_iota(jnp.int32, (blk, n_keys), 1)
    dist = qi + offset - si
    bias = jnp.where((dist >= 0) & (dist <= blk), 0.0, -jnp.inf).astype(F32)

    lane = lax.broadcasted_iota(jnp.int32, (blk, LANES), 1)
    low = lane < HEAD_DIM
    key_low = lax.broadcasted_iota(jnp.int32, (n_keys, LANES), 1) < HEAD_DIM
    zero = jnp.zeros((), BF16)
    old_stat = None if first else stat_in[0]
    new_stat = jnp.zeros((blk, LANES), F32)

    for kvh in range(N_KV_HEADS):
        kv_cols = slice(kvh * LANES, (kvh + 1) * LANES)
        kd = k_ref[0, pl.ds(key_start, n_keys), kv_cols]
        vd = v_ref[0, pl.ds(key_start, n_keys), kv_cols]
        v_low = jnp.where(key_low, vd, zero)
        v_high = jnp.where(key_low, zero, vd)
        col_a = slice(kvh * 2 * LANES, kvh * 2 * LANES + LANES)
        col_b = slice(kvh * 2 * LANES + LANES, (kvh + 1) * 2 * LANES)
        qa = q_ref[0, :, col_a]
        qb = q_ref[0, :, col_b]
        q_stack = jnp.concatenate([jnp.where(low, qa, zero), jnp.where(low, qb, zero),
                                   jnp.where(low, zero, qa), jnp.where(low, zero, qb)], axis=0)
        s = lax.dot_general(q_stack, kd, (((1,), (1,)), ((), ())), preferred_element_type=F32)
        s = s + jnp.concatenate([bias] * 4, axis=0)
        m_blk = jnp.max(s, axis=-1, keepdims=True)
        p = jnp.exp(s - m_blk)
        l_blk = jnp.sum(p, axis=-1, keepdims=True)
        pb = p.astype(BF16)
        o = (jnp.dot(pb[:2 * blk], v_low, preferred_element_type=F32)
             + jnp.dot(pb[2 * blk:], v_high, preferred_element_type=F32))

        scale_old, scale_new, denom = [], [], []
        for slot, g in enumerate((0, 2, 1, 3)):
            head = kvh * (N_Q_HEADS // N_KV_HEADS) + g
            m_b = m_blk[slot * blk:(slot + 1) * blk]
            l_b = l_blk[slot * blk:(slot + 1) * blk]
            if first:
                m_new, l_new = m_b, l_b
                a = None
                b = None
            else:
                m_run = old_stat[:, head:head + 1]
                l_run = old_stat[:, STAT_L_OFFSET + head:STAT_L_OFFSET + head + 1]
                m_new = jnp.maximum(m_run, m_b)
                a = jnp.exp(m_run - m_new)
                b = jnp.exp(m_b - m_new)
                l_new = l_run * a + l_b * b
            scale_old.append(a)
            scale_new.append(b)
            denom.append(l_new)
            if not last:
                new_stat = jnp.where(lane == head, m_new, new_stat)
                new_stat = jnp.where(lane == STAT_L_OFFSET + head, l_new, new_stat)

        for pair, cols in enumerate((col_a, col_b)):
            o_pair = o[pair * blk:(pair + 1) * blk]
            if not first:
                o_pair = (acc_in[0, :, cols] * jnp.where(low, scale_old[pair], scale_old[pair + 2])
                          + o_pair * jnp.where(low, scale_new[pair], scale_new[pair + 2]))
            if last:
                o_pair = o_pair / jnp.where(low, denom[pair], denom[pair + 2])
            acc_out[0, :, cols] = o_pair

    if not last:
        stat_out[0] = new_stat


def _attention_pass(q, k, v, state, dilation, first, last):
    batch, seq, _ = q.shape
    sub_len = seq // dilation
    n_blocks = sub_len // ATTN_BLOCK

    def view(a):
        return a.reshape(batch, sub_len, dilation * a.shape[-1])

    def q_spec(width):
        return pl.BlockSpec((1, ATTN_BLOCK, width), lambda b, r, n: (b, n, r))

    def kv_spec(width):
        return pl.BlockSpec((1, sub_len, width), lambda b, r, n: (b, 0, r))

    dup_kv = 2 * KV_WIDTH
    in_specs = [q_spec(ATTN_WIDTH), kv_spec(dup_kv), kv_spec(dup_kv)]
    args = [view(q), view(k), view(v)]
    if not first:
        in_specs += [q_spec(ATTN_WIDTH), q_spec(LANES)]
        args += [view(state[0]), view(state[1])]
    out_specs = [q_spec(ATTN_WIDTH)]
    out_shape = [jax.ShapeDtypeStruct((batch, sub_len, dilation * ATTN_WIDTH), F32)]
    if not last:
        out_specs.append(q_spec(LANES))
        out_shape.append(jax.ShapeDtypeStruct((batch, sub_len, dilation * LANES), F32))
    outs = pl.pallas_call(
        functools.partial(_attn_kernel, n_blocks=n_blocks, first=first, last=last),
        grid=(batch, dilation, n_blocks),
        in_specs=in_specs, out_specs=out_specs, out_shape=out_shape,
        compiler_params=pltpu.CompilerParams(dimension_semantics=("parallel", "parallel", "parallel"),
                                             vmem_limit_bytes=32 * MIB),
        name=f"attn_d{dilation}",
    )(*args)
    return [o.reshape(batch, seq, -1) for o in outs]


def _split_bf16(x):
    hi = x.astype(BF16)
    lo = (x - hi.astype(F32)).astype(BF16)
    return hi, lo


def _ssd_kernel(xs_ref, b_ref, c_ref, dt_ref, z_ref, arow_ref, dskip_ref, norm_ref, tri_ref, expand_ref,
                y_ref, state_ref):
    chunk = SSD_CHUNK

    @pl.when(pl.program_id(1) == 0)
    def _():
        state_ref[...] = jnp.zeros(state_ref.shape, F32)

    xs = xs_ref[0]
    dt = dt_ref[0]
    a_dt = dt * arow_ref[...]
    a_cs = jnp.dot(tri_ref[...], a_dt, preferred_element_type=F32, precision=lax.Precision.HIGHEST)
    a_last = a_cs[chunk - 1:chunk, :]
    decay_to_end = jnp.exp(a_last - a_cs)
    decay_from_start = jnp.exp(a_cs)

    def expand(stat):
        hi, lo = _split_bf16(stat)
        return (jnp.dot(hi, expand_ref[...], preferred_element_type=F32)
                + jnp.dot(lo, expand_ref[...], preferred_element_type=F32))

    w_state = expand(decay_to_end * dt)
    scale_off = expand(decay_from_start)
    chunk_decay = scale_off[chunk - 1:chunk, :]

    a_cs_t = a_cs.T
    dt_t = dt.T
    row = lax.broadcasted_iota(jnp.int32, (chunk, chunk), 0)
    col = lax.broadcasted_iota(jnp.int32, (chunk, chunk), 1)
    causal = row >= col
    low = lax.broadcasted_iota(jnp.int32, (chunk, LANES), 1) < SSM_HEAD_DIM
    zero = jnp.zeros((), BF16)

    xs_bf = xs.astype(BF16)
    xw_bf = (xs * w_state).astype(BF16)
    heads_per_group = SSM_HEADS // SSM_GROUPS
    y_parts = []
    for g in range(SSM_GROUPS):
        gcols = slice(g * SSM_STATE, (g + 1) * SSM_STATE)
        b_g = b_ref[0, :, gcols]
        c_g = c_ref[0, :, gcols]
        cb = lax.dot_general(c_g, b_g, (((1,), (1,)), ((), ())), preferred_element_type=F32)
        wide = slice(g * GROUP_WIDTH, (g + 1) * GROUP_WIDTH)
        state_g = state_ref[:, wide]
        y_off = jnp.dot(c_g, state_g.astype(BF16), preferred_element_type=F32) * scale_off[:, wide]
        diag_parts = []
        for pair in range(heads_per_group // 2):
            mats = []
            for e in range(2):
                h = g * heads_per_group + 2 * pair + e
                seg = a_cs[:, h:h + 1] - a_cs_t[h:h + 1, :]
                decay = jnp.exp(jnp.where(causal, seg, -jnp.inf))
                mats.append((cb * decay * dt_t[h:h + 1, :]).astype(BF16))
            pcols = slice((g * heads_per_group + 2 * pair) * SSM_HEAD_DIM,
                          (g * heads_per_group + 2 * pair + 2) * SSM_HEAD_DIM)
            x_pair = xs_bf[:, pcols]
            rhs = jnp.concatenate([jnp.where(low, x_pair, zero), jnp.where(low, zero, x_pair)], axis=0)
            diag_parts.append(jnp.dot(jnp.concatenate(mats, axis=1), rhs, preferred_element_type=F32))
        y_parts.append(jnp.concatenate(diag_parts, axis=1) + y_off)
        b_t = b_g.astype(F32).T.astype(BF16)
        new_states = jnp.dot(b_t, xw_bf[:, wide], preferred_element_type=F32)
        state_ref[:, wide] = state_g * chunk_decay[:, wide] + new_states

    y = jnp.concatenate(y_parts, axis=1) + dskip_ref[...] * xs
    y = y * jax.nn.silu(z_ref[0])
    normed = []
    for g in range(SSM_GROUPS):
        y_g = y[:, g * GROUP_WIDTH:(g + 1) * GROUP_WIDTH]
        normed.append(y_g * lax.rsqrt(jnp.mean(y_g * y_g, axis=-1, keepdims=True) + NORM_EPS))
    y_ref[0] = jnp.concatenate(normed, axis=1) * norm_ref[...]


def _ssd(xs, bm, cm, dt, z, arow, dskip, norm, tri, expand):
    batch, seq, _ = xs.shape

    def tile(width):
        return pl.BlockSpec((1, SSD_CHUNK, width), lambda b, c: (b, c, 0))

    return pl.pallas_call(
        _ssd_kernel,
        grid=(batch, seq // SSD_CHUNK),
        in_specs=[tile(SSM_WIDTH), tile(BC_WIDTH), tile(BC_WIDTH), tile(LANES), tile(SSM_WIDTH),
                  _resident((1, LANES)), _resident((1, SSM_WIDTH)), _resident((1, SSM_WIDTH)),
                  _resident((SSD_CHUNK, SSD_CHUNK)), _resident((LANES, SSM_WIDTH))],
        out_specs=tile(SSM_WIDTH),
        out_shape=jax.ShapeDtypeStruct((batch, seq, SSM_WIDTH), F32),
        scratch_shapes=[pltpu.VMEM((SSM_STATE, SSM_WIDTH), F32)],
        compiler_params=pltpu.CompilerParams(dimension_semantics=("parallel", "arbitrary"),
                                             vmem_limit_bytes=32 * MIB),
        name="ssd",
    )(xs, bm, cm, dt, z, arow, dskip, norm, tri, expand)


def _out_ffn_kernel(x_ref, attn_ref, y_ref, wo_ref, mixpost_ref, pre_ref, wg_ref, wu_ref, wd_ref, post_ref, o_ref):
    mixed = (jnp.dot(attn_ref[...].astype(BF16), wo_ref[:ATTN_WIDTH, :], preferred_element_type=F32)
             + jnp.dot(y_ref[...].astype(BF16), wo_ref[ATTN_WIDTH:, :], preferred_element_type=F32))
    x = x_ref[...] + _rmsnorm(mixed, mixpost_ref[...])
    o_ref[...] = _swiglu_half_step(x, pre_ref[...], wg_ref, wu_ref, wd_ref, post_ref[...])


def _out_ffn(x2d, attn2d, y2d, wo, mixpost, pre, wg, wu, wd, post):
    tokens = x2d.shape[0]
    row_spec = pl.BlockSpec((FFN_ROWS, D_MODEL), lambda i: (i, 0))
    return pl.pallas_call(
        _out_ffn_kernel,
        grid=(tokens // FFN_ROWS,),
        in_specs=[row_spec, row_spec, row_spec, _resident((ATTN_WIDTH + SSM_WIDTH, D_MODEL)),
                  _resident((1, D_MODEL)), _resident((1, D_MODEL)), _resident((D_MODEL, D_FF)),
                  _resident((D_MODEL, D_FF)), _resident((D_FF, D_MODEL)), _resident((1, D_MODEL))],
        out_specs=row_spec,
        out_shape=jax.ShapeDtypeStruct((tokens, D_MODEL), F32),
        compiler_params=pltpu.CompilerParams(dimension_semantics=("parallel",), vmem_limit_bytes=56 * MIB),
        name="out_ffn",
    )(x2d, attn2d, y2d, wo, mixpost, pre, wg, wu, wd, post)


def _dup_heads(w):
    d = w.shape[0]
    w = w.reshape(d, N_KV_HEADS, 1, HEAD_DIM)
    return jnp.broadcast_to(w, (d, N_KV_HEADS, 2, HEAD_DIM)).reshape(d, 2 * KV_WIDTH)


def _pad_lanes(v):
    return jnp.pad(v, [(0, 0)] * (v.ndim - 1) + [(0, LANES - v.shape[-1])])


def _layer(x, pos3d, invf, tri, expand, p):
    batch, seq, _ = x.shape
    tokens = batch * seq
    row = lambda v: v.reshape(1, -1)
    bf = lambda w: w.astype(BF16)

    x1 = _ffn(x.reshape(tokens, D_MODEL), row(p["ffn1_pre_norm"]), bf(p["ffn1_w_gate"]), bf(p["ffn1_w_up"]),
              bf(p["ffn1_w_down"]), row(p["ffn1_post_norm"]))

    w_in = p["w_in"]
    o_k = ATTN_WIDTH
    o_v = o_k + KV_WIDTH
    o_x = o_v + KV_WIDTH
    o_z = o_x + CONV_CHANNELS
    o_dt = o_z + SSM_WIDTH
    q, k, v, xs, bm, cm, z, dt = _in_proj(
        x1.reshape(batch, seq, D_MODEL), pos3d, invf, row(p["mix_pre_norm"]),
        bf(w_in[:, :o_k]), bf(_dup_heads(w_in[:, o_k:o_v])), bf(_dup_heads(w_in[:, o_v:o_x])),
        bf(w_in[:, o_x:o_z]), bf(w_in[:, o_z:o_dt]), bf(_pad_lanes(w_in[:, o_dt:])),
        p["conv_w"], row(p["conv_b"]), _pad_lanes(row(p["dt_bias"])))

    state = None
    for i, dilation in enumerate(DILATIONS):
        state = _attention_pass(q, k, v, state, dilation, first=(i == 0), last=(i == len(DILATIONS) - 1))
    attn = state[0]

    arow = _pad_lanes(row(-jnp.exp(p["a_log"])))
    dskip = row(jnp.repeat(p["d_skip"], SSM_HEAD_DIM))
    y = _ssd(xs, bm, cm, dt, z, arow, dskip, row(p["ssm_norm"]), tri, expand)

    out = _out_ffn(x1, attn.reshape(tokens, ATTN_WIDTH), y.reshape(tokens, SSM_WIDTH), bf(p["w_out"]),
                   row(p["mix_post_norm"]), row(p["ffn2_pre_norm"]), bf(p["ffn2_w_gate"]), bf(p["ffn2_w_up"]),
                   bf(p["ffn2_w_down"]), row(p["ffn2_post_norm"]))
    return out.reshape(batch, seq, D_MODEL)


def kernel(x, positions, ffn1_pre_norm, ffn1_w_gate, ffn1_w_up, ffn1_w_down, ffn1_post_norm, mix_pre_norm, w_in, conv_w, conv_b, dt_bias, a_log, d_skip, ssm_norm, w_out, mix_post_norm, ffn2_pre_norm, ffn2_w_gate, ffn2_w_up, ffn2_w_down, ffn2_post_norm):
    params = dict(ffn1_pre_norm=ffn1_pre_norm, ffn1_w_gate=ffn1_w_gate, ffn1_w_up=ffn1_w_up,
                  ffn1_w_down=ffn1_w_down, ffn1_post_norm=ffn1_post_norm, mix_pre_norm=mix_pre_norm, w_in=w_in,
                  conv_w=conv_w, conv_b=conv_b, dt_bias=dt_bias, a_log=a_log, d_skip=d_skip, ssm_norm=ssm_norm,
                  w_out=w_out, mix_post_norm=mix_post_norm, ffn2_pre_norm=ffn2_pre_norm, ffn2_w_gate=ffn2_w_gate,
                  ffn2_w_up=ffn2_w_up, ffn2_w_down=ffn2_w_down, ffn2_post_norm=ffn2_post_norm)
    depth = w_in.shape[0]
    batch, seq, _ = x.shape
    inv_freq = ROPE_THETA ** (-jnp.arange(0, HEAD_DIM, 2, dtype=F32) / HEAD_DIM)
    invf = jnp.tile(inv_freq, LANES // (HEAD_DIM // 2)).reshape(1, LANES)
    pos3d = positions.reshape(batch, seq, 1)
    idx = jnp.arange(SSD_CHUNK)
    tri = (idx[:, None] >= idx[None, :]).astype(F32)
    expand = (jnp.arange(LANES)[:, None] == jnp.arange(SSM_WIDTH)[None, :] // SSM_HEAD_DIM).astype(BF16)
    for i in range(depth):
        x = _layer(x, pos3d, invf, tri, expand, {name: w[i] for name, w in params.items()})
    return x
```

```python
import functools
import math

import jax
import jax.numpy as jnp
from jax import lax
from jax.experimental import pallas as pl
from jax.experimental.pallas import tpu as pltpu

F32 = jnp.float32
BF16 = jnp.bfloat16

D_MODEL = 1024
D_FF = 2816
HEAD_DIM = 64
N_Q_HEADS = 16
N_KV_HEADS = 4
Q_PER_KV = N_Q_HEADS // N_KV_HEADS
ATTN_WIDTH = N_Q_HEADS * HEAD_DIM
KV_WIDTH = N_KV_HEADS * HEAD_DIM
ATTN_BLOCK = 128
ROPE_THETA = 10000.0
SSM_HEADS = 16
SSM_HEAD_DIM = 64
SSM_WIDTH = SSM_HEADS * SSM_HEAD_DIM
SSM_STATE = 128
SSM_GROUPS = 2
GROUP_WIDTH = SSM_WIDTH // SSM_GROUPS
BC_WIDTH = SSM_GROUPS * SSM_STATE
CONV_WIDTH = 4
CONV_CHANNELS = SSM_WIDTH + 2 * BC_WIDTH
SSD_CHUNK = 128
MACARON_WEIGHT = 0.5
NORM_EPS = 1e-6

LANES = 128
CARRY_ROWS = 8
MIB = 1024 * 1024

FFN_ROWS = 512
FF_CHUNK = 256
PROJ_ROWS = 512

Q_SLABS = ATTN_WIDTH // LANES
KV_SLABS = KV_WIDTH // LANES
KV_PAIRS = N_KV_HEADS // 2


def _rmsnorm(x, gain):
    return x * lax.rsqrt(jnp.mean(x * x, axis=-1, keepdims=True) + NORM_EPS) * gain


def _resident(shape):
    return pl.BlockSpec(shape, lambda *_: (0,) * len(shape), pipeline_mode=pl.Buffered(1))


def _swiglu_half_step(x, pre, wg_ref, wu_ref, wd_ref, post):
    xn = _rmsnorm(x, pre).astype(BF16)
    acc = jnp.zeros(x.shape, F32)
    for c in range(D_FF // FF_CHUNK):
        cols = slice(c * FF_CHUNK, (c + 1) * FF_CHUNK)
        gate = jnp.dot(xn, wg_ref[:, cols], preferred_element_type=F32)
        up = jnp.dot(xn, wu_ref[:, cols], preferred_element_type=F32)
        hidden = (jax.nn.silu(gate) * up).astype(BF16)
        acc = acc + jnp.dot(hidden, wd_ref[cols, :], preferred_element_type=F32)
    return x + MACARON_WEIGHT * _rmsnorm(acc, post)


def _ffn_kernel(x_ref, pre_ref, wg_ref, wu_ref, wd_ref, post_ref, o_ref):
    o_ref[...] = _swiglu_half_step(x_ref[...], pre_ref[...], wg_ref, wu_ref, wd_ref, post_ref[...])


def _ffn(x2d, pre, wg, wu, wd, post):
    tokens = x2d.shape[0]
    row_spec = pl.BlockSpec((FFN_ROWS, D_MODEL), lambda i: (i, 0))
    return pl.pallas_call(
        _ffn_kernel,
        grid=(tokens // FFN_ROWS,),
        in_specs=[row_spec, _resident((1, D_MODEL)), _resident((D_MODEL, D_FF)), _resident((D_MODEL, D_FF)),
                  _resident((D_FF, D_MODEL)), _resident((1, D_MODEL))],
        out_specs=row_spec,
        out_shape=jax.ShapeDtypeStruct((tokens, D_MODEL), F32),
        compiler_params=pltpu.CompilerParams(dimension_semantics=("parallel",), vmem_limit_bytes=48 * MIB),
        name="ffn",
    )(x2d, pre, wg, wu, wd, post)


def _softplus(x):
    return jnp.maximum(x, 0.0) + jnp.log1p(jnp.exp(-jnp.abs(x)))


def _inproj_kernel(x_ref, pos_ref, invf_ref, gain_ref, wq_ref, wk_ref, wv_ref, wx_ref, wz_ref, wdt_ref,
                   convw_ref, convb_ref, dtb_ref,
                   q_ref, k_ref, v_ref, xs_ref, b_ref, c_ref, z_ref, dt_ref, ubuf_ref):
    rows = x_ref.shape[1]
    xn = _rmsnorm(x_ref[0], gain_ref[...]).astype(BF16)

    ang = pos_ref[0].astype(F32) * invf_ref[...]
    lane = lax.broadcasted_iota(jnp.int32, (rows, LANES), 1)
    first_half = (lane % HEAD_DIM) < (HEAD_DIM // 2)
    cos = jnp.cos(ang)
    sin = jnp.where(first_half, -jnp.sin(ang), jnp.sin(ang))

    def rope(t):
        back = pltpu.roll(t, HEAD_DIM // 2, 1)
        fwd = pltpu.roll(t, LANES - HEAD_DIM // 2, 1)
        return t * cos + jnp.where(first_half, fwd, back) * sin

    q = jnp.dot(xn, wq_ref[...], preferred_element_type=F32)
    for j in range(Q_SLABS):
        q_ref[0, j] = rope(q[:, j * LANES:(j + 1) * LANES]) * (HEAD_DIM ** -0.5)
    k = jnp.dot(xn, wk_ref[...], preferred_element_type=F32)
    v = jnp.dot(xn, wv_ref[...], preferred_element_type=F32)
    for j in range(KV_SLABS):
        k_ref[0, j] = rope(k[:, j * LANES:(j + 1) * LANES])
        v_ref[0, j] = v[:, j * LANES:(j + 1) * LANES]
    z_ref[0] = jnp.dot(xn, wz_ref[...], preferred_element_type=F32)
    dt_ref[0] = _softplus(jnp.dot(xn, wdt_ref[...], preferred_element_type=F32) + dtb_ref[...])

    @pl.when(pl.program_id(1) == 0)
    def _():
        ubuf_ref[0:CARRY_ROWS, :] = jnp.zeros((CARRY_ROWS, CONV_CHANNELS), F32)

    ubuf_ref[CARRY_ROWS:CARRY_ROWS + rows, :] = jnp.dot(xn, wx_ref[...], preferred_element_type=F32)
    conv = convb_ref[...] + jnp.zeros((rows, CONV_CHANNELS), F32)
    for tap in range(CONV_WIDTH):
        start = CARRY_ROWS - (CONV_WIDTH - 1) + tap
        conv = conv + ubuf_ref[start:start + rows, :] * convw_ref[tap:tap + 1, :]
    act = jax.nn.silu(conv)
    xs_ref[0] = act[:, :SSM_WIDTH]
    b_ref[0] = act[:, SSM_WIDTH:SSM_WIDTH + BC_WIDTH].astype(BF16)
    c_ref[0] = act[:, SSM_WIDTH + BC_WIDTH:].astype(BF16)
    ubuf_ref[0:CARRY_ROWS, :] = ubuf_ref[rows:rows + CARRY_ROWS, :]


def _in_proj(x3d, pos3d, invf, gain, wq, wk, wv, wx, wz, wdt, convw, convb, dtb):
    batch, seq, _ = x3d.shape
    rows = PROJ_ROWS

    def tile(width):
        return pl.BlockSpec((1, rows, width), lambda b, j: (b, j, 0))

    def slabs(n):
        return pl.BlockSpec((1, n, rows, LANES), lambda b, j: (b, 0, j, 0))

    def out(width, dtype):
        return jax.ShapeDtypeStruct((batch, seq, width), dtype)

    def out_slabs(n):
        return jax.ShapeDtypeStruct((batch, n, seq, LANES), F32)

    return pl.pallas_call(
        _inproj_kernel,
        grid=(batch, seq // rows),
        in_specs=[tile(D_MODEL), tile(1), _resident((1, LANES)), _resident((1, D_MODEL)),
                  _resident((D_MODEL, ATTN_WIDTH)), _resident((D_MODEL, KV_WIDTH)), _resident((D_MODEL, KV_WIDTH)),
                  _resident((D_MODEL, CONV_CHANNELS)), _resident((D_MODEL, SSM_WIDTH)),
                  _resident((D_MODEL, LANES)), _resident((CONV_WIDTH, CONV_CHANNELS)),
                  _resident((1, CONV_CHANNELS)), _resident((1, LANES))],
        out_specs=[slabs(Q_SLABS), slabs(KV_SLABS), slabs(KV_SLABS), tile(SSM_WIDTH), tile(BC_WIDTH),
                   tile(BC_WIDTH), tile(SSM_WIDTH), tile(LANES)],
        out_shape=[out_slabs(Q_SLABS), out_slabs(KV_SLABS), out_slabs(KV_SLABS), out(SSM_WIDTH, F32),
                   out(BC_WIDTH, BF16), out(BC_WIDTH, BF16), out(SSM_WIDTH, F32), out(LANES, F32)],
        scratch_shapes=[pltpu.VMEM((rows + CARRY_ROWS, CONV_CHANNELS), F32)],
        compiler_params=pltpu.CompilerParams(dimension_semantics=("parallel", "arbitrary"),
                                             vmem_limit_bytes=56 * MIB),
        name="in_proj",
    )(x3d, pos3d, invf, gain, wq, wk, wv, wx, wz, wdt, convw, convb, dtb)


STATE_ARRAYS = 3
STATE_U, STATE_M, STATE_L = range(STATE_ARRAYS)


def _attn_block(q_slabs, k_blk, v_blk, bias, old):
    blk = ATTN_BLOCK
    n_keys = k_blk.shape[0]
    low = lax.broadcasted_iota(jnp.int32, (blk, LANES), 1) < HEAD_DIM
    key_low = lax.broadcasted_iota(jnp.int32, (n_keys, LANES), 1) < HEAD_DIM
    zero = jnp.zeros((), BF16)
    one = jnp.ones((), BF16)
    kb = k_blk.astype(BF16)
    vb = v_blk.astype(BF16)
    v_low = jnp.where(key_low, vb, one)
    v_high = jnp.where(key_low, one, vb)
    qb = [q.astype(BF16) for q in q_slabs]
    q_stack = jnp.concatenate([jnp.where(low, q, zero) for q in qb] + [jnp.where(low, zero, q) for q in qb], axis=0)
    s = lax.dot_general(q_stack, kb, (((1,), (1,)), ((), ())), preferred_element_type=F32)
    s = s + jnp.concatenate([bias] * (2 * len(qb)), axis=0)
    m = jnp.max(s, axis=-1, keepdims=True)
    p = jnp.exp(s - m).astype(BF16)
    half = len(qb) * blk
    r_low = jnp.dot(p[:half], v_low, preferred_element_type=F32)
    r_high = jnp.dot(p[half:], v_high, preferred_element_type=F32)
    out = []
    for g in range(len(qb)):
        rows = slice(g * blk, (g + 1) * blk)
        rows_b = slice(half + g * blk, half + (g + 1) * blk)
        u_new = jnp.where(low, r_low[rows], r_high[rows])
        l_new = pltpu.roll(jnp.where(low, r_high[rows], r_low[rows]), HEAD_DIM, 1)
        m_new = jnp.where(low, m[rows], m[rows_b])
        if old is not None:
            u_run, m_run, l_run = old[g]
            m_tot = jnp.maximum(m_run, m_new)
            a = jnp.exp(m_run - m_tot)
            b = jnp.exp(m_new - m_tot)
            u_new = u_run * a + u_new * b
            l_new = l_run * a + l_new * b
            m_new = m_tot
        out.append((u_new, m_new, l_new))
    return out


def _attn_kernel(q_ref, k_ref, v_ref, o_ref, st4_ref, stt_ref):
    blk = ATTN_BLOCK
    seq = q_ref.shape[2]
    n_slabs = q_ref.shape[1]
    sub4 = seq // 4
    qi = lax.broadcasted_iota(jnp.int32, (blk, 2 * blk), 0)
    si = lax.broadcasted_iota(jnp.int32, (blk, 2 * blk), 1)
    band_bias = jnp.where((si >= qi) & (si <= qi + blk), 0.0, -jnp.inf).astype(F32)
    causal = (lax.broadcasted_iota(jnp.int32, (blk, blk), 1) <= lax.broadcasted_iota(jnp.int32, (blk, blk), 0))
    causal_bias = jnp.where(causal, 0.0, -jnp.inf).astype(F32)

    def load_q(rows):
        return [q_ref[0, g, rows, :] for g in range(n_slabs)]

    def load_state(ref, rows):
        return [tuple(ref[a * n_slabs + g, rows, :] for a in range(STATE_ARRAYS)) for g in range(n_slabs)]

    def store_state(ref, rows, state):
        for g in range(n_slabs):
            for a in range(STATE_ARRAYS):
                ref[a * n_slabs + g, rows, :] = state[g][a]

    def store_out(rows, state):
        for g in range(n_slabs):
            o_ref[0, g, rows, :] = state[g][STATE_U] / state[g][STATE_L]

    def d16_block(r16, carry):
        c = r16 // 4
        r4 = r16 % 4
        rows = pl.ds(r16, blk, stride=16)
        state = _attn_block(load_q(rows), k_ref[0, 0, rows, :], v_ref[0, 0, rows, :], causal_bias, None)
        store_state(st4_ref, pl.ds(r4 * sub4 + c, blk, stride=4), state)
        return carry

    lax.fori_loop(0, 16, d16_block, 0)

    def d4_residue(r4, carry):
        first = pl.ds(r4, blk, stride=4)
        state = _attn_block(load_q(first), k_ref[0, 0, first, :], v_ref[0, 0, first, :], causal_bias,
                            load_state(st4_ref, pl.ds(pl.multiple_of(r4 * sub4, blk), blk)))
        store_state(stt_ref, first, state)

        def d4_block(n, inner):
            rows = pl.ds(r4 + 4 * blk * n, blk, stride=4)
            keys = pl.ds(r4 + 4 * blk * (n - 1), 2 * blk, stride=4)
            old = load_state(st4_ref, pl.ds(pl.multiple_of(r4 * sub4 + n * blk, blk), blk))
            st = _attn_block(load_q(rows), k_ref[0, 0, keys, :], v_ref[0, 0, keys, :], band_bias, old)
            store_state(stt_ref, rows, st)
            return inner

        return lax.fori_loop(1, sub4 // blk, d4_block, carry)

    lax.fori_loop(0, 4, d4_residue, 0)

    first = pl.ds(0, blk)
    store_out(first, _attn_block(load_q(first), k_ref[0, 0, first, :], v_ref[0, 0, first, :], causal_bias,
                                 load_state(stt_ref, first)))

    def d1_block(n, carry):
        rows = pl.ds(pl.multiple_of(n * blk, blk), blk)
        keys = pl.ds(pl.multiple_of((n - 1) * blk, blk), 2 * blk)
        store_out(rows, _attn_block(load_q(rows), k_ref[0, 0, keys, :], v_ref[0, 0, keys, :], band_bias,
                                    load_state(stt_ref, rows)))
        return carry

    lax.fori_loop(1, seq // blk, d1_block, 0)


def _attention(q, k, v):
    batch, _, seq, _ = q.shape
    slabs_per_pair = Q_SLABS // KV_PAIRS
    q_spec = pl.BlockSpec((1, slabs_per_pair, seq, LANES), lambda b, p: (b, p, 0, 0))
    kv_spec = pl.BlockSpec((1, 1, seq, LANES), lambda b, p: (b, p, 0, 0))
    state_shape = (STATE_ARRAYS * slabs_per_pair, seq, LANES)
    return pl.pallas_call(
        _attn_kernel,
        grid=(batch, KV_PAIRS),
        in_specs=[q_spec, kv_spec, kv_spec],
        out_specs=q_spec,
        out_shape=jax.ShapeDtypeStruct(q.shape, F32),
        scratch_shapes=[pltpu.VMEM(state_shape, F32), pltpu.VMEM(state_shape, F32)],
        compiler_params=pltpu.CompilerParams(dimension_semantics=("parallel", "parallel"),
                                             vmem_limit_bytes=56 * MIB),
        name="attention",
    )(q, k, v)


def _split_bf16(x):
    hi = x.astype(BF16)
    lo = (x - hi.astype(F32)).astype(BF16)
    return hi, lo


def _ssd_kernel(xs_ref, b_ref, c_ref, dt_ref, z_ref, arow_ref, dskip_ref, norm_ref, tri_ref, expand_ref,
                y_ref, state_ref):
    chunk = SSD_CHUNK

    @pl.when(pl.program_id(1) == 0)
    def _():
        state_ref[...] = jnp.zeros(state_ref.shape, F32)

    xs = xs_ref[0]
    dt = dt_ref[0]
    a_dt = dt * arow_ref[...]
    a_cs = jnp.dot(tri_ref[...], a_dt, preferred_element_type=F32, precision=lax.Precision.HIGHEST)
    a_last = a_cs[chunk - 1:chunk, :]
    decay_to_end = jnp.exp(a_last - a_cs)
    decay_from_start = jnp.exp(a_cs)

    def expand(stat):
        hi, lo = _split_bf16(stat)
        return (jnp.dot(hi, expand_ref[...], preferred_element_type=F32)
                + jnp.dot(lo, expand_ref[...], preferred_element_type=F32))

    w_state = expand(decay_to_end * dt)
    scale_off = expand(decay_from_start)
    chunk_decay = scale_off[chunk - 1:chunk, :]

    a_cs_t = a_cs.T
    dt_t = dt.T
    row = lax.broadcasted_iota(jnp.int32, (chunk, chunk), 0)
    col = lax.broadcasted_iota(jnp.int32, (chunk, chunk), 1)
    causal = row >= col
    low = lax.broadcasted_iota(jnp.int32, (chunk, LANES), 1) < SSM_HEAD_DIM
    zero = jnp.zeros((), BF16)

    xs_bf = xs.astype(BF16)
    xw_bf = (xs * w_state).astype(BF16)
    heads_per_group = SSM_HEADS // SSM_GROUPS
    y_parts = []
    for g in range(SSM_GROUPS):
        gcols = slice(g * SSM_STATE, (g + 1) * SSM_STATE)
        b_g = b_ref[0, :, gcols]
        c_g = c_ref[0, :, gcols]
        cb = lax.dot_general(c_g, b_g, (((1,), (1,)), ((), ())), preferred_element_type=F32)
        wide = slice(g * GROUP_WIDTH, (g + 1) * GROUP_WIDTH)
        state_g = state_ref[:, wide]
        y_off = jnp.dot(c_g, state_g.astype(BF16), preferred_element_type=F32) * scale_off[:, wide]
        diag_parts = []
        for pair in range(heads_per_group // 2):
            mats = []
            for e in range(2):
                h = g * heads_per_group + 2 * pair + e
                seg = a_cs[:, h:h + 1] - a_cs_t[h:h + 1, :]
                decay = jnp.exp(jnp.where(causal, seg, -jnp.inf))
                mats.append((cb * decay * dt_t[h:h + 1, :]).astype(BF16))
            pcols = slice((g * heads_per_group + 2 * pair) * SSM_HEAD_DIM,
                          (g * heads_per_group + 2 * pair + 2) * SSM_HEAD_DIM)
            x_pair = xs_bf[:, pcols]
            rhs = jnp.concatenate([jnp.where(low, x_pair, zero), jnp.where(low, zero, x_pair)], axis=0)
            diag_parts.append(jnp.dot(jnp.concatenate(mats, axis=1), rhs, preferred_element_type=F32))
        y_parts.append(jnp.concatenate(diag_parts, axis=1) + y_off)
        b_t = b_g.astype(F32).T.astype(BF16)
        new_states = jnp.dot(b_t, xw_bf[:, wide], preferred_element_type=F32)
        state_ref[:, wide] = state_g * chunk_decay[:, wide] + new_states

    y = jnp.concatenate(y_parts, axis=1) + dskip_ref[...] * xs
    y = y * jax.nn.silu(z_ref[0])
    normed = []
    for g in range(SSM_GROUPS):
        y_g = y[:, g * GROUP_WIDTH:(g + 1) * GROUP_WIDTH]
        normed.append(y_g * lax.rsqrt(jnp.mean(y_g * y_g, axis=-1, keepdims=True) + NORM_EPS))
    y_ref[0] = jnp.concatenate(normed, axis=1) * norm_ref[...]


def _ssd(xs, bm, cm, dt, z, arow, dskip, norm, tri, expand):
    batch, seq, _ = xs.shape

    def tile(width):
        return pl.BlockSpec((1, SSD_CHUNK, width), lambda b, c: (b, c, 0))

    return pl.pallas_call(
        _ssd_kernel,
        grid=(batch, seq // SSD_CHUNK),
        in_specs=[tile(SSM_WIDTH), tile(BC_WIDTH), tile(BC_WIDTH), tile(LANES), tile(SSM_WIDTH),
                  _resident((1, LANES)), _resident((1, SSM_WIDTH)), _resident((1, SSM_WIDTH)),
                  _resident((SSD_CHUNK, SSD_CHUNK)), _resident((LANES, SSM_WIDTH))],
        out_specs=tile(SSM_WIDTH),
        out_shape=jax.ShapeDtypeStruct((batch, seq, SSM_WIDTH), F32),
        scratch_shapes=[pltpu.VMEM((SSM_STATE, SSM_WIDTH), F32)],
        compiler_params=pltpu.CompilerParams(dimension_semantics=("parallel", "arbitrary"),
                                             vmem_limit_bytes=32 * MIB),
        name="ssd",
    )(xs, bm, cm, dt, z, arow, dskip, norm, tri, expand)


def _out_ffn_kernel(x_ref, attn_ref, y_ref, wo_ref, mixpost_ref, pre_ref, wg_ref, wu_ref, wd_ref, post_ref, o_ref):
    attn = jnp.concatenate([attn_ref[0, j] for j in range(Q_SLABS)], axis=1).astype(BF16)
    mixed = (jnp.dot(attn, wo_ref[:ATTN_WIDTH, :], preferred_element_type=F32)
             + jnp.dot(y_ref[0].astype(BF16), wo_ref[ATTN_WIDTH:, :], preferred_element_type=F32))
    x = x_ref[0] + _rmsnorm(mixed, mixpost_ref[...])
    o_ref[0] = _swiglu_half_step(x, pre_ref[...], wg_ref, wu_ref, wd_ref, post_ref[...])


def _out_ffn(x3d, attn, y3d, wo, mixpost, pre, wg, wu, wd, post):
    batch, seq, _ = x3d.shape
    rows = FFN_ROWS
    row_spec = pl.BlockSpec((1, rows, D_MODEL), lambda b, j: (b, j, 0))
    slab_spec = pl.BlockSpec((1, Q_SLABS, rows, LANES), lambda b, j: (b, 0, j, 0))
    return pl.pallas_call(
        _out_ffn_kernel,
        grid=(batch, seq // rows),
        in_specs=[row_spec, slab_spec, row_spec, _resident((ATTN_WIDTH + SSM_WIDTH, D_MODEL)),
                  _resident((1, D_MODEL)), _resident((1, D_MODEL)), _resident((D_MODEL, D_FF)),
                  _resident((D_MODEL, D_FF)), _resident((D_FF, D_MODEL)), _resident((1, D_MODEL))],
        out_specs=row_spec,
        out_shape=jax.ShapeDtypeStruct((batch, seq, D_MODEL), F32),
        compiler_params=pltpu.CompilerParams(dimension_semantics=("parallel", "parallel"),
                                             vmem_limit_bytes=56 * MIB),
        name="out_ffn",
    )(x3d, attn, y3d, wo, mixpost, pre, wg, wu, wd, post)


def _slab_head_order():
    order = []
    for pair in range(KV_PAIRS):
        for g in range(Q_PER_KV):
            order += [(2 * pair) * Q_PER_KV + g, (2 * pair + 1) * Q_PER_KV + g]
    return order


def _head_columns(order):
    return jnp.concatenate([jnp.arange(h * HEAD_DIM, (h + 1) * HEAD_DIM) for h in order])


def _pad_lanes(v):
    return jnp.pad(v, [(0, 0)] * (v.ndim - 1) + [(0, LANES - v.shape[-1])])


def _layer(x, pos3d, invf, tri, expand, p):
    batch, seq, _ = x.shape
    tokens = batch * seq
    row = lambda v: v.reshape(1, -1)
    bf = lambda w: w.astype(BF16)

    x1 = _ffn(x.reshape(tokens, D_MODEL), row(p["ffn1_pre_norm"]), bf(p["ffn1_w_gate"]), bf(p["ffn1_w_up"]),
              bf(p["ffn1_w_down"]), row(p["ffn1_post_norm"])).reshape(batch, seq, D_MODEL)

    w_in = p["w_in"]
    o_k = ATTN_WIDTH
    o_v = o_k + KV_WIDTH
    o_x = o_v + KV_WIDTH
    o_z = o_x + CONV_CHANNELS
    o_dt = o_z + SSM_WIDTH
    q_cols = _head_columns(_slab_head_order())
    q, k, v, xs, bm, cm, z, dt = _in_proj(
        x1, pos3d, invf, row(p["mix_pre_norm"]),
        bf(w_in[:, :o_k][:, q_cols]), bf(w_in[:, o_k:o_v]), bf(w_in[:, o_v:o_x]),
        bf(w_in[:, o_x:o_z]), bf(w_in[:, o_z:o_dt]), bf(_pad_lanes(w_in[:, o_dt:])),
        p["conv_w"], row(p["conv_b"]), _pad_lanes(row(p["dt_bias"])))

    attn = _attention(q, k, v)

    arow = _pad_lanes(row(-jnp.exp(p["a_log"])))
    dskip = row(jnp.repeat(p["d_skip"], SSM_HEAD_DIM))
    y = _ssd(xs, bm, cm, dt, z, arow, dskip, row(p["ssm_norm"]), tri, expand)

    w_out = p["w_out"]
    w_out = jnp.concatenate([w_out[:ATTN_WIDTH][q_cols], w_out[ATTN_WIDTH:]], axis=0)
    return _out_ffn(x1, attn, y, bf(w_out), row(p["mix_post_norm"]), row(p["ffn2_pre_norm"]),
                    bf(p["ffn2_w_gate"]), bf(p["ffn2_w_up"]), bf(p["ffn2_w_down"]), row(p["ffn2_post_norm"]))


def kernel(x, positions, ffn1_pre_norm, ffn1_w_gate, ffn1_w_up, ffn1_w_down, ffn1_post_norm, mix_pre_norm, w_in, conv_w, conv_b, dt_bias, a_log, d_skip, ssm_norm, w_out, mix_post_norm, ffn2_pre_norm, ffn2_w_gate, ffn2_w_up, ffn2_w_down, ffn2_post_norm):
    params = dict(ffn1_pre_norm=ffn1_pre_norm, ffn1_w_gate=ffn1_w_gate, ffn1_w_up=ffn1_w_up,
                  ffn1_w_down=ffn1_w_down, ffn1_post_norm=ffn1_post_norm, mix_pre_norm=mix_pre_norm, w_in=w_in,
                  conv_w=conv_w, conv_b=conv_b, dt_bias=dt_bias, a_log=a_log, d_skip=d_skip, ssm_norm=ssm_norm,
                  w_out=w_out, mix_post_norm=mix_post_norm, ffn2_pre_norm=ffn2_pre_norm, ffn2_w_gate=ffn2_w_gate,
                  ffn2_w_up=ffn2_w_up, ffn2_w_down=ffn2_w_down, ffn2_post_norm=ffn2_post_norm)
    depth = w_in.shape[0]
    batch, seq, _ = x.shape
    inv_freq = ROPE_THETA ** (-jnp.arange(0, HEAD_DIM, 2, dtype=F32) / HEAD_DIM)
    invf = jnp.tile(inv_freq, LANES // (HEAD_DIM // 2)).reshape(1, LANES)
    pos3d = positions.reshape(batch, seq, 1)
    idx = jnp.arange(SSD_CHUNK)
    tri = (idx[:, None] >= idx[None, :]).astype(F32)
    expand = (jnp.arange(LANES)[:, None] == jnp.arange(SSM_WIDTH)[None, :] // SSM_HEAD_DIM).astype(BF16)
    for i in range(depth):
        x = _layer(x, pos3d, invf, tri, expand, {name: w[i] for name, w in params.items()})
    return x
```

```python
import functools
import math

import jax
import jax.numpy as jnp
from jax import lax
from jax.experimental import pallas as pl
from jax.experimental.pallas import tpu as pltpu

F32 = jnp.float32
BF16 = jnp.bfloat16

D_MODEL = 1024
D_FF = 2816
HEAD_DIM = 64
N_Q_HEADS = 16
N_KV_HEADS = 4
Q_PER_KV = N_Q_HEADS // N_KV_HEADS
ATTN_WIDTH = N_Q_HEADS * HEAD_DIM
KV_WIDTH = N_KV_HEADS * HEAD_DIM
ATTN_BLOCK = 128
ROPE_THETA = 10000.0
SSM_HEADS = 16
SSM_HEAD_DIM = 64
SSM_WIDTH = SSM_HEADS * SSM_HEAD_DIM
SSM_STATE = 128
SSM_GROUPS = 2
GROUP_WIDTH = SSM_WIDTH // SSM_GROUPS
BC_WIDTH = SSM_GROUPS * SSM_STATE
CONV_WIDTH = 4
CONV_CHANNELS = SSM_WIDTH + 2 * BC_WIDTH
SSD_CHUNK = 128
MACARON_WEIGHT = 0.5
NORM_EPS = 1e-6
LOG2_E = math.log2(math.e)

LANES = 128
CARRY_ROWS = 8
MIB = 1024 * 1024

FFN_ROWS = 512
FF_CHUNK = 256
PROJ_ROWS = 512

Q_SLABS = ATTN_WIDTH // LANES
KV_SLABS = KV_WIDTH // LANES
KV_PAIRS = N_KV_HEADS // 2


def _rmsnorm(x, gain):
    return x * lax.rsqrt(jnp.mean(x * x, axis=-1, keepdims=True) + NORM_EPS) * gain


def _resident(shape):
    return pl.BlockSpec(shape, lambda *_: (0,) * len(shape), pipeline_mode=pl.Buffered(1))


def _swiglu_half_step(x, pre, wg_ref, wu_ref, wd_ref, post):
    xn = _rmsnorm(x, pre).astype(BF16)
    acc = jnp.zeros(x.shape, F32)
    for c in range(D_FF // FF_CHUNK):
        cols = slice(c * FF_CHUNK, (c + 1) * FF_CHUNK)
        gate = jnp.dot(xn, wg_ref[:, cols], preferred_element_type=F32)
        up = jnp.dot(xn, wu_ref[:, cols], preferred_element_type=F32)
        hidden = (jax.nn.silu(gate) * up).astype(BF16)
        acc = acc + jnp.dot(hidden, wd_ref[cols, :], preferred_element_type=F32)
    return x + MACARON_WEIGHT * _rmsnorm(acc, post)


def _ffn_kernel(x_ref, pre_ref, wg_ref, wu_ref, wd_ref, post_ref, o_ref):
    o_ref[...] = _swiglu_half_step(x_ref[...], pre_ref[...], wg_ref, wu_ref, wd_ref, post_ref[...])


def _ffn(x2d, pre, wg, wu, wd, post):
    tokens = x2d.shape[0]
    row_spec = pl.BlockSpec((FFN_ROWS, D_MODEL), lambda i: (i, 0))
    return pl.pallas_call(
        _ffn_kernel,
        grid=(tokens // FFN_ROWS,),
        in_specs=[row_spec, _resident((1, D_MODEL)), _resident((D_MODEL, D_FF)), _resident((D_MODEL, D_FF)),
                  _resident((D_FF, D_MODEL)), _resident((1, D_MODEL))],
        out_specs=row_spec,
        out_shape=jax.ShapeDtypeStruct((tokens, D_MODEL), F32),
        compiler_params=pltpu.CompilerParams(dimension_semantics=("parallel",), vmem_limit_bytes=48 * MIB),
        name="ffn",
    )(x2d, pre, wg, wu, wd, post)


def _softplus(x):
    return jnp.maximum(x, 0.0) + jnp.log1p(jnp.exp(-jnp.abs(x)))


def _inproj_kernel(x_ref, pos_ref, invf_ref, gain_ref, wq_ref, wk_ref, wv_ref, wx_ref, wz_ref, wdt_ref,
                   convw_ref, convb_ref, dtb_ref,
                   q_ref, k_ref, v_ref, xs_ref, b_ref, c_ref, z_ref, dt_ref, ubuf_ref):
    rows = x_ref.shape[1]
    xn = _rmsnorm(x_ref[0], gain_ref[...]).astype(BF16)

    ang = pos_ref[0].astype(F32) * invf_ref[...]
    lane = lax.broadcasted_iota(jnp.int32, (rows, LANES), 1)
    first_half = (lane % HEAD_DIM) < (HEAD_DIM // 2)
    cos = jnp.cos(ang)
    sin = jnp.where(first_half, -jnp.sin(ang), jnp.sin(ang))

    def rope(t):
        back = pltpu.roll(t, HEAD_DIM // 2, 1)
        fwd = pltpu.roll(t, LANES - HEAD_DIM // 2, 1)
        return t * cos + jnp.where(first_half, fwd, back) * sin

    q = jnp.dot(xn, wq_ref[...], preferred_element_type=F32)
    for j in range(Q_SLABS):
        q_ref[0, j] = rope(q[:, j * LANES:(j + 1) * LANES]) * (HEAD_DIM ** -0.5 * LOG2_E)
    k = jnp.dot(xn, wk_ref[...], preferred_element_type=F32)
    v = jnp.dot(xn, wv_ref[...], preferred_element_type=F32)
    for j in range(KV_SLABS):
        k_ref[0, j] = rope(k[:, j * LANES:(j + 1) * LANES])
        v_ref[0, j] = v[:, j * LANES:(j + 1) * LANES]
    z_ref[0] = jnp.dot(xn, wz_ref[...], preferred_element_type=F32)
    dt_ref[0] = _softplus(jnp.dot(xn, wdt_ref[...], preferred_element_type=F32) + dtb_ref[...])

    @pl.when(pl.program_id(1) == 0)
    def _():
        ubuf_ref[0:CARRY_ROWS, :] = jnp.zeros((CARRY_ROWS, CONV_CHANNELS), F32)

    ubuf_ref[CARRY_ROWS:CARRY_ROWS + rows, :] = jnp.dot(xn, wx_ref[...], preferred_element_type=F32)
    conv = convb_ref[...] + jnp.zeros((rows, CONV_CHANNELS), F32)
    for tap in range(CONV_WIDTH):
        start = CARRY_ROWS - (CONV_WIDTH - 1) + tap
        conv = conv + ubuf_ref[start:start + rows, :] * convw_ref[tap:tap + 1, :]
    act = jax.nn.silu(conv)
    xs_ref[0] = act[:, :SSM_WIDTH]
    b_ref[0] = act[:, SSM_WIDTH:SSM_WIDTH + BC_WIDTH].astype(BF16)
    c_ref[0] = act[:, SSM_WIDTH + BC_WIDTH:].astype(BF16)
    ubuf_ref[0:CARRY_ROWS, :] = ubuf_ref[rows:rows + CARRY_ROWS, :]


def _in_proj(x3d, pos3d, invf, gain, wq, wk, wv, wx, wz, wdt, convw, convb, dtb):
    batch, seq, _ = x3d.shape
    rows = PROJ_ROWS

    def tile(width):
        return pl.BlockSpec((1, rows, width), lambda b, j: (b, j, 0))

    def slabs(n):
        return pl.BlockSpec((1, n, rows, LANES), lambda b, j: (b, 0, j, 0))

    def out(width, dtype):
        return jax.ShapeDtypeStruct((batch, seq, width), dtype)

    def out_slabs(n):
        return jax.ShapeDtypeStruct((batch, n, seq, LANES), F32)

    return pl.pallas_call(
        _inproj_kernel,
        grid=(batch, seq // rows),
        in_specs=[tile(D_MODEL), tile(1), _resident((1, LANES)), _resident((1, D_MODEL)),
                  _resident((D_MODEL, ATTN_WIDTH)), _resident((D_MODEL, KV_WIDTH)), _resident((D_MODEL, KV_WIDTH)),
                  _resident((D_MODEL, CONV_CHANNELS)), _resident((D_MODEL, SSM_WIDTH)),
                  _resident((D_MODEL, LANES)), _resident((CONV_WIDTH, CONV_CHANNELS)),
                  _resident((1, CONV_CHANNELS)), _resident((1, LANES))],
        out_specs=[slabs(Q_SLABS), slabs(KV_SLABS), slabs(KV_SLABS), tile(SSM_WIDTH), tile(BC_WIDTH),
                   tile(BC_WIDTH), tile(SSM_WIDTH), tile(LANES)],
        out_shape=[out_slabs(Q_SLABS), out_slabs(KV_SLABS), out_slabs(KV_SLABS), out(SSM_WIDTH, F32),
                   out(BC_WIDTH, BF16), out(BC_WIDTH, BF16), out(SSM_WIDTH, F32), out(LANES, F32)],
        scratch_shapes=[pltpu.VMEM((rows + CARRY_ROWS, CONV_CHANNELS), F32)],
        compiler_params=pltpu.CompilerParams(dimension_semantics=("parallel", "arbitrary"),
                                             vmem_limit_bytes=56 * MIB),
        name="in_proj",
    )(x3d, pos3d, invf, gain, wq, wk, wv, wx, wz, wdt, convw, convb, dtb)


STATE_ARRAYS = 3
MASKED = -2.0 ** 100
STATE_U, STATE_M, STATE_L = range(STATE_ARRAYS)


def _attn_blocks(blocks):
    blk = ATTN_BLOCK
    low = lax.broadcasted_iota(jnp.int32, (blk, LANES), 1) < HEAD_DIM
    zero = jnp.zeros((), BF16)
    one = jnp.ones((), BF16)

    scores = []
    for q_slabs, k_blk, _, (query_onehot, key_mask), _ in blocks:
        qb = [q.astype(BF16) for q in q_slabs]
        keys = jnp.concatenate([k_blk.astype(BF16), key_mask], axis=1)
        for masked in ([jnp.where(low, q, zero) for q in qb], [jnp.where(low, zero, q) for q in qb]):
            lhs = jnp.concatenate([jnp.concatenate(masked, axis=0), query_onehot], axis=1)
            scores.append(lax.dot_general(lhs, keys, (((1,), (1,)), ((), ())), preferred_element_type=F32))
    maxes = [jnp.max(s, axis=-1, keepdims=True) for s in scores]
    probs = [jnp.exp2(s - m).astype(BF16) for s, m in zip(scores, maxes)]
    results = []
    for i, (_, _, v_blk, _, _) in enumerate(blocks):
        vb = v_blk.astype(BF16)
        key_low = lax.broadcasted_iota(jnp.int32, vb.shape, 1) < HEAD_DIM
        results.append(jnp.dot(probs[2 * i], jnp.where(key_low, vb, one), preferred_element_type=F32))
        results.append(jnp.dot(probs[2 * i + 1], jnp.where(key_low, one, vb), preferred_element_type=F32))
    states = []
    for i, (q_slabs, _, _, _, load_old) in enumerate(blocks):
        r_low, r_high, m_a, m_b = results[2 * i], results[2 * i + 1], maxes[2 * i], maxes[2 * i + 1]
        old = None if load_old is None else load_old()
        state = []
        for g in range(len(q_slabs)):
            rows = slice(g * blk, (g + 1) * blk)
            u_new = jnp.where(low, r_low[rows], r_high[rows])
            l_new = pltpu.roll(jnp.where(low, r_high[rows], r_low[rows]), HEAD_DIM, 1)
            m_new = jnp.where(low, m_a[rows], m_b[rows])
            if old is not None:
                u_run, m_run, l_run = old[g]
                m_tot = jnp.maximum(m_run, m_new)
                a = jnp.exp2(m_run - m_tot)
                b = jnp.exp2(m_new - m_tot)
                u_new = u_run * a + u_new * b
                l_new = l_run * a + l_new * b
                m_new = m_tot
            state.append((u_new, m_new, l_new))
        states.append(state)
    return states


def _attn_kernel(q_ref, k_ref, v_ref, o_ref, st4_ref, stt_ref):
    blk = ATTN_BLOCK
    seq = q_ref.shape[2]
    n_slabs = q_ref.shape[1]
    sub4 = seq // 4

    def key_mask(n_keys, shift):
        si = lax.broadcasted_iota(jnp.int32, (n_keys, blk), 0)
        qi = lax.broadcasted_iota(jnp.int32, (n_keys, blk), 1) + shift
        return jnp.where((si <= qi) & (si >= qi - blk), 0.0, MASKED).astype(BF16)

    stacked = n_slabs * blk
    query_onehot = (lax.broadcasted_iota(jnp.int32, (stacked, blk), 0) % blk
                    == lax.broadcasted_iota(jnp.int32, (stacked, blk), 1)).astype(BF16)
    band_bias = (query_onehot, key_mask(2 * blk, blk))
    causal_bias = (query_onehot, key_mask(blk, 0))

    def load_q(rows):
        return [q_ref[0, g, rows, :] for g in range(n_slabs)]

    def load_state(ref, rows):
        return [tuple(ref[a * n_slabs + g, rows, :] for a in range(STATE_ARRAYS)) for g in range(n_slabs)]

    def store_state(ref, rows, state):
        for g in range(n_slabs):
            for a in range(STATE_ARRAYS):
                ref[a * n_slabs + g, rows, :] = state[g][a]

    def store_out(rows, state):
        for g in range(n_slabs):
            o_ref[0, g, rows, :] = state[g][STATE_U] / state[g][STATE_L]

    def run(plans):
        states = _attn_blocks([(load_q(rows), k_ref[0, 0, keys, :], v_ref[0, 0, keys, :], bias, load_old)
                               for rows, keys, bias, load_old, _ in plans])
        for (rows, _, _, _, store), state in zip(plans, states):
            store(rows, state)

    def paired(n_blocks, plan):
        def trip(i, carry):
            run([plan(2 * i), plan(2 * i + 1)])
            return carry
        lax.fori_loop(0, n_blocks // 2, trip, 0)

    def d16_plan(r16):
        rows = pl.ds(r16, blk, stride=16)
        dest = pl.ds((r16 % 4) * sub4 + r16 // 4, blk, stride=4)
        return rows, rows, causal_bias, None, lambda _, state: store_state(st4_ref, dest, state)

    paired(16, d16_plan)

    def to_token_order(rows, state):
        store_state(stt_ref, rows, state)

    def d4_first_plan(r4):
        rows = pl.ds(r4, blk, stride=4)
        src = pl.ds(pl.multiple_of(r4 * sub4, blk), blk)
        return rows, rows, causal_bias, lambda: load_state(st4_ref, src), to_token_order

    paired(4, d4_first_plan)
    later_blocks = sub4 // blk - 1

    def d4_plan(idx):
        r4 = idx // later_blocks
        n = 1 + idx % later_blocks
        rows = pl.ds(r4 + 4 * blk * n, blk, stride=4)
        keys = pl.ds(r4 + 4 * blk * (n - 1), 2 * blk, stride=4)
        src = pl.ds(pl.multiple_of(r4 * sub4 + n * blk, blk), blk)
        return rows, keys, band_bias, lambda: load_state(st4_ref, src), to_token_order

    paired(4 * later_blocks, d4_plan)

    def d1_plan(n):
        start = n * blk if isinstance(n, int) else pl.multiple_of(n * blk, blk)
        rows = pl.ds(start, blk)
        if isinstance(n, int) and n == 0:
            return rows, rows, causal_bias, lambda: load_state(stt_ref, rows), store_out
        return rows, pl.ds(start - blk, 2 * blk), band_bias, lambda: load_state(stt_ref, rows), store_out

    last = seq // blk - 1
    run([d1_plan(0), d1_plan(last)])
    paired(last - 1, lambda i: d1_plan(i + 1))


def _attention(q, k, v):
    batch, _, seq, _ = q.shape
    slabs_per_pair = Q_SLABS // KV_PAIRS
    q_spec = pl.BlockSpec((1, slabs_per_pair, seq, LANES), lambda b, p: (b, p, 0, 0))
    kv_spec = pl.BlockSpec((1, 1, seq, LANES), lambda b, p: (b, p, 0, 0))
    state_shape = (STATE_ARRAYS * slabs_per_pair, seq, LANES)
    return pl.pallas_call(
        _attn_kernel,
        grid=(batch, KV_PAIRS),
        in_specs=[q_spec, kv_spec, kv_spec],
        out_specs=q_spec,
        out_shape=jax.ShapeDtypeStruct(q.shape, F32),
        scratch_shapes=[pltpu.VMEM(state_shape, F32), pltpu.VMEM(state_shape, F32)],
        compiler_params=pltpu.CompilerParams(dimension_semantics=("parallel", "parallel"),
                                             vmem_limit_bytes=56 * MIB),
        name="attention",
    )(q, k, v)


def _split_bf16(x):
    hi = x.astype(BF16)
    lo = (x - hi.astype(F32)).astype(BF16)
    return hi, lo


def _ssd_kernel(xs_ref, b_ref, c_ref, dt_ref, z_ref, arow_ref, dskip_ref, norm_ref, tri_ref, expand_ref,
                y_ref, state_ref):
    chunk = SSD_CHUNK

    @pl.when(pl.program_id(1) == 0)
    def _():
        state_ref[...] = jnp.zeros(state_ref.shape, F32)

    xs = xs_ref[0]
    dt = dt_ref[0]
    a_dt = dt * arow_ref[...]
    a_cs = jnp.dot(tri_ref[...], a_dt, preferred_element_type=F32, precision=lax.Precision.HIGHEST)
    a_last = a_cs[chunk - 1:chunk, :]
    decay_to_end = jnp.exp(a_last - a_cs)
    decay_from_start = jnp.exp(a_cs)

    def expand(stat):
        hi, lo = _split_bf16(stat)
        return (jnp.dot(hi, expand_ref[...], preferred_element_type=F32)
                + jnp.dot(lo, expand_ref[...], preferred_element_type=F32))

    w_state = expand(decay_to_end * dt)
    scale_off = expand(decay_from_start)
    chunk_decay = scale_off[chunk - 1:chunk, :]

    a_cs_t = a_cs.T
    dt_t = dt.T
    row = lax.broadcasted_iota(jnp.int32, (chunk, chunk), 0)
    col = lax.broadcasted_iota(jnp.int32, (chunk, chunk), 1)
    causal = row >= col
    low = lax.broadcasted_iota(jnp.int32, (chunk, LANES), 1) < SSM_HEAD_DIM
    zero = jnp.zeros((), BF16)

    xs_bf = xs.astype(BF16)
    xw_bf = (xs * w_state).astype(BF16)
    heads_per_group = SSM_HEADS // SSM_GROUPS
    y_parts = []
    for g in range(SSM_GROUPS):
        gcols = slice(g * SSM_STATE, (g + 1) * SSM_STATE)
        b_g = b_ref[0, :, gcols]
        c_g = c_ref[0, :, gcols]
        cb = lax.dot_general(c_g, b_g, (((1,), (1,)), ((), ())), preferred_element_type=F32)
        wide = slice(g * GROUP_WIDTH, (g + 1) * GROUP_WIDTH)
        state_g = state_ref[:, wide]
        y_off = jnp.dot(c_g, state_g.astype(BF16), preferred_element_type=F32) * scale_off[:, wide]
        diag_parts = []
        for pair in range(heads_per_group // 2):
            mats = []
            for e in range(2):
                h = g * heads_per_group + 2 * pair + e
                seg = a_cs[:, h:h + 1] - a_cs_t[h:h + 1, :]
                decay = jnp.exp(jnp.where(causal, seg, -jnp.inf))
                mats.append((cb * decay * dt_t[h:h + 1, :]).astype(BF16))
            pcols = slice((g * heads_per_group + 2 * pair) * SSM_HEAD_DIM,
                          (g * heads_per_group + 2 * pair + 2) * SSM_HEAD_DIM)
            x_pair = xs_bf[:, pcols]
            rhs = jnp.concatenate([jnp.where(low, x_pair, zero), jnp.where(low, zero, x_pair)], axis=0)
            diag_parts.append(jnp.dot(jnp.concatenate(mats, axis=1), rhs, preferred_element_type=F32))
        y_parts.append(jnp.concatenate(diag_parts, axis=1) + y_off)
        b_t = b_g.astype(F32).T.astype(BF16)
        new_states = jnp.dot(b_t, xw_bf[:, wide], preferred_element_type=F32)
        state_ref[:, wide] = state_g * chunk_decay[:, wide] + new_states

    y = jnp.concatenate(y_parts, axis=1) + dskip_ref[...] * xs
    y = y * jax.nn.silu(z_ref[0])
    normed = []
    for g in range(SSM_GROUPS):
        y_g = y[:, g * GROUP_WIDTH:(g + 1) * GROUP_WIDTH]
        normed.append(y_g * lax.rsqrt(jnp.mean(y_g * y_g, axis=-1, keepdims=True) + NORM_EPS))
    y_ref[0] = jnp.concatenate(normed, axis=1) * norm_ref[...]


def _ssd(xs, bm, cm, dt, z, arow, dskip, norm, tri, expand):
    batch, seq, _ = xs.shape

    def tile(width):
        return pl.BlockSpec((1, SSD_CHUNK, width), lambda b, c: (b, c, 0))

    return pl.pallas_call(
        _ssd_kernel,
        grid=(batch, seq // SSD_CHUNK),
        in_specs=[tile(SSM_WIDTH), tile(BC_WIDTH), tile(BC_WIDTH), tile(LANES), tile(SSM_WIDTH),
                  _resident((1, LANES)), _resident((1, SSM_WIDTH)), _resident((1, SSM_WIDTH)),
                  _resident((SSD_CHUNK, SSD_CHUNK)), _resident((LANES, SSM_WIDTH))],
        out_specs=tile(SSM_WIDTH),
        out_shape=jax.ShapeDtypeStruct((batch, seq, SSM_WIDTH), F32),
        scratch_shapes=[pltpu.VMEM((SSM_STATE, SSM_WIDTH), F32)],
        compiler_params=pltpu.CompilerParams(dimension_semantics=("parallel", "arbitrary"),
                                             vmem_limit_bytes=32 * MIB),
        name="ssd",
    )(xs, bm, cm, dt, z, arow, dskip, norm, tri, expand)


def _out_ffn_kernel(x_ref, attn_ref, y_ref, wo_ref, mixpost_ref, pre_ref, wg_ref, wu_ref, wd_ref, post_ref, o_ref):
    attn = jnp.concatenate([attn_ref[0, j] for j in range(Q_SLABS)], axis=1).astype(BF16)
    mixed = (jnp.dot(attn, wo_ref[:ATTN_WIDTH, :], preferred_element_type=F32)
             + jnp.dot(y_ref[0].astype(BF16), wo_ref[ATTN_WIDTH:, :], preferred_element_type=F32))
    x = x_ref[0] + _rmsnorm(mixed, mixpost_ref[...])
    o_ref[0] = _swiglu_half_step(x, pre_ref[...], wg_ref, wu_ref, wd_ref, post_ref[...])


def _out_ffn(x3d, attn, y3d, wo, mixpost, pre, wg, wu, wd, post):
    batch, seq, _ = x3d.shape
    rows = FFN_ROWS
    row_spec = pl.BlockSpec((1, rows, D_MODEL), lambda b, j: (b, j, 0))
    slab_spec = pl.BlockSpec((1, Q_SLABS, rows, LANES), lambda b, j: (b, 0, j, 0))
    return pl.pallas_call(
        _out_ffn_kernel,
        grid=(batch, seq // rows),
        in_specs=[row_spec, slab_spec, row_spec, _resident((ATTN_WIDTH + SSM_WIDTH, D_MODEL)),
                  _resident((1, D_MODEL)), _resident((1, D_MODEL)), _resident((D_MODEL, D_FF)),
                  _resident((D_MODEL, D_FF)), _resident((D_FF, D_MODEL)), _resident((1, D_MODEL))],
        out_specs=row_spec,
        out_shape=jax.ShapeDtypeStruct((batch, seq, D_MODEL), F32),
        compiler_params=pltpu.CompilerParams(dimension_semantics=("parallel", "parallel"),
                                             vmem_limit_bytes=56 * MIB),
        name="out_ffn",
    )(x3d, attn, y3d, wo, mixpost, pre, wg, wu, wd, post)


def _slab_head_order():
    order = []
    for pair in range(KV_PAIRS):
        for g in range(Q_PER_KV):
            order += [(2 * pair) * Q_PER_KV + g, (2 * pair + 1) * Q_PER_KV + g]
    return order


def _head_columns(order):
    return jnp.concatenate([jnp.arange(h * HEAD_DIM, (h + 1) * HEAD_DIM) for h in order])


def _pad_lanes(v):
    return jnp.pad(v, [(0, 0)] * (v.ndim - 1) + [(0, LANES - v.shape[-1])])


def _layer(x, pos3d, invf, tri, expand, p):
    batch, seq, _ = x.shape
    tokens = batch * seq
    row = lambda v: v.reshape(1, -1)
    bf = lambda w: w.astype(BF16)

    x1 = _ffn(x.reshape(tokens, D_MODEL), row(p["ffn1_pre_norm"]), bf(p["ffn1_w_gate"]), bf(p["ffn1_w_up"]),
              bf(p["ffn1_w_down"]), row(p["ffn1_post_norm"])).reshape(batch, seq, D_MODEL)

    w_in = p["w_in"]
    o_k = ATTN_WIDTH
    o_v = o_k + KV_WIDTH
    o_x = o_v + KV_WIDTH
    o_z = o_x + CONV_CHANNELS
    o_dt = o_z + SSM_WIDTH
    q_cols = _head_columns(_slab_head_order())
    q, k, v, xs, bm, cm, z, dt = _in_proj(
        x1, pos3d, invf, row(p["mix_pre_norm"]),
        bf(w_in[:, :o_k][:, q_cols]), bf(w_in[:, o_k:o_v]), bf(w_in[:, o_v:o_x]),
        bf(w_in[:, o_x:o_z]), bf(w_in[:, o_z:o_dt]), bf(_pad_lanes(w_in[:, o_dt:])),
        p["conv_w"], row(p["conv_b"]), _pad_lanes(row(p["dt_bias"])))

    attn = _attention(q, k, v)

    arow = _pad_lanes(row(-jnp.exp(p["a_log"])))
    dskip = row(jnp.repeat(p["d_skip"], SSM_HEAD_DIM))
    y = _ssd(xs, bm, cm, dt, z, arow, dskip, row(p["ssm_norm"]), tri, expand)

    w_out = p["w_out"]
    w_out = jnp.concatenate([w_out[:ATTN_WIDTH][q_cols], w_out[ATTN_WIDTH:]], axis=0)
    return _out_ffn(x1, attn, y, bf(w_out), row(p["mix_post_norm"]), row(p["ffn2_pre_norm"]),
                    bf(p["ffn2_w_gate"]), bf(p["ffn2_w_up"]), bf(p["ffn2_w_down"]), row(p["ffn2_post_norm"]))


def kernel(x, positions, ffn1_pre_norm, ffn1_w_gate, ffn1_w_up, ffn1_w_down, ffn1_post_norm, mix_pre_norm, w_in, conv_w, conv_b, dt_bias, a_log, d_skip, ssm_norm, w_out, mix_post_norm, ffn2_pre_norm, ffn2_w_gate, ffn2_w_up, ffn2_w_down, ffn2_post_norm):
    params = dict(ffn1_pre_norm=ffn1_pre_norm, ffn1_w_gate=ffn1_w_gate, ffn1_w_up=ffn1_w_up,
                  ffn1_w_down=ffn1_w_down, ffn1_post_norm=ffn1_post_norm, mix_pre_norm=mix_pre_norm, w_in=w_in,
                  conv_w=conv_w, conv_b=conv_b, dt_bias=dt_bias, a_log=a_log, d_skip=d_skip, ssm_norm=ssm_norm,
                  w_out=w_out, mix_post_norm=mix_post_norm, ffn2_pre_norm=ffn2_pre_norm, ffn2_w_gate=ffn2_w_gate,
                  ffn2_w_up=ffn2_w_up, ffn2_w_down=ffn2_w_down, ffn2_post_norm=ffn2_post_norm)
    depth = w_in.shape[0]
    batch, seq, _ = x.shape
    inv_freq = ROPE_THETA ** (-jnp.arange(0, HEAD_DIM, 2, dtype=F32) / HEAD_DIM)
    invf = jnp.tile(inv_freq, LANES // (HEAD_DIM // 2)).reshape(1, LANES)
    pos3d = positions.reshape(batch, seq, 1)
    idx = jnp.arange(SSD_CHUNK)
    tri = (idx[:, None] >= idx[None, :]).astype(F32)
    expand = (jnp.arange(LANES)[:, None] == jnp.arange(SSM_WIDTH)[None, :] // SSM_HEAD_DIM).astype(BF16)
    for i in range(depth):
        x = _layer(x, pos3d, invf, tri, expand, {name: w[i] for name, w in params.items()})
    return x
```

```python
import functools
import math

import jax
import jax.numpy as jnp
from jax import lax
from jax.experimental import pallas as pl
from jax.experimental.pallas import tpu as pltpu

F32 = jnp.float32
BF16 = jnp.bfloat16

D_MODEL = 1024
D_FF = 2816
HEAD_DIM = 64
N_Q_HEADS = 16
N_KV_HEADS = 4
Q_PER_KV = N_Q_HEADS // N_KV_HEADS
ATTN_WIDTH = N_Q_HEADS * HEAD_DIM
KV_WIDTH = N_KV_HEADS * HEAD_DIM
ATTN_BLOCK = 128
ROPE_THETA = 10000.0
SSM_HEADS = 16
SSM_HEAD_DIM = 64
SSM_WIDTH = SSM_HEADS * SSM_HEAD_DIM
SSM_STATE = 128
SSM_GROUPS = 2
GROUP_WIDTH = SSM_WIDTH // SSM_GROUPS
BC_WIDTH = SSM_GROUPS * SSM_STATE
CONV_WIDTH = 4
CONV_CHANNELS = SSM_WIDTH + 2 * BC_WIDTH
SSD_CHUNK = 128
MACARON_WEIGHT = 0.5
NORM_EPS = 1e-6
LOG2_E = math.log2(math.e)

LANES = 128
CARRY_ROWS = 8
MIB = 1024 * 1024

FFN_ROWS = 512
FF_CHUNK = 256
PROJ_ROWS = 512

Q_SLABS = ATTN_WIDTH // LANES
KV_SLABS = KV_WIDTH // LANES
KV_PAIRS = N_KV_HEADS // 2


def _rmsnorm(x, gain):
    return x * lax.rsqrt(jnp.mean(x * x, axis=-1, keepdims=True) + NORM_EPS) * gain


def _resident(shape):
    return pl.BlockSpec(shape, lambda *_: (0,) * len(shape), pipeline_mode=pl.Buffered(1))


def _swiglu_half_step(x, pre, wg_ref, wu_ref, wd_ref, post):
    xn = _rmsnorm(x, pre).astype(BF16)
    acc = jnp.zeros(x.shape, F32)
    for c in range(D_FF // FF_CHUNK):
        cols = slice(c * FF_CHUNK, (c + 1) * FF_CHUNK)
        gate = jnp.dot(xn, wg_ref[:, cols], preferred_element_type=F32)
        up = jnp.dot(xn, wu_ref[:, cols], preferred_element_type=F32)
        hidden = (jax.nn.silu(gate) * up).astype(BF16)
        acc = acc + jnp.dot(hidden, wd_ref[cols, :], preferred_element_type=F32)
    return x + MACARON_WEIGHT * _rmsnorm(acc, post)


def _ffn_kernel(x_ref, pre_ref, wg_ref, wu_ref, wd_ref, post_ref, o_ref):
    o_ref[...] = _swiglu_half_step(x_ref[...], pre_ref[...], wg_ref, wu_ref, wd_ref, post_ref[...])


def _ffn(x2d, pre, wg, wu, wd, post):
    tokens = x2d.shape[0]
    row_spec = pl.BlockSpec((FFN_ROWS, D_MODEL), lambda i: (i, 0))
    return pl.pallas_call(
        _ffn_kernel,
        grid=(tokens // FFN_ROWS,),
        in_specs=[row_spec, _resident((1, D_MODEL)), _resident((D_MODEL, D_FF)), _resident((D_MODEL, D_FF)),
                  _resident((D_FF, D_MODEL)), _resident((1, D_MODEL))],
        out_specs=row_spec,
        out_shape=jax.ShapeDtypeStruct((tokens, D_MODEL), F32),
        compiler_params=pltpu.CompilerParams(dimension_semantics=("parallel",), vmem_limit_bytes=48 * MIB),
        name="ffn",
    )(x2d, pre, wg, wu, wd, post)


def _softplus(x):
    return jnp.maximum(x, 0.0) + jnp.log1p(jnp.exp(-jnp.abs(x)))


def _inproj_kernel(x_ref, pos_ref, invf_ref, gain_ref, wq_ref, wk_ref, wv_ref, wx_ref, wz_ref, wdt_ref,
                   convw_ref, convb_ref, dtb_ref,
                   q_ref, k_ref, v_ref, xs_ref, b_ref, c_ref, z_ref, dt_ref, ubuf_ref):
    rows = x_ref.shape[1]
    xn = _rmsnorm(x_ref[0], gain_ref[...]).astype(BF16)

    ang_t = invf_ref[...] * pos_ref[0].astype(F32)
    cos_t = jnp.cos(ang_t)
    sin_t = jnp.sin(ang_t)
    copies = LANES // HEAD_DIM
    cos = jnp.concatenate([cos_t, cos_t] * copies, axis=0).T
    sin = jnp.concatenate([-sin_t, sin_t] * copies, axis=0).T
    lane = lax.broadcasted_iota(jnp.int32, (rows, LANES), 1)
    first_half = (lane % HEAD_DIM) < (HEAD_DIM // 2)

    def rope(t):
        back = pltpu.roll(t, HEAD_DIM // 2, 1)
        fwd = pltpu.roll(t, LANES - HEAD_DIM // 2, 1)
        return t * cos + jnp.where(first_half, fwd, back) * sin

    q = jnp.dot(xn, wq_ref[...], preferred_element_type=F32)
    for j in range(Q_SLABS):
        q_ref[0, j] = rope(q[:, j * LANES:(j + 1) * LANES]) * (HEAD_DIM ** -0.5 * LOG2_E)
    k = jnp.dot(xn, wk_ref[...], preferred_element_type=F32)
    v = jnp.dot(xn, wv_ref[...], preferred_element_type=F32)
    for j in range(KV_SLABS):
        k_ref[0, j] = rope(k[:, j * LANES:(j + 1) * LANES])
        v_ref[0, j] = v[:, j * LANES:(j + 1) * LANES]
    z_ref[0] = jnp.dot(xn, wz_ref[...], preferred_element_type=F32)
    dt_ref[0] = _softplus(jnp.dot(xn, wdt_ref[...], preferred_element_type=F32) + dtb_ref[...])

    n_slabs = CONV_CHANNELS // LANES

    @pl.when(pl.program_id(1) == 0)
    def _():
        ubuf_ref[:, 0:CARRY_ROWS, :] = jnp.zeros((n_slabs, CARRY_ROWS, LANES), F32)

    u = jnp.dot(xn, wx_ref[...], preferred_element_type=F32)
    for j in range(n_slabs):
        cols = slice(j * LANES, (j + 1) * LANES)
        ubuf_ref[j, CARRY_ROWS:CARRY_ROWS + rows, :] = u[:, cols]
        conv = convb_ref[:, cols] + ubuf_ref[j, CARRY_ROWS:CARRY_ROWS + rows, :] * convw_ref[CONV_WIDTH - 1:CONV_WIDTH, cols]
        for tap in range(CONV_WIDTH - 1):
            start = CARRY_ROWS - (CONV_WIDTH - 1) + tap
            conv = conv + ubuf_ref[j, start:start + rows, :] * convw_ref[tap:tap + 1, cols]
        act = jax.nn.silu(conv)
        if j < SSM_WIDTH // LANES:
            xs_ref[0, :, cols] = act
        elif j < (SSM_WIDTH + BC_WIDTH) // LANES:
            b_ref[0, :, j * LANES - SSM_WIDTH:(j + 1) * LANES - SSM_WIDTH] = act.astype(BF16)
        else:
            first = SSM_WIDTH + BC_WIDTH
            c_ref[0, :, j * LANES - first:(j + 1) * LANES - first] = act.astype(BF16)
        ubuf_ref[j, 0:CARRY_ROWS, :] = ubuf_ref[j, rows:rows + CARRY_ROWS, :]


def _in_proj(x3d, pos3d, invf, gain, wq, wk, wv, wx, wz, wdt, convw, convb, dtb):
    batch, seq, _ = x3d.shape
    rows = PROJ_ROWS

    def tile(width):
        return pl.BlockSpec((1, rows, width), lambda b, j: (b, j, 0))

    def slabs(n):
        return pl.BlockSpec((1, n, rows, LANES), lambda b, j: (b, 0, j, 0))

    def out(width, dtype):
        return jax.ShapeDtypeStruct((batch, seq, width), dtype)

    def out_slabs(n):
        return jax.ShapeDtypeStruct((batch, n, seq, LANES), F32)

    return pl.pallas_call(
        _inproj_kernel,
        grid=(batch, seq // rows),
        in_specs=[tile(D_MODEL), pl.BlockSpec((1, 1, rows), lambda b, j: (b, 0, j)),
                  _resident((HEAD_DIM // 2, 1)), _resident((1, D_MODEL)),
                  _resident((D_MODEL, ATTN_WIDTH)), _resident((D_MODEL, KV_WIDTH)), _resident((D_MODEL, KV_WIDTH)),
                  _resident((D_MODEL, CONV_CHANNELS)), _resident((D_MODEL, SSM_WIDTH)),
                  _resident((D_MODEL, LANES)), _resident((CONV_WIDTH, CONV_CHANNELS)),
                  _resident((1, CONV_CHANNELS)), _resident((1, LANES))],
        out_specs=[slabs(Q_SLABS), slabs(KV_SLABS), slabs(KV_SLABS), tile(SSM_WIDTH), tile(BC_WIDTH),
                   tile(BC_WIDTH), tile(SSM_WIDTH), tile(LANES)],
        out_shape=[out_slabs(Q_SLABS), out_slabs(KV_SLABS), out_slabs(KV_SLABS), out(SSM_WIDTH, F32),
                   out(BC_WIDTH, BF16), out(BC_WIDTH, BF16), out(SSM_WIDTH, F32), out(LANES, F32)],
        scratch_shapes=[pltpu.VMEM((CONV_CHANNELS // LANES, rows + CARRY_ROWS, LANES), F32)],
        compiler_params=pltpu.CompilerParams(dimension_semantics=("parallel", "arbitrary"),
                                             vmem_limit_bytes=56 * MIB),
        name="in_proj",
    )(x3d, pos3d, invf, gain, wq, wk, wv, wx, wz, wdt, convw, convb, dtb)


STATE_ARRAYS = 3
MASKED = -2.0 ** 100
STATE_U, STATE_M, STATE_L = range(STATE_ARRAYS)


def _attn_blocks(blocks):
    blk = ATTN_BLOCK
    low = lax.broadcasted_iota(jnp.int32, (blk, LANES), 1) < HEAD_DIM
    zero = jnp.zeros((), BF16)
    one = jnp.ones((), BF16)

    scores = []
    for q_slabs, k_blk, _, (query_onehot, key_mask), _ in blocks:
        qb = [q.astype(BF16) for q in q_slabs]
        keys = jnp.concatenate([k_blk.astype(BF16), key_mask], axis=1)
        for masked in ([jnp.where(low, q, zero) for q in qb], [jnp.where(low, zero, q) for q in qb]):
            lhs = jnp.concatenate([jnp.concatenate(masked, axis=0), query_onehot], axis=1)
            scores.append(lax.dot_general(lhs, keys, (((1,), (1,)), ((), ())), preferred_element_type=F32))
    maxes = [jnp.max(s, axis=-1, keepdims=True) for s in scores]
    probs = [jnp.exp2(s - m).astype(BF16) for s, m in zip(scores, maxes)]
    results = []
    for i, (_, _, v_blk, _, _) in enumerate(blocks):
        vb = v_blk.astype(BF16)
        key_low = lax.broadcasted_iota(jnp.int32, vb.shape, 1) < HEAD_DIM
        results.append(jnp.dot(probs[2 * i], jnp.where(key_low, vb, one), preferred_element_type=F32))
        results.append(jnp.dot(probs[2 * i + 1], jnp.where(key_low, one, vb), preferred_element_type=F32))
    states = []
    for i, (q_slabs, _, _, _, load_old) in enumerate(blocks):
        r_low, r_high, m_a, m_b = results[2 * i], results[2 * i + 1], maxes[2 * i], maxes[2 * i + 1]
        old = None if load_old is None else load_old()
        state = []
        for g in range(len(q_slabs)):
            rows = slice(g * blk, (g + 1) * blk)
            u_new = jnp.where(low, r_low[rows], r_high[rows])
            l_new = pltpu.roll(jnp.where(low, r_high[rows], r_low[rows]), HEAD_DIM, 1)
            m_new = jnp.where(low, m_a[rows], m_b[rows])
            if old is not None:
                u_run, m_run, l_run = old[g]
                m_tot = jnp.maximum(m_run, m_new)
                a = jnp.exp2(m_run - m_tot)
                b = jnp.exp2(m_new - m_tot)
                u_new = u_run * a + u_new * b
                l_new = l_run * a + l_new * b
                m_new = m_tot
            state.append((u_new, m_new, l_new))
        states.append(state)
    return states


def _attn_kernel(q_ref, k_ref, v_ref, o_ref, st4_ref, stt_ref):
    blk = ATTN_BLOCK
    seq = q_ref.shape[2]
    n_slabs = q_ref.shape[1]
    sub4 = seq // 4

    def key_mask(n_keys, shift):
        si = lax.broadcasted_iota(jnp.int32, (n_keys, blk), 0)
        qi = lax.broadcasted_iota(jnp.int32, (n_keys, blk), 1) + shift
        return jnp.where((si <= qi) & (si >= qi - blk), 0.0, MASKED).astype(BF16)

    stacked = n_slabs * blk
    query_onehot = (lax.broadcasted_iota(jnp.int32, (stacked, blk), 0) % blk
                    == lax.broadcasted_iota(jnp.int32, (stacked, blk), 1)).astype(BF16)
    band_bias = (query_onehot, key_mask(2 * blk, blk))
    causal_bias = (query_onehot, key_mask(blk, 0))

    def load_q(rows):
        return [q_ref[0, g, rows, :] for g in range(n_slabs)]

    def load_state(ref, rows):
        return [tuple(ref[a * n_slabs + g, rows, :] for a in range(STATE_ARRAYS)) for g in range(n_slabs)]

    def store_state(ref, rows, state):
        for g in range(n_slabs):
            for a in range(STATE_ARRAYS):
                ref[a * n_slabs + g, rows, :] = state[g][a]

    def store_out(rows, state):
        for g in range(n_slabs):
            o_ref[0, g, rows, :] = state[g][STATE_U] / state[g][STATE_L]

    def run(plans):
        states = _attn_blocks([(load_q(rows), k_ref[0, 0, keys, :], v_ref[0, 0, keys, :], bias, load_old)
                               for rows, keys, bias, load_old, _ in plans])
        for (rows, _, _, _, store), state in zip(plans, states):
            store(rows, state)

    def paired(n_blocks, plan):
        def trip(i, carry):
            run([plan(2 * i), plan(2 * i + 1)])
            return carry
        lax.fori_loop(0, n_blocks // 2, trip, 0)

    def d16_plan(r16):
        rows = pl.ds(r16, blk, stride=16)
        dest = pl.ds((r16 % 4) * sub4 + r16 // 4, blk, stride=4)
        return rows, rows, causal_bias, None, lambda _, state: store_state(st4_ref, dest, state)

    paired(16, d16_plan)

    def to_token_order(rows, state):
        store_state(stt_ref, rows, state)

    def d4_first_plan(r4):
        rows = pl.ds(r4, blk, stride=4)
        src = pl.ds(pl.multiple_of(r4 * sub4, blk), blk)
        return rows, rows, causal_bias, lambda: load_state(st4_ref, src), to_token_order

    paired(4, d4_first_plan)
    later_blocks = sub4 // blk - 1

    def d4_plan(idx):
        r4 = idx // later_blocks
        n = 1 + idx % later_blocks
        rows = pl.ds(r4 + 4 * blk * n, blk, stride=4)
        keys = pl.ds(r4 + 4 * blk * (n - 1), 2 * blk, stride=4)
        src = pl.ds(pl.multiple_of(r4 * sub4 + n * blk, blk), blk)
        return rows, keys, band_bias, lambda: load_state(st4_ref, src), to_token_order

    paired(4 * later_blocks, d4_plan)

    def d1_plan(n):
        start = n * blk if isinstance(n, int) else pl.multiple_of(n * blk, blk)
        rows = pl.ds(start, blk)
        if isinstance(n, int) and n == 0:
            return rows, rows, causal_bias, lambda: load_state(stt_ref, rows), store_out
        return rows, pl.ds(start - blk, 2 * blk), band_bias, lambda: load_state(stt_ref, rows), store_out

    last = seq // blk - 1
    run([d1_plan(0), d1_plan(last)])
    paired(last - 1, lambda i: d1_plan(i + 1))


def _attention(q, k, v):
    batch, _, seq, _ = q.shape
    slabs_per_pair = Q_SLABS // KV_PAIRS
    q_spec = pl.BlockSpec((1, slabs_per_pair, seq, LANES), lambda b, p: (b, p, 0, 0))
    kv_spec = pl.BlockSpec((1, 1, seq, LANES), lambda b, p: (b, p, 0, 0))
    state_shape = (STATE_ARRAYS * slabs_per_pair, seq, LANES)
    return pl.pallas_call(
        _attn_kernel,
        grid=(batch, KV_PAIRS),
        in_specs=[q_spec, kv_spec, kv_spec],
        out_specs=q_spec,
        out_shape=jax.ShapeDtypeStruct(q.shape, F32),
        scratch_shapes=[pltpu.VMEM(state_shape, F32), pltpu.VMEM(state_shape, F32)],
        compiler_params=pltpu.CompilerParams(dimension_semantics=("parallel", "parallel"),
                                             vmem_limit_bytes=56 * MIB),
        name="attention",
    )(q, k, v)


def _split_bf16(x, pieces):
    parts = []
    for _ in range(pieces):
        part = x.astype(BF16)
        parts.append(part)
        x = x - part.astype(F32)
    return parts


def _ssd_kernel(xs_ref, b_ref, c_ref, dt_ref, z_ref, arow_ref, dskip_ref, norm_ref, tri_ref, expand_ref,
                y_ref, state_ref):
    chunk = SSD_CHUNK

    @pl.when(pl.program_id(1) == 0)
    def _():
        state_ref[...] = jnp.zeros(state_ref.shape, F32)

    xs = xs_ref[0]
    dt = dt_ref[0]
    a3 = jnp.dot(tri_ref[...], jnp.concatenate(_split_bf16(dt * arow_ref[...], 3), axis=1),
                 preferred_element_type=F32)
    a_cs = a3[:, :LANES] + a3[:, LANES:2 * LANES] + a3[:, 2 * LANES:]
    a_last = a_cs[chunk - 1:chunk, :]
    decay_to_end = jnp.exp(a_last - a_cs)
    decay_from_start = jnp.exp(a_cs)

    def expand(stat):
        return jnp.dot(jnp.concatenate(_split_bf16(stat, 2), axis=1), expand_ref[...], preferred_element_type=F32)

    w_state = expand(decay_to_end * dt)
    scale_off = expand(decay_from_start)
    chunk_decay = scale_off[chunk - 1:chunk, :]

    a_cs_t = a_cs.T
    dt_t = dt.T
    row = lax.broadcasted_iota(jnp.int32, (chunk, chunk), 0)
    col = lax.broadcasted_iota(jnp.int32, (chunk, chunk), 1)
    causal = row >= col
    low = lax.broadcasted_iota(jnp.int32, (chunk, LANES), 1) < SSM_HEAD_DIM
    zero = jnp.zeros((), BF16)

    xs_bf = xs.astype(BF16)
    xw_bf = (xs * w_state).astype(BF16)
    heads_per_group = SSM_HEADS // SSM_GROUPS
    y_parts = []
    for g in range(SSM_GROUPS):
        gcols = slice(g * SSM_STATE, (g + 1) * SSM_STATE)
        b_g = b_ref[0, :, gcols]
        c_g = c_ref[0, :, gcols]
        cb = lax.dot_general(c_g, b_g, (((1,), (1,)), ((), ())), preferred_element_type=F32)
        wide = slice(g * GROUP_WIDTH, (g + 1) * GROUP_WIDTH)
        state_g = state_ref[:, wide]
        y_off = jnp.dot(c_g, state_g.astype(BF16), preferred_element_type=F32) * scale_off[:, wide]
        diag_parts = []
        for pair in range(heads_per_group // 2):
            mats = []
            for e in range(2):
                h = g * heads_per_group + 2 * pair + e
                seg = a_cs[:, h:h + 1] - a_cs_t[h:h + 1, :]
                decay = jnp.exp(jnp.where(causal, seg, -jnp.inf))
                mats.append((cb * decay * dt_t[h:h + 1, :]).astype(BF16))
            pcols = slice((g * heads_per_group + 2 * pair) * SSM_HEAD_DIM,
                          (g * heads_per_group + 2 * pair + 2) * SSM_HEAD_DIM)
            x_pair = xs_bf[:, pcols]
            rhs = jnp.concatenate([jnp.where(low, x_pair, zero), jnp.where(low, zero, x_pair)], axis=0)
            diag_parts.append(jnp.dot(jnp.concatenate(mats, axis=1), rhs, preferred_element_type=F32))
        y_parts.append(jnp.concatenate(diag_parts, axis=1) + y_off)
        b_t = b_g.astype(F32).T.astype(BF16)
        new_states = jnp.dot(b_t, xw_bf[:, wide], preferred_element_type=F32)
        state_ref[:, wide] = state_g * chunk_decay[:, wide] + new_states

    y = jnp.concatenate(y_parts, axis=1) + dskip_ref[...] * xs
    y = y * jax.nn.silu(z_ref[0])
    normed = []
    for g in range(SSM_GROUPS):
        y_g = y[:, g * GROUP_WIDTH:(g + 1) * GROUP_WIDTH]
        normed.append(y_g * lax.rsqrt(jnp.mean(y_g * y_g, axis=-1, keepdims=True) + NORM_EPS))
    y_ref[0] = jnp.concatenate(normed, axis=1) * norm_ref[...]


def _ssd(xs, bm, cm, dt, z, arow, dskip, norm, tri, expand):
    batch, seq, _ = xs.shape

    def tile(width):
        return pl.BlockSpec((1, SSD_CHUNK, width), lambda b, c: (b, c, 0))

    return pl.pallas_call(
        _ssd_kernel,
        grid=(batch, seq // SSD_CHUNK),
        in_specs=[tile(SSM_WIDTH), tile(BC_WIDTH), tile(BC_WIDTH), tile(LANES), tile(SSM_WIDTH),
                  _resident((1, LANES)), _resident((1, SSM_WIDTH)), _resident((1, SSM_WIDTH)),
                  _resident((SSD_CHUNK, SSD_CHUNK)), _resident((2 * LANES, SSM_WIDTH))],
        out_specs=tile(SSM_WIDTH),
        out_shape=jax.ShapeDtypeStruct((batch, seq, SSM_WIDTH), F32),
        scratch_shapes=[pltpu.VMEM((SSM_STATE, SSM_WIDTH), F32)],
        compiler_params=pltpu.CompilerParams(dimension_semantics=("parallel", "arbitrary"),
                                             vmem_limit_bytes=32 * MIB),
        name="ssd",
    )(xs, bm, cm, dt, z, arow, dskip, norm, tri, expand)


def _out_ffn_kernel(x_ref, attn_ref, y_ref, wo_ref, mixpost_ref, pre_ref, wg_ref, wu_ref, wd_ref, post_ref, o_ref):
    attn = jnp.concatenate([attn_ref[0, j] for j in range(Q_SLABS)], axis=1).astype(BF16)
    mixed = (jnp.dot(attn, wo_ref[:ATTN_WIDTH, :], preferred_element_type=F32)
             + jnp.dot(y_ref[0].astype(BF16), wo_ref[ATTN_WIDTH:, :], preferred_element_type=F32))
    x = x_ref[0] + _rmsnorm(mixed, mixpost_ref[...])
    o_ref[0] = _swiglu_half_step(x, pre_ref[...], wg_ref, wu_ref, wd_ref, post_ref[...])


def _out_ffn(x3d, attn, y3d, wo, mixpost, pre, wg, wu, wd, post):
    batch, seq, _ = x3d.shape
    rows = FFN_ROWS
    row_spec = pl.BlockSpec((1, rows, D_MODEL), lambda b, j: (b, j, 0))
    slab_spec = pl.BlockSpec((1, Q_SLABS, rows, LANES), lambda b, j: (b, 0, j, 0))
    return pl.pallas_call(
        _out_ffn_kernel,
        grid=(batch, seq // rows),
        in_specs=[row_spec, slab_spec, row_spec, _resident((ATTN_WIDTH + SSM_WIDTH, D_MODEL)),
                  _resident((1, D_MODEL)), _resident((1, D_MODEL)), _resident((D_MODEL, D_FF)),
                  _resident((D_MODEL, D_FF)), _resident((D_FF, D_MODEL)), _resident((1, D_MODEL))],
        out_specs=row_spec,
        out_shape=jax.ShapeDtypeStruct((batch, seq, D_MODEL), F32),
        compiler_params=pltpu.CompilerParams(dimension_semantics=("parallel", "parallel"),
                                             vmem_limit_bytes=56 * MIB),
        name="out_ffn",
    )(x3d, attn, y3d, wo, mixpost, pre, wg, wu, wd, post)


def _slab_head_order():
    order = []
    for pair in range(KV_PAIRS):
        for g in range(Q_PER_KV):
            order += [(2 * pair) * Q_PER_KV + g, (2 * pair + 1) * Q_PER_KV + g]
    return order


def _head_columns(order):
    return jnp.concatenate([jnp.arange(h * HEAD_DIM, (h + 1) * HEAD_DIM) for h in order])


def _pad_lanes(v):
    return jnp.pad(v, [(0, 0)] * (v.ndim - 1) + [(0, LANES - v.shape[-1])])


def _layer(x, pos3d, invf, tri, expand, p):
    batch, seq, _ = x.shape
    tokens = batch * seq
    row = lambda v: v.reshape(1, -1)
    bf = lambda w: w.astype(BF16)

    x1 = _ffn(x.reshape(tokens, D_MODEL), row(p["ffn1_pre_norm"]), bf(p["ffn1_w_gate"]), bf(p["ffn1_w_up"]),
              bf(p["ffn1_w_down"]), row(p["ffn1_post_norm"])).reshape(batch, seq, D_MODEL)

    w_in = p["w_in"]
    o_k = ATTN_WIDTH
    o_v = o_k + KV_WIDTH
    o_x = o_v + KV_WIDTH
    o_z = o_x + CONV_CHANNELS
    o_dt = o_z + SSM_WIDTH
    q_cols = _head_columns(_slab_head_order())
    q, k, v, xs, bm, cm, z, dt = _in_proj(
        x1, pos3d, invf, row(p["mix_pre_norm"]),
        bf(w_in[:, :o_k][:, q_cols]), bf(w_in[:, o_k:o_v]), bf(w_in[:, o_v:o_x]),
        bf(w_in[:, o_x:o_z]), bf(w_in[:, o_z:o_dt]), bf(_pad_lanes(w_in[:, o_dt:])),
        p["conv_w"], row(p["conv_b"]), _pad_lanes(row(p["dt_bias"])))

    attn = _attention(q, k, v)

    arow = _pad_lanes(row(-jnp.exp(p["a_log"])))
    dskip = row(jnp.repeat(p["d_skip"], SSM_HEAD_DIM))
    y = _ssd(xs, bm, cm, dt, z, arow, dskip, row(p["ssm_norm"]), tri, expand)

    w_out = p["w_out"]
    w_out = jnp.concatenate([w_out[:ATTN_WIDTH][q_cols], w_out[ATTN_WIDTH:]], axis=0)
    return _out_ffn(x1, attn, y, bf(w_out), row(p["mix_post_norm"]), row(p["ffn2_pre_norm"]),
                    bf(p["ffn2_w_gate"]), bf(p["ffn2_w_up"]), bf(p["ffn2_w_down"]), row(p["ffn2_post_norm"]))


def kernel(x, positions, ffn1_pre_norm, ffn1_w_gate, ffn1_w_up, ffn1_w_down, ffn1_post_norm, mix_pre_norm, w_in, conv_w, conv_b, dt_bias, a_log, d_skip, ssm_norm, w_out, mix_post_norm, ffn2_pre_norm, ffn2_w_gate, ffn2_w_up, ffn2_w_down, ffn2_post_norm):
    params = dict(ffn1_pre_norm=ffn1_pre_norm, ffn1_w_gate=ffn1_w_gate, ffn1_w_up=ffn1_w_up,
                  ffn1_w_down=ffn1_w_down, ffn1_post_norm=ffn1_post_norm, mix_pre_norm=mix_pre_norm, w_in=w_in,
                  conv_w=conv_w, conv_b=conv_b, dt_bias=dt_bias, a_log=a_log, d_skip=d_skip, ssm_norm=ssm_norm,
                  w_out=w_out, mix_post_norm=mix_post_norm, ffn2_pre_norm=ffn2_pre_norm, ffn2_w_gate=ffn2_w_gate,
                  ffn2_w_up=ffn2_w_up, ffn2_w_down=ffn2_w_down, ffn2_post_norm=ffn2_post_norm)
    depth = w_in.shape[0]
    batch, seq, _ = x.shape
    inv_freq = ROPE_THETA ** (-jnp.arange(0, HEAD_DIM, 2, dtype=F32) / HEAD_DIM)
    invf = inv_freq.reshape(HEAD_DIM // 2, 1)
    pos3d = positions.reshape(batch, 1, seq)
    idx = jnp.arange(SSD_CHUNK)
    tri = (idx[:, None] >= idx[None, :]).astype(BF16)
    expand = (jnp.arange(2 * LANES)[:, None] % LANES == jnp.arange(SSM_WIDTH)[None, :] // SSM_HEAD_DIM).astype(BF16)
    for i in range(depth):
        x = _layer(x, pos3d, invf, tri, expand, {name: w[i] for name, w in params.items()})
    return x
```

```python
import functools
import math

import jax
import jax.numpy as jnp
from jax import lax
from jax.experimental import pallas as pl
from jax.experimental.pallas import tpu as pltpu

F32 = jnp.float32
BF16 = jnp.bfloat16

D_MODEL = 1024
D_FF = 2816
HEAD_DIM = 64
N_Q_HEADS = 16
N_KV_HEADS = 4
Q_PER_KV = N_Q_HEADS // N_KV_HEADS
ATTN_WIDTH = N_Q_HEADS * HEAD_DIM
KV_WIDTH = N_KV_HEADS * HEAD_DIM
ATTN_BLOCK = 128
ROPE_THETA = 10000.0
SSM_HEADS = 16
SSM_HEAD_DIM = 64
SSM_WIDTH = SSM_HEADS * SSM_HEAD_DIM
SSM_STATE = 128
SSM_GROUPS = 2
GROUP_WIDTH = SSM_WIDTH // SSM_GROUPS
BC_WIDTH = SSM_GROUPS * SSM_STATE
CONV_WIDTH = 4
CONV_CHANNELS = SSM_WIDTH + 2 * BC_WIDTH
SSD_CHUNK = 128
MACARON_WEIGHT = 0.5
NORM_EPS = 1e-6
LOG2_E = math.log2(math.e)

LANES = 128
CARRY_ROWS = 8
MIB = 1024 * 1024

FFN_ROWS = 512
FF_CHUNK = 256
PROJ_ROWS = 512

Q_SLABS = ATTN_WIDTH // LANES
KV_SLABS = KV_WIDTH // LANES
KV_PAIRS = N_KV_HEADS // 2


def _rmsnorm(x, gain):
    return x * lax.rsqrt(jnp.mean(x * x, axis=-1, keepdims=True) + NORM_EPS) * gain


def _resident(shape):
    return pl.BlockSpec(shape, lambda *_: (0,) * len(shape), pipeline_mode=pl.Buffered(1))


def _swiglu_half_step(x, pre, wg_ref, wu_ref, wd_ref, post):
    xn = _rmsnorm(x, pre).astype(BF16)
    acc = jnp.zeros(x.shape, F32)
    for c in range(D_FF // FF_CHUNK):
        cols = slice(c * FF_CHUNK, (c + 1) * FF_CHUNK)
        gate = jnp.dot(xn, wg_ref[:, cols], preferred_element_type=F32)
        up = jnp.dot(xn, wu_ref[:, cols], preferred_element_type=F32)
        hidden = (jax.nn.silu(gate) * up).astype(BF16)
        acc = acc + jnp.dot(hidden, wd_ref[cols, :], preferred_element_type=F32)
    return x + MACARON_WEIGHT * _rmsnorm(acc, post)


def _ffn_kernel(x_ref, pre_ref, wg_ref, wu_ref, wd_ref, post_ref, o_ref):
    o_ref[...] = _swiglu_half_step(x_ref[...], pre_ref[...], wg_ref, wu_ref, wd_ref, post_ref[...])


def _ffn(x2d, pre, wg, wu, wd, post):
    tokens = x2d.shape[0]
    row_spec = pl.BlockSpec((FFN_ROWS, D_MODEL), lambda i: (i, 0))
    return pl.pallas_call(
        _ffn_kernel,
        grid=(tokens // FFN_ROWS,),
        in_specs=[row_spec, _resident((1, D_MODEL)), _resident((D_MODEL, D_FF)), _resident((D_MODEL, D_FF)),
                  _resident((D_FF, D_MODEL)), _resident((1, D_MODEL))],
        out_specs=row_spec,
        out_shape=jax.ShapeDtypeStruct((tokens, D_MODEL), F32),
        compiler_params=pltpu.CompilerParams(dimension_semantics=("parallel",), vmem_limit_bytes=48 * MIB),
        name="ffn",
    )(x2d, pre, wg, wu, wd, post)


def _softplus(x):
    return jnp.maximum(x, 0.0) + jnp.log1p(jnp.exp(-jnp.abs(x)))


def _inproj_kernel(x_ref, pos_ref, invf_ref, gain_ref, wq_ref, wk_ref, wv_ref, wx_ref, wz_ref, wdt_ref,
                   convw_ref, convb_ref, dtb_ref,
                   q_ref, k_ref, v_ref, xs_ref, b_ref, c_ref, z_ref, dt_ref, ubuf_ref):
    rows = x_ref.shape[1]
    xn = _rmsnorm(x_ref[0], gain_ref[...]).astype(BF16)

    ang_t = invf_ref[...] * pos_ref[0].astype(F32)
    cos_t = jnp.cos(ang_t)
    sin_t = jnp.sin(ang_t)
    copies = LANES // HEAD_DIM
    cos = jnp.concatenate([cos_t, cos_t] * copies, axis=0).T
    sin = jnp.concatenate([-sin_t, sin_t] * copies, axis=0).T
    lane = lax.broadcasted_iota(jnp.int32, (rows, LANES), 1)
    first_half = (lane % HEAD_DIM) < (HEAD_DIM // 2)

    def rope(t):
        back = pltpu.roll(t, HEAD_DIM // 2, 1)
        fwd = pltpu.roll(t, LANES - HEAD_DIM // 2, 1)
        return t * cos + jnp.where(first_half, fwd, back) * sin

    q = jnp.dot(xn, wq_ref[...], preferred_element_type=F32)
    for j in range(Q_SLABS):
        q_ref[0, j] = rope(q[:, j * LANES:(j + 1) * LANES]) * (HEAD_DIM ** -0.5 * LOG2_E)
    k = jnp.dot(xn, wk_ref[...], preferred_element_type=F32)
    v = jnp.dot(xn, wv_ref[...], preferred_element_type=F32)
    for j in range(KV_SLABS):
        k_ref[0, j] = rope(k[:, j * LANES:(j + 1) * LANES])
        v_ref[0, j] = v[:, j * LANES:(j + 1) * LANES]
    z_ref[0] = jnp.dot(xn, wz_ref[...], preferred_element_type=F32)
    dt_ref[0] = _softplus(jnp.dot(xn, wdt_ref[...], preferred_element_type=F32) + dtb_ref[...])

    n_slabs = CONV_CHANNELS // LANES

    @pl.when(pl.program_id(1) == 0)
    def _():
        ubuf_ref[:, 0:CARRY_ROWS, :] = jnp.zeros((n_slabs, CARRY_ROWS, LANES), F32)

    u = jnp.dot(xn, wx_ref[...], preferred_element_type=F32)
    for j in range(n_slabs):
        cols = slice(j * LANES, (j + 1) * LANES)
        ubuf_ref[j, CARRY_ROWS:CARRY_ROWS + rows, :] = u[:, cols]
        conv = convb_ref[:, cols] + ubuf_ref[j, CARRY_ROWS:CARRY_ROWS + rows, :] * convw_ref[CONV_WIDTH - 1:CONV_WIDTH, cols]
        for tap in range(CONV_WIDTH - 1):
            start = CARRY_ROWS - (CONV_WIDTH - 1) + tap
            conv = conv + ubuf_ref[j, start:start + rows, :] * convw_ref[tap:tap + 1, cols]
        act = jax.nn.silu(conv)
        if j < SSM_WIDTH // LANES:
            xs_ref[0, :, cols] = act
        elif j < (SSM_WIDTH + BC_WIDTH) // LANES:
            b_ref[0, :, j * LANES - SSM_WIDTH:(j + 1) * LANES - SSM_WIDTH] = act.astype(BF16)
        else:
            first = SSM_WIDTH + BC_WIDTH
            c_ref[0, :, j * LANES - first:(j + 1) * LANES - first] = act.astype(BF16)
        ubuf_ref[j, 0:CARRY_ROWS, :] = ubuf_ref[j, rows:rows + CARRY_ROWS, :]


def _in_proj(x3d, pos3d, invf, gain, wq, wk, wv, wx, wz, wdt, convw, convb, dtb):
    batch, seq, _ = x3d.shape
    rows = PROJ_ROWS

    def tile(width):
        return pl.BlockSpec((1, rows, width), lambda b, j: (b, j, 0))

    def slabs(n):
        return pl.BlockSpec((1, n, rows, LANES), lambda b, j: (b, 0, j, 0))

    def out(width, dtype):
        return jax.ShapeDtypeStruct((batch, seq, width), dtype)

    def out_slabs(n):
        return jax.ShapeDtypeStruct((batch, n, seq, LANES), F32)

    return pl.pallas_call(
        _inproj_kernel,
        grid=(batch, seq // rows),
        in_specs=[tile(D_MODEL), pl.BlockSpec((1, 1, rows), lambda b, j: (b, 0, j)),
                  _resident((HEAD_DIM // 2, 1)), _resident((1, D_MODEL)),
                  _resident((D_MODEL, ATTN_WIDTH)), _resident((D_MODEL, KV_WIDTH)), _resident((D_MODEL, KV_WIDTH)),
                  _resident((D_MODEL, CONV_CHANNELS)), _resident((D_MODEL, SSM_WIDTH)),
                  _resident((D_MODEL, LANES)), _resident((CONV_WIDTH, CONV_CHANNELS)),
                  _resident((1, CONV_CHANNELS)), _resident((1, LANES))],
        out_specs=[slabs(Q_SLABS), slabs(KV_SLABS), slabs(KV_SLABS), tile(SSM_WIDTH), tile(BC_WIDTH),
                   tile(BC_WIDTH), tile(SSM_WIDTH), tile(LANES)],
        out_shape=[out_slabs(Q_SLABS), out_slabs(KV_SLABS), out_slabs(KV_SLABS), out(SSM_WIDTH, F32),
                   out(BC_WIDTH, BF16), out(BC_WIDTH, BF16), out(SSM_WIDTH, F32), out(LANES, F32)],
        scratch_shapes=[pltpu.VMEM((CONV_CHANNELS // LANES, rows + CARRY_ROWS, LANES), F32)],
        compiler_params=pltpu.CompilerParams(dimension_semantics=("parallel", "arbitrary"),
                                             vmem_limit_bytes=56 * MIB),
        name="in_proj",
    )(x3d, pos3d, invf, gain, wq, wk, wv, wx, wz, wdt, convw, convb, dtb)


STATE_ARRAYS = 3
MASKED = -2.0 ** 100
STATE_U, STATE_M, STATE_L = range(STATE_ARRAYS)


def _attn_blocks(blocks):
    blk = ATTN_BLOCK
    low = lax.broadcasted_iota(jnp.int32, (blk, LANES), 1) < HEAD_DIM
    zero = jnp.zeros((), BF16)
    one = jnp.ones((), BF16)

    scores = []
    for q_slabs, k_blk, _, (query_onehot, key_mask), _ in blocks:
        qb = [q.astype(BF16) for q in q_slabs]
        keys = jnp.concatenate([k_blk.astype(BF16), key_mask], axis=1)
        for masked in ([jnp.where(low, q, zero) for q in qb], [jnp.where(low, zero, q) for q in qb]):
            lhs = jnp.concatenate([jnp.concatenate(masked, axis=0), query_onehot], axis=1)
            scores.append(lax.dot_general(lhs, keys, (((1,), (1,)), ((), ())), preferred_element_type=F32))
    maxes = [jnp.max(s, axis=-1, keepdims=True) for s in scores]
    probs = [jnp.exp2((s - m).astype(BF16)) for s, m in zip(scores, maxes)]
    results = []
    for i, (_, _, v_blk, _, _) in enumerate(blocks):
        vb = v_blk.astype(BF16)
        key_low = lax.broadcasted_iota(jnp.int32, vb.shape, 1) < HEAD_DIM
        results.append(jnp.dot(probs[2 * i], jnp.where(key_low, vb, one), preferred_element_type=F32))
        results.append(jnp.dot(probs[2 * i + 1], jnp.where(key_low, one, vb), preferred_element_type=F32))
    states = []
    for i, (q_slabs, _, _, _, load_old) in enumerate(blocks):
        r_low, r_high, m_a, m_b = results[2 * i], results[2 * i + 1], maxes[2 * i], maxes[2 * i + 1]
        old = None if load_old is None else load_old()
        state = []
        for g in range(len(q_slabs)):
            rows = slice(g * blk, (g + 1) * blk)
            u_new = jnp.where(low, r_low[rows], r_high[rows])
            l_new = pltpu.roll(jnp.where(low, r_high[rows], r_low[rows]), HEAD_DIM, 1)
            m_new = jnp.where(low, m_a[rows], m_b[rows])
            if old is not None:
                u_run, m_run, l_run = old[g]
                m_tot = jnp.maximum(m_run, m_new)
                a = jnp.exp2(m_run - m_tot)
                b = jnp.exp2(m_new - m_tot)
                u_new = u_run * a + u_new * b
                l_new = l_run * a + l_new * b
                m_new = m_tot
            state.append((u_new, m_new, l_new))
        states.append(state)
    return states


def _attn_kernel(q_ref, k_ref, v_ref, o_ref, st4_ref, stt_ref):
    blk = ATTN_BLOCK
    seq = q_ref.shape[2]
    n_slabs = q_ref.shape[1]
    sub4 = seq // 4

    def key_mask(n_keys, shift):
        si = lax.broadcasted_iota(jnp.int32, (n_keys, blk), 0)
        qi = lax.broadcasted_iota(jnp.int32, (n_keys, blk), 1) + shift
        return jnp.where((si <= qi) & (si >= qi - blk), 0.0, MASKED).astype(BF16)

    stacked = n_slabs * blk
    query_onehot = (lax.broadcasted_iota(jnp.int32, (stacked, blk), 0) % blk
                    == lax.broadcasted_iota(jnp.int32, (stacked, blk), 1)).astype(BF16)
    band_bias = (query_onehot, key_mask(2 * blk, blk))
    causal_bias = (query_onehot, key_mask(blk, 0))

    def load_q(rows):
        return [q_ref[0, g, rows, :] for g in range(n_slabs)]

    def load_state(ref, rows):
        return [tuple(ref[a * n_slabs + g, rows, :] for a in range(STATE_ARRAYS)) for g in range(n_slabs)]

    def store_state(ref, rows, state):
        for g in range(n_slabs):
            for a in range(STATE_ARRAYS):
                ref[a * n_slabs + g, rows, :] = state[g][a]

    def store_out(rows, state):
        for g in range(n_slabs):
            o_ref[0, g, rows, :] = state[g][STATE_U] / state[g][STATE_L]

    def run(plans):
        states = _attn_blocks([(load_q(rows), k_ref[0, 0, keys, :], v_ref[0, 0, keys, :], bias, load_old)
                               for rows, keys, bias, load_old, _ in plans])
        for (rows, _, _, _, store), state in zip(plans, states):
            store(rows, state)

    def paired(n_blocks, plan, per_trip=2):
        def trip(i, carry):
            run([plan(per_trip * i + j) for j in range(per_trip)])
            return carry
        lax.fori_loop(0, n_blocks // per_trip, trip, 0)

    def d16_plan(r16):
        rows = pl.ds(r16, blk, stride=16)
        dest = pl.ds((r16 % 4) * sub4 + r16 // 4, blk, stride=4)
        return rows, rows, causal_bias, None, lambda _, state: store_state(st4_ref, dest, state)

    paired(16, d16_plan, per_trip=4)

    def to_token_order(rows, state):
        store_state(stt_ref, rows, state)

    def d4_first_plan(r4):
        rows = pl.ds(r4, blk, stride=4)
        src = pl.ds(pl.multiple_of(r4 * sub4, blk), blk)
        return rows, rows, causal_bias, lambda: load_state(st4_ref, src), to_token_order

    paired(4, d4_first_plan, per_trip=4)
    later_blocks = sub4 // blk - 1

    def d4_plan(idx):
        r4 = idx // later_blocks
        n = 1 + idx % later_blocks
        rows = pl.ds(r4 + 4 * blk * n, blk, stride=4)
        keys = pl.ds(r4 + 4 * blk * (n - 1), 2 * blk, stride=4)
        src = pl.ds(pl.multiple_of(r4 * sub4 + n * blk, blk), blk)
        return rows, keys, band_bias, lambda: load_state(st4_ref, src), to_token_order

    paired(4 * later_blocks, d4_plan, per_trip=3)

    def d1_plan(n):
        start = n * blk if isinstance(n, int) else pl.multiple_of(n * blk, blk)
        rows = pl.ds(start, blk)
        if isinstance(n, int) and n == 0:
            return rows, rows, causal_bias, lambda: load_state(stt_ref, rows), store_out
        return rows, pl.ds(start - blk, 2 * blk), band_bias, lambda: load_state(stt_ref, rows), store_out

    last = seq // blk - 1
    run([d1_plan(0)])
    paired(last, lambda i: d1_plan(i + 1), per_trip=3)


def _attention(q, k, v):
    batch, _, seq, _ = q.shape
    slabs_per_pair = Q_SLABS // KV_PAIRS
    q_spec = pl.BlockSpec((1, slabs_per_pair, seq, LANES), lambda b, p: (b, p, 0, 0))
    kv_spec = pl.BlockSpec((1, 1, seq, LANES), lambda b, p: (b, p, 0, 0))
    state_shape = (STATE_ARRAYS * slabs_per_pair, seq, LANES)
    return pl.pallas_call(
        _attn_kernel,
        grid=(batch, KV_PAIRS),
        in_specs=[q_spec, kv_spec, kv_spec],
        out_specs=q_spec,
        out_shape=jax.ShapeDtypeStruct(q.shape, F32),
        scratch_shapes=[pltpu.VMEM(state_shape, F32), pltpu.VMEM(state_shape, F32)],
        compiler_params=pltpu.CompilerParams(dimension_semantics=("parallel", "parallel"),
                                             vmem_limit_bytes=56 * MIB),
        name="attention",
    )(q, k, v)


def _split_bf16(x, pieces):
    parts = []
    for _ in range(pieces):
        part = x.astype(BF16)
        parts.append(part)
        x = x - part.astype(F32)
    return parts


def _ssd_kernel(xs_ref, b_ref, c_ref, dt_ref, z_ref, arow_ref, dskip_ref, norm_ref, tri_ref, expand_ref,
                y_ref, state_ref):
    chunk = SSD_CHUNK

    @pl.when(pl.program_id(1) == 0)
    def _():
        state_ref[...] = jnp.zeros(state_ref.shape, F32)

    xs = xs_ref[0]
    dt = dt_ref[0]
    a3 = jnp.dot(tri_ref[...], jnp.concatenate(_split_bf16(dt * arow_ref[...], 3), axis=1),
                 preferred_element_type=F32)
    a_cs = a3[:, :LANES] + a3[:, LANES:2 * LANES] + a3[:, 2 * LANES:]
    a_last = a_cs[chunk - 1:chunk, :]
    decay_to_end = jnp.exp(a_last - a_cs)
    decay_from_start = jnp.exp(a_cs)

    def expand(stat):
        return jnp.dot(jnp.concatenate(_split_bf16(stat, 2), axis=1), expand_ref[...], preferred_element_type=F32)

    w_state = expand(decay_to_end * dt)
    scale_off = expand(decay_from_start)
    chunk_decay = scale_off[chunk - 1:chunk, :]

    a_cs_t = a_cs.T
    dt_t = dt.T
    row = lax.broadcasted_iota(jnp.int32, (chunk, chunk), 0)
    col = lax.broadcasted_iota(jnp.int32, (chunk, chunk), 1)
    causal = row >= col
    low = lax.broadcasted_iota(jnp.int32, (chunk, LANES), 1) < SSM_HEAD_DIM
    zero = jnp.zeros((), BF16)

    xs_bf = xs.astype(BF16)
    xw_bf = (xs * w_state).astype(BF16)
    heads_per_group = SSM_HEADS // SSM_GROUPS
    y_parts = []
    for g in range(SSM_GROUPS):
        gcols = slice(g * SSM_STATE, (g + 1) * SSM_STATE)
        b_g = b_ref[0, :, gcols]
        c_g = c_ref[0, :, gcols]
        cb = lax.dot_general(c_g, b_g, (((1,), (1,)), ((), ())), preferred_element_type=F32)
        wide = slice(g * GROUP_WIDTH, (g + 1) * GROUP_WIDTH)
        state_g = state_ref[:, wide]
        y_off = jnp.dot(c_g, state_g.astype(BF16), preferred_element_type=F32) * scale_off[:, wide]
        diag_parts = []
        for pair in range(heads_per_group // 2):
            mats = []
            for e in range(2):
                h = g * heads_per_group + 2 * pair + e
                seg = a_cs[:, h:h + 1] - a_cs_t[h:h + 1, :]
                decay = jnp.exp(jnp.where(causal, seg, -jnp.inf))
                mats.append((cb * decay * dt_t[h:h + 1, :]).astype(BF16))
            pcols = slice((g * heads_per_group + 2 * pair) * SSM_HEAD_DIM,
                          (g * heads_per_group + 2 * pair + 2) * SSM_HEAD_DIM)
            x_pair = xs_bf[:, pcols]
            rhs = jnp.concatenate([jnp.where(low, x_pair, zero), jnp.where(low, zero, x_pair)], axis=0)
            diag_parts.append(jnp.dot(jnp.concatenate(mats, axis=1), rhs, preferred_element_type=F32))
        y_parts.append(jnp.concatenate(diag_parts, axis=1) + y_off)
        b_t = b_g.astype(F32).T.astype(BF16)
        new_states = jnp.dot(b_t, xw_bf[:, wide], preferred_element_type=F32)
        state_ref[:, wide] = state_g * chunk_decay[:, wide] + new_states

    y = jnp.concatenate(y_parts, axis=1) + dskip_ref[...] * xs
    y = y * jax.nn.silu(z_ref[0])
    normed = []
    for g in range(SSM_GROUPS):
        y_g = y[:, g * GROUP_WIDTH:(g + 1) * GROUP_WIDTH]
        normed.append(y_g * lax.rsqrt(jnp.mean(y_g * y_g, axis=-1, keepdims=True) + NORM_EPS))
    y_ref[0] = jnp.concatenate(normed, axis=1) * norm_ref[...]


def _ssd(xs, bm, cm, dt, z, arow, dskip, norm, tri, expand):
    batch, seq, _ = xs.shape

    def tile(width):
        return pl.BlockSpec((1, SSD_CHUNK, width), lambda b, c: (b, c, 0))

    return pl.pallas_call(
        _ssd_kernel,
        grid=(batch, seq // SSD_CHUNK),
        in_specs=[tile(SSM_WIDTH), tile(BC_WIDTH), tile(BC_WIDTH), tile(LANES), tile(SSM_WIDTH),
                  _resident((1, LANES)), _resident((1, SSM_WIDTH)), _resident((1, SSM_WIDTH)),
                  _resident((SSD_CHUNK, SSD_CHUNK)), _resident((2 * LANES, SSM_WIDTH))],
        out_specs=tile(SSM_WIDTH),
        out_shape=jax.ShapeDtypeStruct((batch, seq, SSM_WIDTH), F32),
        scratch_shapes=[pltpu.VMEM((SSM_STATE, SSM_WIDTH), F32)],
        compiler_params=pltpu.CompilerParams(dimension_semantics=("parallel", "arbitrary"),
                                             vmem_limit_bytes=32 * MIB),
        name="ssd",
    )(xs, bm, cm, dt, z, arow, dskip, norm, tri, expand)


def _out_ffn_kernel(x_ref, attn_ref, y_ref, wo_ref, mixpost_ref, pre_ref, wg_ref, wu_ref, wd_ref, post_ref, o_ref):
    attn = jnp.concatenate([attn_ref[0, j] for j in range(Q_SLABS)], axis=1).astype(BF16)
    mixed = (jnp.dot(attn, wo_ref[:ATTN_WIDTH, :], preferred_element_type=F32)
             + jnp.dot(y_ref[0].astype(BF16), wo_ref[ATTN_WIDTH:, :], preferred_element_type=F32))
    x = x_ref[0] + _rmsnorm(mixed, mixpost_ref[...])
    o_ref[0] = _swiglu_half_step(x, pre_ref[...], wg_ref, wu_ref, wd_ref, post_ref[...])


def _out_ffn(x3d, attn, y3d, wo, mixpost, pre, wg, wu, wd, post):
    batch, seq, _ = x3d.shape
    rows = FFN_ROWS
    row_spec = pl.BlockSpec((1, rows, D_MODEL), lambda b, j: (b, j, 0))
    slab_spec = pl.BlockSpec((1, Q_SLABS, rows, LANES), lambda b, j: (b, 0, j, 0))
    return pl.pallas_call(
        _out_ffn_kernel,
        grid=(batch, seq // rows),
        in_specs=[row_spec, slab_spec, row_spec, _resident((ATTN_WIDTH + SSM_WIDTH, D_MODEL)),
                  _resident((1, D_MODEL)), _resident((1, D_MODEL)), _resident((D_MODEL, D_FF)),
                  _resident((D_MODEL, D_FF)), _resident((D_FF, D_MODEL)), _resident((1, D_MODEL))],
        out_specs=row_spec,
        out_shape=jax.ShapeDtypeStruct((batch, seq, D_MODEL), F32),
        compiler_params=pltpu.CompilerParams(dimension_semantics=("parallel", "parallel"),
                                             vmem_limit_bytes=56 * MIB),
        name="out_ffn",
    )(x3d, attn, y3d, wo, mixpost, pre, wg, wu, wd, post)


def _slab_head_order():
    order = []
    for pair in range(KV_PAIRS):
        for g in range(Q_PER_KV):
            order += [(2 * pair) * Q_PER_KV + g, (2 * pair + 1) * Q_PER_KV + g]
    return order


def _head_columns(order):
    return jnp.concatenate([jnp.arange(h * HEAD_DIM, (h + 1) * HEAD_DIM) for h in order])


def _pad_lanes(v):
    return jnp.pad(v, [(0, 0)] * (v.ndim - 1) + [(0, LANES - v.shape[-1])])


def _layer(x, pos3d, invf, tri, expand, p):
    batch, seq, _ = x.shape
    tokens = batch * seq
    row = lambda v: v.reshape(1, -1)
    bf = lambda w: w.astype(BF16)

    x1 = _ffn(x.reshape(tokens, D_MODEL), row(p["ffn1_pre_norm"]), bf(p["ffn1_w_gate"]), bf(p["ffn1_w_up"]),
              bf(p["ffn1_w_down"]), row(p["ffn1_post_norm"])).reshape(batch, seq, D_MODEL)

    w_in = p["w_in"]
    o_k = ATTN_WIDTH
    o_v = o_k + KV_WIDTH
    o_x = o_v + KV_WIDTH
    o_z = o_x + CONV_CHANNELS
    o_dt = o_z + SSM_WIDTH
    q_cols = _head_columns(_slab_head_order())
    q, k, v, xs, bm, cm, z, dt = _in_proj(
        x1, pos3d, invf, row(p["mix_pre_norm"]),
        bf(w_in[:, :o_k][:, q_cols]), bf(w_in[:, o_k:o_v]), bf(w_in[:, o_v:o_x]),
        bf(w_in[:, o_x:o_z]), bf(w_in[:, o_z:o_dt]), bf(_pad_lanes(w_in[:, o_dt:])),
        p["conv_w"], row(p["conv_b"]), _pad_lanes(row(p["dt_bias"])))

    attn = _attention(q, k, v)

    arow = _pad_lanes(row(-jnp.exp(p["a_log"])))
    dskip = row(jnp.repeat(p["d_skip"], SSM_HEAD_DIM))
    y = _ssd(xs, bm, cm, dt, z, arow, dskip, row(p["ssm_norm"]), tri, expand)

    w_out = p["w_out"]
    w_out = jnp.concatenate([w_out[:ATTN_WIDTH][q_cols], w_out[ATTN_WIDTH:]], axis=0)
    return _out_ffn(x1, attn, y, bf(w_out), row(p["mix_post_norm"]), row(p["ffn2_pre_norm"]),
                    bf(p["ffn2_w_gate"]), bf(p["ffn2_w_up"]), bf(p["ffn2_w_down"]), row(p["ffn2_post_norm"]))


def kernel(x, positions, ffn1_pre_norm, ffn1_w_gate, ffn1_w_up, ffn1_w_down, ffn1_post_norm, mix_pre_norm, w_in, conv_w, conv_b, dt_bias, a_log, d_skip, ssm_norm, w_out, mix_post_norm, ffn2_pre_norm, ffn2_w_gate, ffn2_w_up, ffn2_w_down, ffn2_post_norm):
    params = dict(ffn1_pre_norm=ffn1_pre_norm, ffn1_w_gate=ffn1_w_gate, ffn1_w_up=ffn1_w_up,
                  ffn1_w_down=ffn1_w_down, ffn1_post_norm=ffn1_post_norm, mix_pre_norm=mix_pre_norm, w_in=w_in,
                  conv_w=conv_w, conv_b=conv_b, dt_bias=dt_bias, a_log=a_log, d_skip=d_skip, ssm_norm=ssm_norm,
                  w_out=w_out, mix_post_norm=mix_post_norm, ffn2_pre_norm=ffn2_pre_norm, ffn2_w_gate=ffn2_w_gate,
                  ffn2_w_up=ffn2_w_up, ffn2_w_down=ffn2_w_down, ffn2_post_norm=ffn2_post_norm)
    depth = w_in.shape[0]
    batch, seq, _ = x.shape
    inv_freq = ROPE_THETA ** (-jnp.arange(0, HEAD_DIM, 2, dtype=F32) / HEAD_DIM)
    invf = inv_freq.reshape(HEAD_DIM // 2, 1)
    pos3d = positions.reshape(batch, 1, seq)
    idx = jnp.arange(SSD_CHUNK)
    tri = (idx[:, None] >= idx[None, :]).astype(BF16)
    expand = (jnp.arange(2 * LANES)[:, None] % LANES == jnp.arange(SSM_WIDTH)[None, :] // SSM_HEAD_DIM).astype(BF16)
    for i in range(depth):
        x = _layer(x, pos3d, invf, tri, expand, {name: w[i] for name, w in params.items()})
    return x
```

```python
import functools
import math

import jax
import jax.numpy as jnp
from jax import lax
from jax.experimental import pallas as pl
from jax.experimental.pallas import tpu as pltpu

F32 = jnp.float32
BF16 = jnp.bfloat16

D_MODEL = 1024
D_FF = 2816
HEAD_DIM = 64
N_Q_HEADS = 16
N_KV_HEADS = 4
Q_PER_KV = N_Q_HEADS // N_KV_HEADS
ATTN_WIDTH = N_Q_HEADS * HEAD_DIM
KV_WIDTH = N_KV_HEADS * HEAD_DIM
ATTN_BLOCK = 128
ROPE_THETA = 10000.0
SSM_HEADS = 16
SSM_HEAD_DIM = 64
SSM_WIDTH = SSM_HEADS * SSM_HEAD_DIM
SSM_STATE = 128
SSM_GROUPS = 2
GROUP_WIDTH = SSM_WIDTH // SSM_GROUPS
BC_WIDTH = SSM_GROUPS * SSM_STATE
CONV_WIDTH = 4
CONV_CHANNELS = SSM_WIDTH + 2 * BC_WIDTH
SSD_CHUNK = 128
MACARON_WEIGHT = 0.5
NORM_EPS = 1e-6
LOG2_E = math.log2(math.e)

LANES = 128
CARRY_ROWS = 8
MIB = 1024 * 1024

FFN_ROWS = 512
FF_CHUNK = 256
PROJ_ROWS = 512
SSD_BATCH = 4

Q_SLABS = ATTN_WIDTH // LANES
KV_SLABS = KV_WIDTH // LANES
KV_PAIRS = N_KV_HEADS // 2


def _rmsnorm(x, gain):
    return x * lax.rsqrt(jnp.mean(x * x, axis=-1, keepdims=True) + NORM_EPS) * gain


def _resident(shape):
    return pl.BlockSpec(shape, lambda *_: (0,) * len(shape), pipeline_mode=pl.Buffered(1))


def _swiglu_half_step(x, pre, wg_ref, wu_ref, wd_ref, post):
    xn = _rmsnorm(x, pre).astype(BF16)
    acc = jnp.zeros(x.shape, F32)
    for c in range(D_FF // FF_CHUNK):
        cols = slice(c * FF_CHUNK, (c + 1) * FF_CHUNK)
        gate = jnp.dot(xn, wg_ref[:, cols], preferred_element_type=F32)
        up = jnp.dot(xn, wu_ref[:, cols], preferred_element_type=F32)
        hidden = (jax.nn.silu(gate) * up).astype(BF16)
        acc = acc + jnp.dot(hidden, wd_ref[cols, :], preferred_element_type=F32)
    return x + MACARON_WEIGHT * _rmsnorm(acc, post)


def _ffn_kernel(x_ref, pre_ref, wg_ref, wu_ref, wd_ref, post_ref, o_ref):
    o_ref[...] = _swiglu_half_step(x_ref[...], pre_ref[...], wg_ref, wu_ref, wd_ref, post_ref[...])


def _ffn(x2d, pre, wg, wu, wd, post):
    tokens = x2d.shape[0]
    row_spec = pl.BlockSpec((FFN_ROWS, D_MODEL), lambda i: (i, 0))
    return pl.pallas_call(
        _ffn_kernel,
        grid=(tokens // FFN_ROWS,),
        in_specs=[row_spec, _resident((1, D_MODEL)), _resident((D_MODEL, D_FF)), _resident((D_MODEL, D_FF)),
                  _resident((D_FF, D_MODEL)), _resident((1, D_MODEL))],
        out_specs=row_spec,
        out_shape=jax.ShapeDtypeStruct((tokens, D_MODEL), F32),
        compiler_params=pltpu.CompilerParams(dimension_semantics=("parallel",), vmem_limit_bytes=48 * MIB),
        name="ffn",
    )(x2d, pre, wg, wu, wd, post)


def _softplus(x):
    return jnp.maximum(x, 0.0) + jnp.log1p(jnp.exp(-jnp.abs(x)))


def _inproj_kernel(x_ref, pos_ref, invf_ref, gain_ref, wq_ref, wk_ref, wv_ref, wx_ref, wz_ref, wdt_ref,
                   convw_ref, convb_ref, dtb_ref,
                   q_ref, k_ref, v_ref, xs_ref, b_ref, c_ref, z_ref, dt_ref, ubuf_ref):
    rows = x_ref.shape[1]
    xn = _rmsnorm(x_ref[0], gain_ref[...]).astype(BF16)
    wide = 2 * LANES

    def project(w_ref, c):
        return jnp.dot(xn, w_ref[:, c * wide:(c + 1) * wide], preferred_element_type=F32)

    n_slabs = CONV_CHANNELS // LANES

    @pl.when(pl.program_id(1) == 0)
    def _():
        ubuf_ref[:, 0:CARRY_ROWS, :] = jnp.zeros((n_slabs, CARRY_ROWS, LANES), F32)

    def conv_chunk(c, u):
        for j in (2 * c, 2 * c + 1):
            cols = slice(j * LANES, (j + 1) * LANES)
            ubuf_ref[j, CARRY_ROWS:CARRY_ROWS + rows, :] = u[:, (j % 2) * LANES:(j % 2 + 1) * LANES]
            conv = (convb_ref[:, cols]
                    + ubuf_ref[j, CARRY_ROWS:CARRY_ROWS + rows, :] * convw_ref[CONV_WIDTH - 1:CONV_WIDTH, cols])
            for tap in range(CONV_WIDTH - 1):
                start = CARRY_ROWS - (CONV_WIDTH - 1) + tap
                conv = conv + ubuf_ref[j, start:start + rows, :] * convw_ref[tap:tap + 1, cols]
            act = jax.nn.silu(conv)
            if j < SSM_WIDTH // LANES:
                xs_ref[0, :, cols] = act
            elif j < (SSM_WIDTH + BC_WIDTH) // LANES:
                b_ref[0, :, j * LANES - SSM_WIDTH:(j + 1) * LANES - SSM_WIDTH] = act.astype(BF16)
            else:
                first = SSM_WIDTH + BC_WIDTH
                c_ref[0, :, j * LANES - first:(j + 1) * LANES - first] = act.astype(BF16)
            ubuf_ref[j, 0:CARRY_ROWS, :] = ubuf_ref[j, rows:rows + CARRY_ROWS, :]

    def z_chunk(c):
        z_ref[0, :, c * wide:(c + 1) * wide] = project(wz_ref, c)

    u = project(wx_ref, 0)
    z_chunk(0)

    ang_t = invf_ref[...] * pos_ref[0].astype(F32)
    cos_t = jnp.cos(ang_t)
    sin_t = jnp.sin(ang_t)
    copies = LANES // HEAD_DIM
    cos = jnp.concatenate([cos_t, cos_t] * copies, axis=0).T
    sin = jnp.concatenate([-sin_t, sin_t] * copies, axis=0).T
    lane = lax.broadcasted_iota(jnp.int32, (rows, LANES), 1)
    first_half = (lane % HEAD_DIM) < (HEAD_DIM // 2)

    def rope(t):
        back = pltpu.roll(t, HEAD_DIM // 2, 1)
        fwd = pltpu.roll(t, LANES - HEAD_DIM // 2, 1)
        return t * cos + jnp.where(first_half, fwd, back) * sin

    def q_chunk(c):
        q = project(wq_ref, c)
        for half in range(2):
            q_ref[0, 2 * c + half] = rope(q[:, half * LANES:(half + 1) * LANES]) * (HEAD_DIM ** -0.5 * LOG2_E)

    def kv_chunk(c):
        k = project(wk_ref, c)
        v = project(wv_ref, c)
        for half in range(2):
            k_ref[0, 2 * c + half] = rope(k[:, half * LANES:(half + 1) * LANES])
            v_ref[0, 2 * c + half] = v[:, half * LANES:(half + 1) * LANES]

    dt_ref[0] = _softplus(jnp.dot(xn, wdt_ref[...], preferred_element_type=F32) + dtb_ref[...])
    n_conv = CONV_CHANNELS // wide
    fillers = ([functools.partial(q_chunk, c) for c in range(ATTN_WIDTH // wide)]
               + [functools.partial(kv_chunk, c) for c in range(KV_WIDTH // wide)]
               + [functools.partial(z_chunk, c) for c in range(1, SSM_WIDTH // wide)])
    for c in range(n_conv):
        u_next = project(wx_ref, c + 1) if c + 1 < n_conv else None
        conv_chunk(c, u)
        fillers.pop(0)()
        u = u_next
    for filler in fillers:
        filler()


def _in_proj(x3d, pos3d, invf, gain, wq, wk, wv, wx, wz, wdt, convw, convb, dtb):
    batch, seq, _ = x3d.shape
    rows = PROJ_ROWS

    def tile(width):
        return pl.BlockSpec((1, rows, width), lambda b, j: (b, j, 0))

    def slabs(n):
        return pl.BlockSpec((1, n, rows, LANES), lambda b, j: (b, 0, j, 0))

    def out(width, dtype):
        return jax.ShapeDtypeStruct((batch, seq, width), dtype)

    def out_slabs(n):
        return jax.ShapeDtypeStruct((batch, n, seq, LANES), F32)

    return pl.pallas_call(
        _inproj_kernel,
        grid=(batch, seq // rows),
        in_specs=[tile(D_MODEL), pl.BlockSpec((1, 1, rows), lambda b, j: (b, 0, j)),
                  _resident((HEAD_DIM // 2, 1)), _resident((1, D_MODEL)),
                  _resident((D_MODEL, ATTN_WIDTH)), _resident((D_MODEL, KV_WIDTH)), _resident((D_MODEL, KV_WIDTH)),
                  _resident((D_MODEL, CONV_CHANNELS)), _resident((D_MODEL, SSM_WIDTH)),
                  _resident((D_MODEL, LANES)), _resident((CONV_WIDTH, CONV_CHANNELS)),
                  _resident((1, CONV_CHANNELS)), _resident((1, LANES))],
        out_specs=[slabs(Q_SLABS), slabs(KV_SLABS), slabs(KV_SLABS), tile(SSM_WIDTH), tile(BC_WIDTH),
                   tile(BC_WIDTH), tile(SSM_WIDTH), tile(LANES)],
        out_shape=[out_slabs(Q_SLABS), out_slabs(KV_SLABS), out_slabs(KV_SLABS), out(SSM_WIDTH, F32),
                   out(BC_WIDTH, BF16), out(BC_WIDTH, BF16), out(SSM_WIDTH, F32), out(LANES, F32)],
        scratch_shapes=[pltpu.VMEM((CONV_CHANNELS // LANES, rows + CARRY_ROWS, LANES), F32)],
        compiler_params=pltpu.CompilerParams(dimension_semantics=("parallel", "arbitrary"),
                                             vmem_limit_bytes=56 * MIB),
        name="in_proj",
    )(x3d, pos3d, invf, gain, wq, wk, wv, wx, wz, wdt, convw, convb, dtb)


STATE_ARRAYS = 3
MASKED = -2.0 ** 100
STATE_U, STATE_M, STATE_L = range(STATE_ARRAYS)


def _attn_blocks(blocks):
    blk = ATTN_BLOCK
    low = lax.broadcasted_iota(jnp.int32, (blk, LANES), 1) < HEAD_DIM
    zero = jnp.zeros((), BF16)
    one = jnp.ones((), BF16)

    scores = []
    for q_slabs, k_blk, _, (query_onehot, key_mask), _ in blocks:
        qb = [q.astype(BF16) for q in q_slabs]
        keys = jnp.concatenate([k_blk.astype(BF16), key_mask], axis=1)
        for masked in ([jnp.where(low, q, zero) for q in qb], [jnp.where(low, zero, q) for q in qb]):
            lhs = jnp.concatenate([jnp.concatenate(masked, axis=0), query_onehot], axis=1)
            scores.append(lax.dot_general(lhs, keys, (((1,), (1,)), ((), ())), preferred_element_type=F32))
    maxes = [jnp.max(s, axis=-1, keepdims=True) for s in scores]
    probs = [jnp.exp2((s - m).astype(BF16)) for s, m in zip(scores, maxes)]
    results = []
    for i, (_, _, v_blk, _, _) in enumerate(blocks):
        vb = v_blk.astype(BF16)
        key_low = lax.broadcasted_iota(jnp.int32, vb.shape, 1) < HEAD_DIM
        results.append(jnp.dot(probs[2 * i], jnp.where(key_low, vb, one), preferred_element_type=F32))
        results.append(jnp.dot(probs[2 * i + 1], jnp.where(key_low, one, vb), preferred_element_type=F32))
    states = []
    for i, (q_slabs, _, _, _, load_old) in enumerate(blocks):
        r_low, r_high, m_a, m_b = results[2 * i], results[2 * i + 1], maxes[2 * i], maxes[2 * i + 1]
        old = None if load_old is None else load_old()
        state = []
        for g in range(len(q_slabs)):
            rows = slice(g * blk, (g + 1) * blk)
            u_new = jnp.where(low, r_low[rows], r_high[rows])
            l_new = pltpu.roll(jnp.where(low, r_high[rows], r_low[rows]), HEAD_DIM, 1)
            m_new = jnp.where(low, m_a[rows], m_b[rows])
            if old is not None:
                u_run, m_run, l_run = old[g]
                m_tot = jnp.maximum(m_run, m_new)
                a = jnp.exp2(m_run - m_tot)
                b = jnp.exp2(m_new - m_tot)
                u_new = u_run * a + u_new * b
                l_new = l_run * a + l_new * b
                m_new = m_tot
            state.append((u_new, m_new, l_new))
        states.append(state)
    return states


def _attn_kernel(q_ref, k_ref, v_ref, o_ref, st4_ref, stt_ref):
    blk = ATTN_BLOCK
    seq = q_ref.shape[2]
    n_slabs = q_ref.shape[1]
    sub4 = seq // 4

    def key_mask(n_keys, shift):
        si = lax.broadcasted_iota(jnp.int32, (n_keys, blk), 0)
        qi = lax.broadcasted_iota(jnp.int32, (n_keys, blk), 1) + shift
        return jnp.where((si <= qi) & (si >= qi - blk), 0.0, MASKED).astype(BF16)

    stacked = n_slabs * blk
    query_onehot = (lax.broadcasted_iota(jnp.int32, (stacked, blk), 0) % blk
                    == lax.broadcasted_iota(jnp.int32, (stacked, blk), 1)).astype(BF16)
    band_bias = (query_onehot, key_mask(2 * blk, blk))
    causal_bias = (query_onehot, key_mask(blk, 0))

    def load_q(rows):
        return [q_ref[0, g, rows, :] for g in range(n_slabs)]

    def load_state(ref, rows):
        return [tuple(ref[a * n_slabs + g, rows, :] for a in range(STATE_ARRAYS)) for g in range(n_slabs)]

    def store_state(ref, rows, state):
        for g in range(n_slabs):
            for a in range(STATE_ARRAYS):
                ref[a * n_slabs + g, rows, :] = state[g][a]

    def store_out(rows, state):
        for g in range(n_slabs):
            o_ref[0, g, rows, :] = state[g][STATE_U] / state[g][STATE_L]

    def run(plans):
        states = _attn_blocks([(load_q(rows), k_ref[0, 0, keys, :], v_ref[0, 0, keys, :], bias, load_old)
                               for rows, keys, bias, load_old, _ in plans])
        for (rows, _, _, _, store), state in zip(plans, states):
            store(rows, state)

    def paired(n_blocks, plan, per_trip=2):
        def trip(i, carry):
            run([plan(per_trip * i + j) for j in range(per_trip)])
            return carry
        lax.fori_loop(0, n_blocks // per_trip, trip, 0)

    def d16_plan(r16):
        rows = pl.ds(r16, blk, stride=16)
        dest = pl.ds((r16 % 4) * sub4 + r16 // 4, blk, stride=4)
        return rows, rows, causal_bias, None, lambda _, state: store_state(st4_ref, dest, state)

    paired(16, d16_plan, per_trip=4)

    def to_token_order(rows, state):
        store_state(stt_ref, rows, state)

    def d4_first_plan(r4):
        rows = pl.ds(r4, blk, stride=4)
        src = pl.ds(pl.multiple_of(r4 * sub4, blk), blk)
        return rows, rows, causal_bias, lambda: load_state(st4_ref, src), to_token_order

    paired(4, d4_first_plan, per_trip=4)
    later_blocks = sub4 // blk - 1

    def d4_plan(idx):
        r4 = idx // later_blocks
        n = 1 + idx % later_blocks
        rows = pl.ds(r4 + 4 * blk * n, blk, stride=4)
        keys = pl.ds(r4 + 4 * blk * (n - 1), 2 * blk, stride=4)
        src = pl.ds(pl.multiple_of(r4 * sub4 + n * blk, blk), blk)
        return rows, keys, band_bias, lambda: load_state(st4_ref, src), to_token_order

    paired(4 * later_blocks, d4_plan, per_trip=3)

    def d1_plan(n):
        start = n * blk if isinstance(n, int) else pl.multiple_of(n * blk, blk)
        rows = pl.ds(start, blk)
        if isinstance(n, int) and n == 0:
            return rows, rows, causal_bias, lambda: load_state(stt_ref, rows), store_out
        return rows, pl.ds(start - blk, 2 * blk), band_bias, lambda: load_state(stt_ref, rows), store_out

    last = seq // blk - 1
    run([d1_plan(0)])
    paired(last, lambda i: d1_plan(i + 1), per_trip=3)


def _attention(q, k, v):
    batch, _, seq, _ = q.shape
    slabs_per_pair = Q_SLABS // KV_PAIRS
    q_spec = pl.BlockSpec((1, slabs_per_pair, seq, LANES), lambda b, p: (b, p, 0, 0))
    kv_spec = pl.BlockSpec((1, 1, seq, LANES), lambda b, p: (b, p, 0, 0))
    state_shape = (STATE_ARRAYS * slabs_per_pair, seq, LANES)
    return pl.pallas_call(
        _attn_kernel,
        grid=(batch, KV_PAIRS),
        in_specs=[q_spec, kv_spec, kv_spec],
        out_specs=q_spec,
        out_shape=jax.ShapeDtypeStruct(q.shape, F32),
        scratch_shapes=[pltpu.VMEM(state_shape, F32), pltpu.VMEM(state_shape, F32)],
        compiler_params=pltpu.CompilerParams(dimension_semantics=("parallel", "parallel"),
                                             vmem_limit_bytes=56 * MIB),
        name="attention",
    )(q, k, v)


def _split_bf16(x, pieces):
    parts = []
    for _ in range(pieces):
        part = x.astype(BF16)
        parts.append(part)
        x = x - part.astype(F32)
    return parts


def _ssd_kernel(xs_ref, b_ref, c_ref, dt_ref, z_ref, arow_ref, dskip_ref, norm_ref, tri_ref, expand_ref,
                y_ref, state_ref):
    chunk = SSD_CHUNK
    lanes = range(xs_ref.shape[0])
    heads_per_group = SSM_HEADS // SSM_GROUPS

    @pl.when(pl.program_id(1) == 0)
    def _():
        state_ref[...] = jnp.zeros(state_ref.shape, F32)

    row = lax.broadcasted_iota(jnp.int32, (chunk, chunk), 0)
    col = lax.broadcasted_iota(jnp.int32, (chunk, chunk), 1)
    causal = row >= col
    low = lax.broadcasted_iota(jnp.int32, (chunk, LANES), 1) < SSM_HEAD_DIM
    zero = jnp.zeros((), BF16)

    def group_cols(g):
        return slice(g * SSM_STATE, (g + 1) * SSM_STATE), slice(g * GROUP_WIDTH, (g + 1) * GROUP_WIDTH)

    dts = [dt_ref[i] for i in lanes]
    a3s = [jnp.dot(tri_ref[...], jnp.concatenate(_split_bf16(dt * arow_ref[...], 3), axis=1),
                   preferred_element_type=F32) for dt in dts]
    a_css = [a3[:, :LANES] + a3[:, LANES:2 * LANES] + a3[:, 2 * LANES:] for a3 in a3s]

    def expand(stat):
        return jnp.dot(jnp.concatenate(_split_bf16(stat, 2), axis=1), expand_ref[...], preferred_element_type=F32)

    w_states = [expand(jnp.exp(a_cs[chunk - 1:chunk, :] - a_cs) * dt) for a_cs, dt in zip(a_css, dts)]
    scale_offs = [expand(jnp.exp(a_cs)) for a_cs in a_css]
    a_cs_ts = [a_cs.T for a_cs in a_css]
    dt_ts = [dt.T for dt in dts]

    cbs, y_offs, new_states = {}, {}, {}
    for i in lanes:
        xw_bf = (xs_ref[i] * w_states[i]).astype(BF16)
        for g in range(SSM_GROUPS):
            gcols, wide = group_cols(g)
            b_g = b_ref[i, :, gcols]
            c_g = c_ref[i, :, gcols]
            cbs[i, g] = lax.dot_general(c_g, b_g, (((1,), (1,)), ((), ())), preferred_element_type=F32)
            y_offs[i, g] = jnp.dot(c_g, state_ref[i, :, wide].astype(BF16), preferred_element_type=F32)
            b_t = b_g.astype(F32).T.astype(BF16)
            new_states[i, g] = jnp.dot(b_t, xw_bf[:, wide], preferred_element_type=F32)

    y_diags = {}
    for i in lanes:
        xs_bf = xs_ref[i].astype(BF16)
        for g in range(SSM_GROUPS):
            parts = []
            for pair in range(heads_per_group // 2):
                mats = []
                for e in range(2):
                    h = g * heads_per_group + 2 * pair + e
                    seg = a_css[i][:, h:h + 1] - a_cs_ts[i][h:h + 1, :]
                    decay = jnp.exp(jnp.where(causal, seg, -jnp.inf))
                    mats.append((cbs[i, g] * decay * dt_ts[i][h:h + 1, :]).astype(BF16))
                first = (g * heads_per_group + 2 * pair) * SSM_HEAD_DIM
                x_pair = xs_bf[:, first:first + 2 * SSM_HEAD_DIM]
                rhs = jnp.concatenate([jnp.where(low, x_pair, zero), jnp.where(low, zero, x_pair)], axis=0)
                parts.append(jnp.dot(jnp.concatenate(mats, axis=1), rhs, preferred_element_type=F32))
            y_diags[i, g] = jnp.concatenate(parts, axis=1)

    for i in lanes:
        chunk_decay = scale_offs[i][chunk - 1:chunk, :]
        y_parts = []
        for g in range(SSM_GROUPS):
            _, wide = group_cols(g)
            y_parts.append(y_diags[i, g] + y_offs[i, g] * scale_offs[i][:, wide])
            state_ref[i, :, wide] = state_ref[i, :, wide] * chunk_decay[:, wide] + new_states[i, g]
        y = jnp.concatenate(y_parts, axis=1) + dskip_ref[...] * xs_ref[i]
        y = y * jax.nn.silu(z_ref[i])
        normed = []
        for g in range(SSM_GROUPS):
            y_g = y[:, g * GROUP_WIDTH:(g + 1) * GROUP_WIDTH]
            normed.append(y_g * lax.rsqrt(jnp.mean(y_g * y_g, axis=-1, keepdims=True) + NORM_EPS))
        y_ref[i] = jnp.concatenate(normed, axis=1) * norm_ref[...]


def _ssd(xs, bm, cm, dt, z, arow, dskip, norm, tri, expand):
    batch, seq, _ = xs.shape

    def tile(width):
        return pl.BlockSpec((SSD_BATCH, SSD_CHUNK, width), lambda b, c: (b, c, 0))

    return pl.pallas_call(
        _ssd_kernel,
        grid=(batch // SSD_BATCH, seq // SSD_CHUNK),
        in_specs=[tile(SSM_WIDTH), tile(BC_WIDTH), tile(BC_WIDTH), tile(LANES), tile(SSM_WIDTH),
                  _resident((1, LANES)), _resident((1, SSM_WIDTH)), _resident((1, SSM_WIDTH)),
                  _resident((SSD_CHUNK, SSD_CHUNK)), _resident((2 * LANES, SSM_WIDTH))],
        out_specs=tile(SSM_WIDTH),
        out_shape=jax.ShapeDtypeStruct((batch, seq, SSM_WIDTH), F32),
        scratch_shapes=[pltpu.VMEM((SSD_BATCH, SSM_STATE, SSM_WIDTH), F32)],
        compiler_params=pltpu.CompilerParams(dimension_semantics=("parallel", "arbitrary"),
                                             vmem_limit_bytes=32 * MIB),
        name="ssd",
    )(xs, bm, cm, dt, z, arow, dskip, norm, tri, expand)


def _out_ffn_kernel(x_ref, attn_ref, y_ref, wo_ref, mixpost_ref, pre_ref, wg_ref, wu_ref, wd_ref, post_ref, o_ref):
    attn = jnp.concatenate([attn_ref[0, j] for j in range(Q_SLABS)], axis=1).astype(BF16)
    mixed = (jnp.dot(attn, wo_ref[:ATTN_WIDTH, :], preferred_element_type=F32)
             + jnp.dot(y_ref[0].astype(BF16), wo_ref[ATTN_WIDTH:, :], preferred_element_type=F32))
    x = x_ref[0] + _rmsnorm(mixed, mixpost_ref[...])
    o_ref[0] = _swiglu_half_step(x, pre_ref[...], wg_ref, wu_ref, wd_ref, post_ref[...])


def _out_ffn(x3d, attn, y3d, wo, mixpost, pre, wg, wu, wd, post):
    batch, seq, _ = x3d.shape
    rows = FFN_ROWS
    row_spec = pl.BlockSpec((1, rows, D_MODEL), lambda b, j: (b, j, 0))
    slab_spec = pl.BlockSpec((1, Q_SLABS, rows, LANES), lambda b, j: (b, 0, j, 0))
    return pl.pallas_call(
        _out_ffn_kernel,
        grid=(batch, seq // rows),
        in_specs=[row_spec, slab_spec, row_spec, _resident((ATTN_WIDTH + SSM_WIDTH, D_MODEL)),
                  _resident((1, D_MODEL)), _resident((1, D_MODEL)), _resident((D_MODEL, D_FF)),
                  _resident((D_MODEL, D_FF)), _resident((D_FF, D_MODEL)), _resident((1, D_MODEL))],
        out_specs=row_spec,
        out_shape=jax.ShapeDtypeStruct((batch, seq, D_MODEL), F32),
        compiler_params=pltpu.CompilerParams(dimension_semantics=("parallel", "parallel"),
                                             vmem_limit_bytes=56 * MIB),
        name="out_ffn",
    )(x3d, attn, y3d, wo, mixpost, pre, wg, wu, wd, post)


def _slab_head_order():
    order = []
    for pair in range(KV_PAIRS):
        for g in range(Q_PER_KV):
            order += [(2 * pair) * Q_PER_KV + g, (2 * pair + 1) * Q_PER_KV + g]
    return order


def _head_columns(order):
    return jnp.concatenate([jnp.arange(h * HEAD_DIM, (h + 1) * HEAD_DIM) for h in order])


def _pad_lanes(v):
    return jnp.pad(v, [(0, 0)] * (v.ndim - 1) + [(0, LANES - v.shape[-1])])


def _layer(x, pos3d, invf, tri, expand, p):
    batch, seq, _ = x.shape
    tokens = batch * seq
    row = lambda v: v.reshape(1, -1)
    bf = lambda w: w.astype(BF16)

    x1 = _ffn(x.reshape(tokens, D_MODEL), row(p["ffn1_pre_norm"]), bf(p["ffn1_w_gate"]), bf(p["ffn1_w_up"]),
              bf(p["ffn1_w_down"]), row(p["ffn1_post_norm"])).reshape(batch, seq, D_MODEL)

    w_in = p["w_in"]
    o_k = ATTN_WIDTH
    o_v = o_k + KV_WIDTH
    o_x = o_v + KV_WIDTH
    o_z = o_x + CONV_CHANNELS
    o_dt = o_z + SSM_WIDTH
    q_cols = _head_columns(_slab_head_order())
    q, k, v, xs, bm, cm, z, dt = _in_proj(
        x1, pos3d, invf, row(p["mix_pre_norm"]),
        bf(w_in[:, :o_k][:, q_cols]), bf(w_in[:, o_k:o_v]), bf(w_in[:, o_v:o_x]),
        bf(w_in[:, o_x:o_z]), bf(w_in[:, o_z:o_dt]), bf(_pad_lanes(w_in[:, o_dt:])),
        p["conv_w"], row(p["conv_b"]), _pad_lanes(row(p["dt_bias"])))

    attn = _attention(q, k, v)

    arow = _pad_lanes(row(-jnp.exp(p["a_log"])))
    dskip = row(jnp.repeat(p["d_skip"], SSM_HEAD_DIM))
    y = _ssd(xs, bm, cm, dt, z, arow, dskip, row(p["ssm_norm"]), tri, expand)

    w_out = p["w_out"]
    w_out = jnp.concatenate([w_out[:ATTN_WIDTH][q_cols], w_out[ATTN_WIDTH:]], axis=0)
    return _out_ffn(x1, attn, y, bf(w_out), row(p["mix_post_norm"]), row(p["ffn2_pre_norm"]),
                    bf(p["ffn2_w_gate"]), bf(p["ffn2_w_up"]), bf(p["ffn2_w_down"]), row(p["ffn2_post_norm"]))


def kernel(x, positions, ffn1_pre_norm, ffn1_w_gate, ffn1_w_up, ffn1_w_down, ffn1_post_norm, mix_pre_norm, w_in, conv_w, conv_b, dt_bias, a_log, d_skip, ssm_norm, w_out, mix_post_norm, ffn2_pre_norm, ffn2_w_gate, ffn2_w_up, ffn2_w_down, ffn2_post_norm):
    params = dict(ffn1_pre_norm=ffn1_pre_norm, ffn1_w_gate=ffn1_w_gate, ffn1_w_up=ffn1_w_up,
                  ffn1_w_down=ffn1_w_down, ffn1_post_norm=ffn1_post_norm, mix_pre_norm=mix_pre_norm, w_in=w_in,
                  conv_w=conv_w, conv_b=conv_b, dt_bias=dt_bias, a_log=a_log, d_skip=d_skip, ssm_norm=ssm_norm,
                  w_out=w_out, mix_post_norm=mix_post_norm, ffn2_pre_norm=ffn2_pre_norm, ffn2_w_gate=ffn2_w_gate,
                  ffn2_w_up=ffn2_w_up, ffn2_w_down=ffn2_w_down, ffn2_post_norm=ffn2_post_norm)
    depth = w_in.shape[0]
    batch, seq, _ = x.shape
    inv_freq = ROPE_THETA ** (-jnp.arange(0, HEAD_DIM, 2, dtype=F32) / HEAD_DIM)
    invf = inv_freq.reshape(HEAD_DIM // 2, 1)
    pos3d = positions.reshape(batch, 1, seq)
    idx = jnp.arange(SSD_CHUNK)
    tri = (idx[:, None] >= idx[None, :]).astype(BF16)
    expand = (jnp.arange(2 * LANES)[:, None] % LANES == jnp.arange(SSM_WIDTH)[None, :] // SSM_HEAD_DIM).astype(BF16)
    for i in range(depth):
        x = _layer(x, pos3d, invf, tri, expand, {name: w[i] for name, w in params.items()})
    return x
```

```python
import functools
import math

import jax
import jax.numpy as jnp
from jax import lax
from jax.experimental import pallas as pl
from jax.experimental.pallas import tpu as pltpu

F32 = jnp.float32
BF16 = jnp.bfloat16

D_MODEL = 1024
D_FF = 2816
HEAD_DIM = 64
N_Q_HEADS = 16
N_KV_HEADS = 4
Q_PER_KV = N_Q_HEADS // N_KV_HEADS
ATTN_WIDTH = N_Q_HEADS * HEAD_DIM
KV_WIDTH = N_KV_HEADS * HEAD_DIM
ATTN_BLOCK = 128
ROPE_THETA = 10000.0
SSM_HEADS = 16
SSM_HEAD_DIM = 64
SSM_WIDTH = SSM_HEADS * SSM_HEAD_DIM
SSM_STATE = 128
SSM_GROUPS = 2
GROUP_WIDTH = SSM_WIDTH // SSM_GROUPS
BC_WIDTH = SSM_GROUPS * SSM_STATE
CONV_WIDTH = 4
CONV_CHANNELS = SSM_WIDTH + 2 * BC_WIDTH
SSD_CHUNK = 128
MACARON_WEIGHT = 0.5
NORM_EPS = 1e-6
LOG2_E = math.log2(math.e)

LANES = 128
CARRY_ROWS = 8
MIB = 1024 * 1024

FFN_ROWS = 512
FFN_TILES_PER_STEP = 2
FF_CHUNK = 256
PROJ_ROWS = 512
SSD_BATCH = 4

Q_SLABS = ATTN_WIDTH // LANES
KV_SLABS = KV_WIDTH // LANES
KV_PAIRS = N_KV_HEADS // 2


def _rmsnorm(x, gain):
    return x * lax.rsqrt(jnp.mean(x * x, axis=-1, keepdims=True) + NORM_EPS) * gain


def _resident(shape):
    return pl.BlockSpec(shape, lambda *_: (0,) * len(shape), pipeline_mode=pl.Buffered(1))


def _swiglu_half_step(x, pre, wg_ref, wu_ref, wd_ref, post):
    return _swiglu_half_steps([x], pre, wg_ref, wu_ref, wd_ref, post)[0]


def _swiglu_half_steps(tiles, pre, wg_ref, wu_ref, wd_ref, post):
    results = []
    finish = None
    xn = _rmsnorm(tiles[0], pre).astype(BF16)
    for t, x in enumerate(tiles):
        acc = jnp.zeros(x.shape, F32)
        xn_next = None
        for c in range(D_FF // FF_CHUNK):
            cols = slice(c * FF_CHUNK, (c + 1) * FF_CHUNK)
            gate = jnp.dot(xn, wg_ref[:, cols], preferred_element_type=F32)
            up = jnp.dot(xn, wu_ref[:, cols], preferred_element_type=F32)
            hidden = (jax.nn.silu(gate) * up).astype(BF16)
            acc = acc + jnp.dot(hidden, wd_ref[cols, :], preferred_element_type=F32)
            if c == 1 and finish is not None:
                results.append(finish())
            if c == 2 and t + 1 < len(tiles):
                xn_next = _rmsnorm(tiles[t + 1], pre).astype(BF16)
        finish = functools.partial(lambda x_t, acc_t: x_t + MACARON_WEIGHT * _rmsnorm(acc_t, post), x, acc)
        xn = xn_next
    results.append(finish())
    return results


def _ffn_kernel(x_ref, pre_ref, wg_ref, wu_ref, wd_ref, post_ref, o_ref):
    tiles = [x_ref[t * FFN_ROWS:(t + 1) * FFN_ROWS, :] for t in range(FFN_TILES_PER_STEP)]
    outs = _swiglu_half_steps(tiles, pre_ref[...], wg_ref, wu_ref, wd_ref, post_ref[...])
    for t, out in enumerate(outs):
        o_ref[t * FFN_ROWS:(t + 1) * FFN_ROWS, :] = out


def _ffn(x2d, pre, wg, wu, wd, post):
    tokens = x2d.shape[0]
    step_rows = FFN_ROWS * FFN_TILES_PER_STEP
    row_spec = pl.BlockSpec((step_rows, D_MODEL), lambda i: (i, 0))
    return pl.pallas_call(
        _ffn_kernel,
        grid=(tokens // step_rows,),
        in_specs=[row_spec, _resident((1, D_MODEL)), _resident((D_MODEL, D_FF)), _resident((D_MODEL, D_FF)),
                  _resident((D_FF, D_MODEL)), _resident((1, D_MODEL))],
        out_specs=row_spec,
        out_shape=jax.ShapeDtypeStruct((tokens, D_MODEL), F32),
        compiler_params=pltpu.CompilerParams(dimension_semantics=("parallel",), vmem_limit_bytes=48 * MIB),
        name="ffn",
    )(x2d, pre, wg, wu, wd, post)


def _softplus(x):
    return jnp.maximum(x, 0.0) + jnp.log1p(jnp.exp(-jnp.abs(x)))


def _inproj_kernel(x_ref, pos_ref, invf_ref, gain_ref, wq_ref, wk_ref, wv_ref, wx_ref, wz_ref, wdt_ref,
                   convw_ref, convb_ref, dtb_ref,
                   q_ref, k_ref, v_ref, xs_ref, b_ref, c_ref, z_ref, dt_ref, ubuf_ref):
    rows = x_ref.shape[1]
    xn = _rmsnorm(x_ref[0], gain_ref[...]).astype(BF16)
    wide = 2 * LANES

    def project(w_ref, c):
        return jnp.dot(xn, w_ref[:, c * wide:(c + 1) * wide], preferred_element_type=F32)

    n_slabs = CONV_CHANNELS // LANES

    @pl.when(pl.program_id(1) == 0)
    def _():
        ubuf_ref[:, 0:CARRY_ROWS, :] = jnp.zeros((n_slabs, CARRY_ROWS, LANES), F32)

    def conv_chunk(c, u):
        for j in (2 * c, 2 * c + 1):
            cols = slice(j * LANES, (j + 1) * LANES)
            ubuf_ref[j, CARRY_ROWS:CARRY_ROWS + rows, :] = u[:, (j % 2) * LANES:(j % 2 + 1) * LANES]
            conv = (convb_ref[:, cols]
                    + ubuf_ref[j, CARRY_ROWS:CARRY_ROWS + rows, :] * convw_ref[CONV_WIDTH - 1:CONV_WIDTH, cols])
            for tap in range(CONV_WIDTH - 1):
                start = CARRY_ROWS - (CONV_WIDTH - 1) + tap
                conv = conv + ubuf_ref[j, start:start + rows, :] * convw_ref[tap:tap + 1, cols]
            act = jax.nn.silu(conv)
            if j < SSM_WIDTH // LANES:
                xs_ref[0, :, cols] = act
            elif j < (SSM_WIDTH + BC_WIDTH) // LANES:
                b_ref[0, :, j * LANES - SSM_WIDTH:(j + 1) * LANES - SSM_WIDTH] = act.astype(BF16)
            else:
                first = SSM_WIDTH + BC_WIDTH
                c_ref[0, :, j * LANES - first:(j + 1) * LANES - first] = act.astype(BF16)
            ubuf_ref[j, 0:CARRY_ROWS, :] = ubuf_ref[j, rows:rows + CARRY_ROWS, :]

    def z_chunk(c):
        z_ref[0, :, c * wide:(c + 1) * wide] = project(wz_ref, c)

    u = project(wx_ref, 0)
    z_chunk(0)

    ang_t = invf_ref[...] * pos_ref[0].astype(F32)
    cos_t = jnp.cos(ang_t)
    sin_t = jnp.sin(ang_t)
    copies = LANES // HEAD_DIM
    cos = jnp.concatenate([cos_t, cos_t] * copies, axis=0).T
    sin = jnp.concatenate([-sin_t, sin_t] * copies, axis=0).T
    lane = lax.broadcasted_iota(jnp.int32, (rows, LANES), 1)
    first_half = (lane % HEAD_DIM) < (HEAD_DIM // 2)

    def rope(t):
        back = pltpu.roll(t, HEAD_DIM // 2, 1)
        fwd = pltpu.roll(t, LANES - HEAD_DIM // 2, 1)
        return t * cos + jnp.where(first_half, fwd, back) * sin

    def q_chunk(c):
        q = project(wq_ref, c)
        for half in range(2):
            q_ref[0, 2 * c + half] = rope(q[:, half * LANES:(half + 1) * LANES]) * (HEAD_DIM ** -0.5 * LOG2_E)

    def kv_chunk(c):
        k = project(wk_ref, c)
        v = project(wv_ref, c)
        for half in range(2):
            k_ref[0, 2 * c + half] = rope(k[:, half * LANES:(half + 1) * LANES])
            v_ref[0, 2 * c + half] = v[:, half * LANES:(half + 1) * LANES]

    dt_ref[0] = _softplus(jnp.dot(xn, wdt_ref[...], preferred_element_type=F32) + dtb_ref[...])
    n_conv = CONV_CHANNELS // wide
    fillers = ([functools.partial(q_chunk, c) for c in range(ATTN_WIDTH // wide)]
               + [functools.partial(kv_chunk, c) for c in range(KV_WIDTH // wide)]
               + [functools.partial(z_chunk, c) for c in range(1, SSM_WIDTH // wide)])
    for c in range(n_conv):
        u_next = project(wx_ref, c + 1) if c + 1 < n_conv else None
        conv_chunk(c, u)
        fillers.pop(0)()
        u = u_next
    for filler in fillers:
        filler()


def _in_proj(x3d, pos3d, invf, gain, wq, wk, wv, wx, wz, wdt, convw, convb, dtb):
    batch, seq, _ = x3d.shape
    rows = PROJ_ROWS

    def tile(width):
        return pl.BlockSpec((1, rows, width), lambda b, j: (b, j, 0))

    def slabs(n):
        return pl.BlockSpec((1, n, rows, LANES), lambda b, j: (b, 0, j, 0))

    def out(width, dtype):
        return jax.ShapeDtypeStruct((batch, seq, width), dtype)

    def out_slabs(n):
        return jax.ShapeDtypeStruct((batch, n, seq, LANES), F32)

    return pl.pallas_call(
        _inproj_kernel,
        grid=(batch, seq // rows),
        in_specs=[tile(D_MODEL), pl.BlockSpec((1, 1, rows), lambda b, j: (b, 0, j)),
                  _resident((HEAD_DIM // 2, 1)), _resident((1, D_MODEL)),
                  _resident((D_MODEL, ATTN_WIDTH)), _resident((D_MODEL, KV_WIDTH)), _resident((D_MODEL, KV_WIDTH)),
                  _resident((D_MODEL, CONV_CHANNELS)), _resident((D_MODEL, SSM_WIDTH)),
                  _resident((D_MODEL, LANES)), _resident((CONV_WIDTH, CONV_CHANNELS)),
                  _resident((1, CONV_CHANNELS)), _resident((1, LANES))],
        out_specs=[slabs(Q_SLABS), slabs(KV_SLABS), slabs(KV_SLABS), tile(SSM_WIDTH), tile(BC_WIDTH),
                   tile(BC_WIDTH), tile(SSM_WIDTH), tile(LANES)],
        out_shape=[out_slabs(Q_SLABS), out_slabs(KV_SLABS), out_slabs(KV_SLABS), out(SSM_WIDTH, F32),
                   out(BC_WIDTH, BF16), out(BC_WIDTH, BF16), out(SSM_WIDTH, F32), out(LANES, F32)],
        scratch_shapes=[pltpu.VMEM((CONV_CHANNELS // LANES, rows + CARRY_ROWS, LANES), F32)],
        compiler_params=pltpu.CompilerParams(dimension_semantics=("parallel", "arbitrary"),
                                             vmem_limit_bytes=56 * MIB),
        name="in_proj",
    )(x3d, pos3d, invf, gain, wq, wk, wv, wx, wz, wdt, convw, convb, dtb)


STATE_ARRAYS = 3
MASKED = -2.0 ** 100
STATE_U, STATE_M, STATE_L = range(STATE_ARRAYS)


def _attn_blocks(blocks):
    blk = ATTN_BLOCK
    low = lax.broadcasted_iota(jnp.int32, (blk, LANES), 1) < HEAD_DIM
    zero = jnp.zeros((), BF16)
    one = jnp.ones((), BF16)

    scores = []
    for q_slabs, k_blk, _, (query_onehot, key_mask), _ in blocks:
        qb = [q.astype(BF16) for q in q_slabs]
        keys = jnp.concatenate([k_blk.astype(BF16), key_mask], axis=1)
        for masked in ([jnp.where(low, q, zero) for q in qb], [jnp.where(low, zero, q) for q in qb]):
            lhs = jnp.concatenate([jnp.concatenate(masked, axis=0), query_onehot], axis=1)
            scores.append(lax.dot_general(lhs, keys, (((1,), (1,)), ((), ())), preferred_element_type=F32))
    maxes = [jnp.max(s, axis=-1, keepdims=True) for s in scores]
    probs = [jnp.exp2((s - m).astype(BF16)) for s, m in zip(scores, maxes)]
    results = []
    for i, (_, _, v_blk, _, _) in enumerate(blocks):
        vb = v_blk.astype(BF16)
        key_low = lax.broadcasted_iota(jnp.int32, vb.shape, 1) < HEAD_DIM
        results.append(jnp.dot(probs[2 * i], jnp.where(key_low, vb, one), preferred_element_type=F32))
        results.append(jnp.dot(probs[2 * i + 1], jnp.where(key_low, one, vb), preferred_element_type=F32))
    states = []
    for i, (q_slabs, _, _, _, load_old) in enumerate(blocks):
        r_low, r_high, m_a, m_b = results[2 * i], results[2 * i + 1], maxes[2 * i], maxes[2 * i + 1]
        old = None if load_old is None else load_old()
        state = []
        for g in range(len(q_slabs)):
            rows = slice(g * blk, (g + 1) * blk)
            u_new = jnp.where(low, r_low[rows], r_high[rows])
            l_new = pltpu.roll(jnp.where(low, r_high[rows], r_low[rows]), HEAD_DIM, 1)
            m_new = jnp.where(low, m_a[rows], m_b[rows])
            if old is not None:
                u_run, m_run, l_run = old[g]
                m_tot = jnp.maximum(m_run, m_new)
                a = jnp.exp2(m_run - m_tot)
                b = jnp.exp2(m_new - m_tot)
                u_new = u_run * a + u_new * b
                l_new = l_run * a + l_new * b
                m_new = m_tot
            state.append((u_new, m_new, l_new))
        states.append(state)
    return states


def _attn_kernel(q_ref, k_ref, v_ref, o_ref, st4_ref, stt_ref):
    blk = ATTN_BLOCK
    seq = q_ref.shape[2]
    n_slabs = q_ref.shape[1]
    sub4 = seq // 4

    def key_mask(n_keys, shift):
        si = lax.broadcasted_iota(jnp.int32, (n_keys, blk), 0)
        qi = lax.broadcasted_iota(jnp.int32, (n_keys, blk), 1) + shift
        return jnp.where((si <= qi) & (si >= qi - blk), 0.0, MASKED).astype(BF16)

    stacked = n_slabs * blk
    query_onehot = (lax.broadcasted_iota(jnp.int32, (stacked, blk), 0) % blk
                    == lax.broadcasted_iota(jnp.int32, (stacked, blk), 1)).astype(BF16)
    band_bias = (query_onehot, key_mask(2 * blk, blk))
    causal_bias = (query_onehot, key_mask(blk, 0))

    def load_q(rows):
        return [q_ref[0, g, rows, :] for g in range(n_slabs)]

    def load_state(ref, rows):
        return [tuple(ref[a * n_slabs + g, rows, :] for a in range(STATE_ARRAYS)) for g in range(n_slabs)]

    def store_state(ref, rows, state):
        for g in range(n_slabs):
            for a in range(STATE_ARRAYS):
                ref[a * n_slabs + g, rows, :] = state[g][a]

    def store_out(rows, state):
        for g in range(n_slabs):
            o_ref[0, g, rows, :] = state[g][STATE_U] / state[g][STATE_L]

    def run(plans):
        states = _attn_blocks([(load_q(rows), k_ref[0, 0, keys, :], v_ref[0, 0, keys, :], bias, load_old)
                               for rows, keys, bias, load_old, _ in plans])
        for (rows, _, _, _, store), state in zip(plans, states):
            store(rows, state)

    def paired(n_blocks, plan, per_trip=2):
        def trip(i, carry):
            run([plan(per_trip * i + j) for j in range(per_trip)])
            return carry
        lax.fori_loop(0, n_blocks // per_trip, trip, 0)

    def d16_plan(r16):
        rows = pl.ds(r16, blk, stride=16)
        dest = pl.ds((r16 % 4) * sub4 + r16 // 4, blk, stride=4)
        return rows, rows, causal_bias, None, lambda _, state: store_state(st4_ref, dest, state)

    paired(16, d16_plan, per_trip=4)

    def to_token_order(rows, state):
        store_state(stt_ref, rows, state)

    def d4_first_plan(r4):
        rows = pl.ds(r4, blk, stride=4)
        src = pl.ds(pl.multiple_of(r4 * sub4, blk), blk)
        return rows, rows, causal_bias, lambda: load_state(st4_ref, src), to_token_order

    paired(4, d4_first_plan, per_trip=4)
    later_blocks = sub4 // blk - 1

    def d4_plan(idx):
        r4 = idx // later_blocks
        n = 1 + idx % later_blocks
        rows = pl.ds(r4 + 4 * blk * n, blk, stride=4)
        keys = pl.ds(r4 + 4 * blk * (n - 1), 2 * blk, stride=4)
        src = pl.ds(pl.multiple_of(r4 * sub4 + n * blk, blk), blk)
        return rows, keys, band_bias, lambda: load_state(st4_ref, src), to_token_order

    paired(4 * later_blocks, d4_plan, per_trip=3)

    def d1_plan(n):
        start = n * blk if isinstance(n, int) else pl.multiple_of(n * blk, blk)
        rows = pl.ds(start, blk)
        if isinstance(n, int) and n == 0:
            return rows, rows, causal_bias, lambda: load_state(stt_ref, rows), store_out
        return rows, pl.ds(start - blk, 2 * blk), band_bias, lambda: load_state(stt_ref, rows), store_out

    last = seq // blk - 1
    run([d1_plan(0)])
    paired(last, lambda i: d1_plan(i + 1), per_trip=3)


def _attention(q, k, v):
    batch, _, seq, _ = q.shape
    slabs_per_pair = Q_SLABS // KV_PAIRS
    q_spec = pl.BlockSpec((1, slabs_per_pair, seq, LANES), lambda b, p: (b, p, 0, 0))
    kv_spec = pl.BlockSpec((1, 1, seq, LANES), lambda b, p: (b, p, 0, 0))
    state_shape = (STATE_ARRAYS * slabs_per_pair, seq, LANES)
    return pl.pallas_call(
        _attn_kernel,
        grid=(batch, KV_PAIRS),
        in_specs=[q_spec, kv_spec, kv_spec],
        out_specs=q_spec,
        out_shape=jax.ShapeDtypeStruct(q.shape, F32),
        scratch_shapes=[pltpu.VMEM(state_shape, F32), pltpu.VMEM(state_shape, F32)],
        compiler_params=pltpu.CompilerParams(dimension_semantics=("parallel", "parallel"),
                                             vmem_limit_bytes=56 * MIB),
        name="attention",
    )(q, k, v)


def _split_bf16(x, pieces):
    parts = []
    for _ in range(pieces):
        part = x.astype(BF16)
        parts.append(part)
        x = x - part.astype(F32)
    return parts


def _ssd_kernel(xs_ref, b_ref, c_ref, dt_ref, z_ref, arow_ref, dskip_ref, norm_ref, tri_ref, expand_ref,
                y_ref, state_ref):
    chunk = SSD_CHUNK
    lanes = range(xs_ref.shape[0])
    heads_per_group = SSM_HEADS // SSM_GROUPS

    @pl.when(pl.program_id(1) == 0)
    def _():
        state_ref[...] = jnp.zeros(state_ref.shape, F32)

    row = lax.broadcasted_iota(jnp.int32, (chunk, chunk), 0)
    col = lax.broadcasted_iota(jnp.int32, (chunk, chunk), 1)
    causal = row >= col
    low = lax.broadcasted_iota(jnp.int32, (chunk, LANES), 1) < SSM_HEAD_DIM
    zero = jnp.zeros((), BF16)

    def group_cols(g):
        return slice(g * SSM_STATE, (g + 1) * SSM_STATE), slice(g * GROUP_WIDTH, (g + 1) * GROUP_WIDTH)

    dts = [dt_ref[i] for i in lanes]
    a3s = [jnp.dot(tri_ref[...], jnp.concatenate(_split_bf16(dt * arow_ref[...], 3), axis=1),
                   preferred_element_type=F32) for dt in dts]
    a_css = [a3[:, :LANES] + a3[:, LANES:2 * LANES] + a3[:, 2 * LANES:] for a3 in a3s]

    def expand(stat):
        return jnp.dot(jnp.concatenate(_split_bf16(stat, 2), axis=1), expand_ref[...], preferred_element_type=F32)

    w_states = [expand(jnp.exp(a_cs[chunk - 1:chunk, :] - a_cs) * dt) for a_cs, dt in zip(a_css, dts)]
    scale_offs = [expand(jnp.exp(a_cs)) for a_cs in a_css]
    a_cs_ts = [a_cs.T for a_cs in a_css]
    dt_ts = [dt.T for dt in dts]

    cbs, y_offs, new_states = {}, {}, {}
    for i in lanes:
        xw_bf = (xs_ref[i] * w_states[i]).astype(BF16)
        for g in range(SSM_GROUPS):
            gcols, wide = group_cols(g)
            b_g = b_ref[i, :, gcols]
            c_g = c_ref[i, :, gcols]
            cbs[i, g] = lax.dot_general(c_g, b_g, (((1,), (1,)), ((), ())), preferred_element_type=F32)
            y_offs[i, g] = jnp.dot(c_g, state_ref[i, :, wide].astype(BF16), preferred_element_type=F32)
            b_t = b_g.astype(F32).T.astype(BF16)
            new_states[i, g] = jnp.dot(b_t, xw_bf[:, wide], preferred_element_type=F32)

    y_diags = {}
    for i in lanes:
        xs_bf = xs_ref[i].astype(BF16)
        for g in range(SSM_GROUPS):
            parts = []
            for pair in range(heads_per_group // 2):
                mats = []
                for e in range(2):
                    h = g * heads_per_group + 2 * pair + e
                    seg = a_css[i][:, h:h + 1] - a_cs_ts[i][h:h + 1, :]
                    decay = jnp.exp(jnp.where(causal, seg, -jnp.inf))
                    mats.append((cbs[i, g] * decay * dt_ts[i][h:h + 1, :]).astype(BF16))
                first = (g * heads_per_group + 2 * pair) * SSM_HEAD_DIM
                x_pair = xs_bf[:, first:first + 2 * SSM_HEAD_DIM]
                rhs = jnp.concatenate([jnp.where(low, x_pair, zero), jnp.where(low, zero, x_pair)], axis=0)
                parts.append(jnp.dot(jnp.concatenate(mats, axis=1), rhs, preferred_element_type=F32))
            y_diags[i, g] = jnp.concatenate(parts, axis=1)

    for i in lanes:
        chunk_decay = scale_offs[i][chunk - 1:chunk, :]
        y_parts = []
        for g in range(SSM_GROUPS):
            _, wide = group_cols(g)
            y_parts.append(y_diags[i, g] + y_offs[i, g] * scale_offs[i][:, wide])
            state_ref[i, :, wide] = state_ref[i, :, wide] * chunk_decay[:, wide] + new_states[i, g]
        y = jnp.concatenate(y_parts, axis=1) + dskip_ref[...] * xs_ref[i]
        y = y * jax.nn.silu(z_ref[i])
        normed = []
        for g in range(SSM_GROUPS):
            y_g = y[:, g * GROUP_WIDTH:(g + 1) * GROUP_WIDTH]
            normed.append(y_g * lax.rsqrt(jnp.mean(y_g * y_g, axis=-1, keepdims=True) + NORM_EPS))
        y_ref[i] = jnp.concatenate(normed, axis=1) * norm_ref[...]


def _ssd(xs, bm, cm, dt, z, arow, dskip, norm, tri, expand):
    batch, seq, _ = xs.shape

    def tile(width):
        return pl.BlockSpec((SSD_BATCH, SSD_CHUNK, width), lambda b, c: (b, c, 0))

    return pl.pallas_call(
        _ssd_kernel,
        grid=(batch // SSD_BATCH, seq // SSD_CHUNK),
        in_specs=[tile(SSM_WIDTH), tile(BC_WIDTH), tile(BC_WIDTH), tile(LANES), tile(SSM_WIDTH),
                  _resident((1, LANES)), _resident((1, SSM_WIDTH)), _resident((1, SSM_WIDTH)),
                  _resident((SSD_CHUNK, SSD_CHUNK)), _resident((2 * LANES, SSM_WIDTH))],
        out_specs=tile(SSM_WIDTH),
        out_shape=jax.ShapeDtypeStruct((batch, seq, SSM_WIDTH), F32),
        scratch_shapes=[pltpu.VMEM((SSD_BATCH, SSM_STATE, SSM_WIDTH), F32)],
        compiler_params=pltpu.CompilerParams(dimension_semantics=("parallel", "arbitrary"),
                                             vmem_limit_bytes=32 * MIB),
        name="ssd",
    )(xs, bm, cm, dt, z, arow, dskip, norm, tri, expand)


def _out_ffn_kernel(x_ref, attn_ref, y_ref, wo_attn_ref, wo_ssm_ref, mixpost_ref, pre_ref, wg_ref, wu_ref, wd_ref,
                    post_ref, o_ref):
    attn = jnp.concatenate([attn_ref[0, j] for j in range(Q_SLABS)], axis=1).astype(BF16)
    mixed = (jnp.dot(attn, wo_attn_ref[...], preferred_element_type=F32)
             + jnp.dot(y_ref[0].astype(BF16), wo_ssm_ref[...], preferred_element_type=F32))
    x = x_ref[0] + _rmsnorm(mixed, mixpost_ref[...])
    o_ref[0] = _swiglu_half_step(x, pre_ref[...], wg_ref, wu_ref, wd_ref, post_ref[...])


def _out_ffn(x3d, attn, y3d, wo_attn, wo_ssm, mixpost, pre, wg, wu, wd, post):
    batch, seq, _ = x3d.shape
    rows = FFN_ROWS
    row_spec = pl.BlockSpec((1, rows, D_MODEL), lambda b, j: (b, j, 0))
    slab_spec = pl.BlockSpec((1, Q_SLABS, rows, LANES), lambda b, j: (b, 0, j, 0))
    return pl.pallas_call(
        _out_ffn_kernel,
        grid=(batch, seq // rows),
        in_specs=[row_spec, slab_spec, row_spec, _resident((ATTN_WIDTH, D_MODEL)), _resident((SSM_WIDTH, D_MODEL)),
                  _resident((1, D_MODEL)), _resident((1, D_MODEL)), _resident((D_MODEL, D_FF)),
                  _resident((D_MODEL, D_FF)), _resident((D_FF, D_MODEL)), _resident((1, D_MODEL))],
        out_specs=row_spec,
        out_shape=jax.ShapeDtypeStruct((batch, seq, D_MODEL), F32),
        compiler_params=pltpu.CompilerParams(dimension_semantics=("parallel", "parallel"),
                                             vmem_limit_bytes=56 * MIB),
        name="out_ffn",
    )(x3d, attn, y3d, wo_attn, wo_ssm, mixpost, pre, wg, wu, wd, post)


def _to_slab_head_order(w, axis):
    shape = w.shape
    split = shape[:axis] + (KV_PAIRS, 2, Q_PER_KV, HEAD_DIM) + shape[axis + 1:]
    return jnp.swapaxes(w.reshape(split), axis + 1, axis + 2).reshape(shape)


def _pad_lanes(v):
    return jnp.pad(v, [(0, 0)] * (v.ndim - 1) + [(0, LANES - v.shape[-1])])


def _layer(x, pos3d, invf, tri, expand, p):
    batch, seq, _ = x.shape
    tokens = batch * seq
    row = lambda v: v.reshape(1, -1)
    bf = lambda w: w.astype(BF16)

    x1 = _ffn(x.reshape(tokens, D_MODEL), row(p["ffn1_pre_norm"]), bf(p["ffn1_w_gate"]), bf(p["ffn1_w_up"]),
              bf(p["ffn1_w_down"]), row(p["ffn1_post_norm"])).reshape(batch, seq, D_MODEL)

    w_in = p["w_in"]
    o_k = ATTN_WIDTH
    o_v = o_k + KV_WIDTH
    o_x = o_v + KV_WIDTH
    o_z = o_x + CONV_CHANNELS
    o_dt = o_z + SSM_WIDTH
    q, k, v, xs, bm, cm, z, dt = _in_proj(
        x1, pos3d, invf, row(p["mix_pre_norm"]),
        bf(_to_slab_head_order(w_in[:, :o_k], 1)), bf(w_in[:, o_k:o_v]), bf(w_in[:, o_v:o_x]),
        bf(w_in[:, o_x:o_z]), bf(w_in[:, o_z:o_dt]), bf(_pad_lanes(w_in[:, o_dt:])),
        p["conv_w"], row(p["conv_b"]), _pad_lanes(row(p["dt_bias"])))

    attn = _attention(q, k, v)

    arow = _pad_lanes(row(-jnp.exp(p["a_log"])))
    dskip = row(jnp.repeat(p["d_skip"], SSM_HEAD_DIM))
    y = _ssd(xs, bm, cm, dt, z, arow, dskip, row(p["ssm_norm"]), tri, expand)

    w_out = p["w_out"]
    return _out_ffn(x1, attn, y, bf(_to_slab_head_order(w_out[:ATTN_WIDTH], 0)), bf(w_out[ATTN_WIDTH:]),
                    row(p["mix_post_norm"]), row(p["ffn2_pre_norm"]),
                    bf(p["ffn2_w_gate"]), bf(p["ffn2_w_up"]), bf(p["ffn2_w_down"]), row(p["ffn2_post_norm"]))


def kernel(x, positions, ffn1_pre_norm, ffn1_w_gate, ffn1_w_up, ffn1_w_down, ffn1_post_norm, mix_pre_norm, w_in, conv_w, conv_b, dt_bias, a_log, d_skip, ssm_norm, w_out, mix_post_norm, ffn2_pre_norm, ffn2_w_gate, ffn2_w_up, ffn2_w_down, ffn2_post_norm):
    params = dict(ffn1_pre_norm=ffn1_pre_norm, ffn1_w_gate=ffn1_w_gate, ffn1_w_up=ffn1_w_up,
                  ffn1_w_down=ffn1_w_down, ffn1_post_norm=ffn1_post_norm, mix_pre_norm=mix_pre_norm, w_in=w_in,
                  conv_w=conv_w, conv_b=conv_b, dt_bias=dt_bias, a_log=a_log, d_skip=d_skip, ssm_norm=ssm_norm,
                  w_out=w_out, mix_post_norm=mix_post_norm, ffn2_pre_norm=ffn2_pre_norm, ffn2_w_gate=ffn2_w_gate,
                  ffn2_w_up=ffn2_w_up, ffn2_w_down=ffn2_w_down, ffn2_post_norm=ffn2_post_norm)
    depth = w_in.shape[0]
    batch, seq, _ = x.shape
    inv_freq = ROPE_THETA ** (-jnp.arange(0, HEAD_DIM, 2, dtype=F32) / HEAD_DIM)
    invf = inv_freq.reshape(HEAD_DIM // 2, 1)
    pos3d = positions.reshape(batch, 1, seq)
    idx = jnp.arange(SSD_CHUNK)
    tri = (idx[:, None] >= idx[None, :]).astype(BF16)
    expand = (jnp.arange(2 * LANES)[:, None] % LANES == jnp.arange(SSM_WIDTH)[None, :] // SSM_HEAD_DIM).astype(BF16)
    for i in range(depth):
        x = _layer(x, pos3d, invf, tri, expand, {name: w[i] for name, w in params.items()})
    return x
```

```python
import functools
import math

import jax
import jax.numpy as jnp
from jax import lax
from jax.experimental import pallas as pl
from jax.experimental.pallas import tpu as pltpu

F32 = jnp.float32
BF16 = jnp.bfloat16

D_MODEL = 1024
D_FF = 2816
HEAD_DIM = 64
N_Q_HEADS = 16
N_KV_HEADS = 4
Q_PER_KV = N_Q_HEADS // N_KV_HEADS
ATTN_WIDTH = N_Q_HEADS * HEAD_DIM
KV_WIDTH = N_KV_HEADS * HEAD_DIM
ATTN_BLOCK = 128
ROPE_THETA = 10000.0
SSM_HEADS = 16
SSM_HEAD_DIM = 64
SSM_WIDTH = SSM_HEADS * SSM_HEAD_DIM
SSM_STATE = 128
SSM_GROUPS = 2
GROUP_WIDTH = SSM_WIDTH // SSM_GROUPS
BC_WIDTH = SSM_GROUPS * SSM_STATE
CONV_WIDTH = 4
CONV_CHANNELS = SSM_WIDTH + 2 * BC_WIDTH
SSD_CHUNK = 128
MACARON_WEIGHT = 0.5
NORM_EPS = 1e-6
LOG2_E = math.log2(math.e)

LANES = 128
CARRY_ROWS = 8
MIB = 1024 * 1024

FFN_ROWS = 512
FFN_TILES_PER_STEP = 2
OUT_FFN_TILES_PER_STEP = 2
FF_CHUNK = 256
PROJ_ROWS = 512
SSD_BATCH = 4

Q_SLABS = ATTN_WIDTH // LANES
KV_SLABS = KV_WIDTH // LANES
KV_PAIRS = N_KV_HEADS // 2


def _rmsnorm(x, gain):
    return x * lax.rsqrt(jnp.mean(x * x, axis=-1, keepdims=True) + NORM_EPS) * gain


def _resident(shape):
    return pl.BlockSpec(shape, lambda *_: (0,) * len(shape), pipeline_mode=pl.Buffered(1))


def _swiglu_half_steps(load_tiles, pre, wg_ref, wu_ref, wd_ref, post):
    results = []
    finish = None
    x = load_tiles[0]()
    xn = _rmsnorm(x, pre).astype(BF16)
    for t in range(len(load_tiles)):
        acc = jnp.zeros(x.shape, F32)
        x_next = xn_next = None
        for c in range(D_FF // FF_CHUNK):
            cols = slice(c * FF_CHUNK, (c + 1) * FF_CHUNK)
            gate = jnp.dot(xn, wg_ref[:, cols], preferred_element_type=F32)
            up = jnp.dot(xn, wu_ref[:, cols], preferred_element_type=F32)
            hidden = (jax.nn.silu(gate) * up).astype(BF16)
            acc = acc + jnp.dot(hidden, wd_ref[cols, :], preferred_element_type=F32)
            if c == 1 and finish is not None:
                results.append(finish())
            if c == 2 and t + 1 < len(load_tiles):
                x_next = load_tiles[t + 1]()
                xn_next = _rmsnorm(x_next, pre).astype(BF16)
        finish = functools.partial(lambda x_t, acc_t: x_t + MACARON_WEIGHT * _rmsnorm(acc_t, post), x, acc)
        x, xn = x_next, xn_next
    results.append(finish())
    return results


def _ffn_kernel(x_ref, pre_ref, wg_ref, wu_ref, wd_ref, post_ref, o_ref):
    tiles = [functools.partial(lambda t: x_ref[t * FFN_ROWS:(t + 1) * FFN_ROWS, :], t)
             for t in range(FFN_TILES_PER_STEP)]
    outs = _swiglu_half_steps(tiles, pre_ref[...], wg_ref, wu_ref, wd_ref, post_ref[...])
    for t, out in enumerate(outs):
        o_ref[t * FFN_ROWS:(t + 1) * FFN_ROWS, :] = out


def _ffn(x2d, pre, wg, wu, wd, post):
    tokens = x2d.shape[0]
    step_rows = FFN_ROWS * FFN_TILES_PER_STEP
    row_spec = pl.BlockSpec((step_rows, D_MODEL), lambda i: (i, 0))
    return pl.pallas_call(
        _ffn_kernel,
        grid=(tokens // step_rows,),
        in_specs=[row_spec, _resident((1, D_MODEL)), _resident((D_MODEL, D_FF)), _resident((D_MODEL, D_FF)),
                  _resident((D_FF, D_MODEL)), _resident((1, D_MODEL))],
        out_specs=row_spec,
        out_shape=jax.ShapeDtypeStruct((tokens, D_MODEL), F32),
        compiler_params=pltpu.CompilerParams(dimension_semantics=("parallel",), vmem_limit_bytes=48 * MIB),
        name="ffn",
    )(x2d, pre, wg, wu, wd, post)


def _softplus(x):
    return jnp.maximum(x, 0.0) + jnp.log1p(jnp.exp(-jnp.abs(x)))


def _inproj_kernel(x_ref, pos_ref, invf_ref, gain_ref, wq_ref, wk_ref, wv_ref, wx_ref, wz_ref, wdt_ref,
                   convw_ref, convb_ref, dtb_ref,
                   q_ref, k_ref, v_ref, xs_ref, b_ref, c_ref, z_ref, dt_ref, ubuf_ref):
    rows = x_ref.shape[1]
    xn = _rmsnorm(x_ref[0], gain_ref[...]).astype(BF16)
    wide = 2 * LANES

    def project(w_ref, c):
        return jnp.dot(xn, w_ref[:, c * wide:(c + 1) * wide], preferred_element_type=F32)

    n_slabs = CONV_CHANNELS // LANES

    @pl.when(pl.program_id(1) == 0)
    def _():
        ubuf_ref[:, 0:CARRY_ROWS, :] = jnp.zeros((n_slabs, CARRY_ROWS, LANES), F32)

    def conv_chunk(c, u):
        for j in (2 * c, 2 * c + 1):
            cols = slice(j * LANES, (j + 1) * LANES)
            ubuf_ref[j, CARRY_ROWS:CARRY_ROWS + rows, :] = u[:, (j % 2) * LANES:(j % 2 + 1) * LANES]
            conv = (convb_ref[:, cols]
                    + ubuf_ref[j, CARRY_ROWS:CARRY_ROWS + rows, :] * convw_ref[CONV_WIDTH - 1:CONV_WIDTH, cols])
            for tap in range(CONV_WIDTH - 1):
                start = CARRY_ROWS - (CONV_WIDTH - 1) + tap
                conv = conv + ubuf_ref[j, start:start + rows, :] * convw_ref[tap:tap + 1, cols]
            act = jax.nn.silu(conv)
            if j < SSM_WIDTH // LANES:
                xs_ref[0, :, cols] = act
            elif j < (SSM_WIDTH + BC_WIDTH) // LANES:
                b_ref[0, :, j * LANES - SSM_WIDTH:(j + 1) * LANES - SSM_WIDTH] = act.astype(BF16)
            else:
                first = SSM_WIDTH + BC_WIDTH
                c_ref[0, :, j * LANES - first:(j + 1) * LANES - first] = act.astype(BF16)
            ubuf_ref[j, 0:CARRY_ROWS, :] = ubuf_ref[j, rows:rows + CARRY_ROWS, :]

    def z_chunk(c):
        z_ref[0, :, c * wide:(c + 1) * wide] = project(wz_ref, c)

    u = project(wx_ref, 0)
    z_chunk(0)

    ang_t = invf_ref[...] * pos_ref[0].astype(F32)
    cos_t = jnp.cos(ang_t)
    sin_t = jnp.sin(ang_t)
    copies = LANES // HEAD_DIM
    cos = jnp.concatenate([cos_t, cos_t] * copies, axis=0).T
    sin = jnp.concatenate([-sin_t, sin_t] * copies, axis=0).T
    lane = lax.broadcasted_iota(jnp.int32, (rows, LANES), 1)
    first_half = (lane % HEAD_DIM) < (HEAD_DIM // 2)

    def rope(t):
        back = pltpu.roll(t, HEAD_DIM // 2, 1)
        fwd = pltpu.roll(t, LANES - HEAD_DIM // 2, 1)
        return t * cos + jnp.where(first_half, fwd, back) * sin

    def q_chunk(c):
        q = project(wq_ref, c)
        for half in range(2):
            q_ref[0, 2 * c + half] = rope(q[:, half * LANES:(half + 1) * LANES]) * (HEAD_DIM ** -0.5 * LOG2_E)

    def kv_chunk(c):
        k = project(wk_ref, c)
        v = project(wv_ref, c)
        for half in range(2):
            k_ref[0, 2 * c + half] = rope(k[:, half * LANES:(half + 1) * LANES])
            v_ref[0, 2 * c + half] = v[:, half * LANES:(half + 1) * LANES]

    dt_ref[0] = _softplus(jnp.dot(xn, wdt_ref[...], preferred_element_type=F32) + dtb_ref[...])
    n_conv = CONV_CHANNELS // wide
    fillers = ([functools.partial(q_chunk, c) for c in range(ATTN_WIDTH // wide)]
               + [functools.partial(kv_chunk, c) for c in range(KV_WIDTH // wide)]
               + [functools.partial(z_chunk, c) for c in range(1, SSM_WIDTH // wide)])
    for c in range(n_conv):
        u_next = project(wx_ref, c + 1) if c + 1 < n_conv else None
        conv_chunk(c, u)
        fillers.pop(0)()
        u = u_next
    for filler in fillers:
        filler()


def _in_proj(x3d, pos3d, invf, gain, wq, wk, wv, wx, wz, wdt, convw, convb, dtb):
    batch, seq, _ = x3d.shape
    rows = PROJ_ROWS

    def tile(width):
        return pl.BlockSpec((1, rows, width), lambda b, j: (b, j, 0))

    def slabs(n):
        return pl.BlockSpec((1, n, rows, LANES), lambda b, j: (b, 0, j, 0))

    def out(width, dtype):
        return jax.ShapeDtypeStruct((batch, seq, width), dtype)

    def out_slabs(n):
        return jax.ShapeDtypeStruct((batch, n, seq, LANES), F32)

    return pl.pallas_call(
        _inproj_kernel,
        grid=(batch, seq // rows),
        in_specs=[tile(D_MODEL), pl.BlockSpec((1, 1, rows), lambda b, j: (b, 0, j)),
                  _resident((HEAD_DIM // 2, 1)), _resident((1, D_MODEL)),
                  _resident((D_MODEL, ATTN_WIDTH)), _resident((D_MODEL, KV_WIDTH)), _resident((D_MODEL, KV_WIDTH)),
                  _resident((D_MODEL, CONV_CHANNELS)), _resident((D_MODEL, SSM_WIDTH)),
                  _resident((D_MODEL, LANES)), _resident((CONV_WIDTH, CONV_CHANNELS)),
                  _resident((1, CONV_CHANNELS)), _resident((1, LANES))],
        out_specs=[slabs(Q_SLABS), slabs(KV_SLABS), slabs(KV_SLABS), tile(SSM_WIDTH), tile(BC_WIDTH),
                   tile(BC_WIDTH), tile(SSM_WIDTH), tile(LANES)],
        out_shape=[out_slabs(Q_SLABS), out_slabs(KV_SLABS), out_slabs(KV_SLABS), out(SSM_WIDTH, F32),
                   out(BC_WIDTH, BF16), out(BC_WIDTH, BF16), out(SSM_WIDTH, F32), out(LANES, F32)],
        scratch_shapes=[pltpu.VMEM((CONV_CHANNELS // LANES, rows + CARRY_ROWS, LANES), F32)],
        compiler_params=pltpu.CompilerParams(dimension_semantics=("parallel", "arbitrary"),
                                             vmem_limit_bytes=56 * MIB),
        name="in_proj",
    )(x3d, pos3d, invf, gain, wq, wk, wv, wx, wz, wdt, convw, convb, dtb)


STATE_ARRAYS = 3
MASKED = -2.0 ** 100
STATE_U, STATE_M, STATE_L = range(STATE_ARRAYS)


def _attn_blocks(blocks):
    blk = ATTN_BLOCK
    low = lax.broadcasted_iota(jnp.int32, (blk, LANES), 1) < HEAD_DIM
    zero = jnp.zeros((), BF16)
    one = jnp.ones((), BF16)

    scores = []
    for q_slabs, k_blk, _, (query_onehot, key_mask), _ in blocks:
        qb = [q.astype(BF16) for q in q_slabs]
        keys = jnp.concatenate([k_blk.astype(BF16), key_mask], axis=1)
        for masked in ([jnp.where(low, q, zero) for q in qb], [jnp.where(low, zero, q) for q in qb]):
            lhs = jnp.concatenate([jnp.concatenate(masked, axis=0), query_onehot], axis=1)
            scores.append(lax.dot_general(lhs, keys, (((1,), (1,)), ((), ())), preferred_element_type=F32))
    maxes = [jnp.max(s, axis=-1, keepdims=True) for s in scores]
    probs = [jnp.exp2((s - m).astype(BF16)) for s, m in zip(scores, maxes)]
    results = []
    for i, (_, _, v_blk, _, _) in enumerate(blocks):
        vb = v_blk.astype(BF16)
        key_low = lax.broadcasted_iota(jnp.int32, vb.shape, 1) < HEAD_DIM
        results.append(jnp.dot(probs[2 * i], jnp.where(key_low, vb, one), preferred_element_type=F32))
        results.append(jnp.dot(probs[2 * i + 1], jnp.where(key_low, one, vb), preferred_element_type=F32))
    states = []
    for i, (q_slabs, _, _, _, load_old) in enumerate(blocks):
        r_low, r_high, m_a, m_b = results[2 * i], results[2 * i + 1], maxes[2 * i], maxes[2 * i + 1]
        old = None if load_old is None else load_old()
        state = []
        for g in range(len(q_slabs)):
            rows = slice(g * blk, (g + 1) * blk)
            u_new = jnp.where(low, r_low[rows], r_high[rows])
            l_new = pltpu.roll(jnp.where(low, r_high[rows], r_low[rows]), HEAD_DIM, 1)
            m_new = jnp.where(low, m_a[rows], m_b[rows])
            if old is not None:
                u_run, m_run, l_run = old[g]
                m_tot = jnp.maximum(m_run, m_new)
                a = jnp.exp2(m_run - m_tot)
                b = jnp.exp2(m_new - m_tot)
                u_new = u_run * a + u_new * b
                l_new = l_run * a + l_new * b
                m_new = m_tot
            state.append((u_new, m_new, l_new))
        states.append(state)
    return states


def _attn_kernel(q_ref, k_ref, v_ref, o_ref, st4_ref, stt_ref):
    blk = ATTN_BLOCK
    seq = q_ref.shape[2]
    n_slabs = q_ref.shape[1]
    sub4 = seq // 4

    def key_mask(n_keys, shift):
        si = lax.broadcasted_iota(jnp.int32, (n_keys, blk), 0)
        qi = lax.broadcasted_iota(jnp.int32, (n_keys, blk), 1) + shift
        return jnp.where((si <= qi) & (si >= qi - blk), 0.0, MASKED).astype(BF16)

    stacked = n_slabs * blk
    query_onehot = (lax.broadcasted_iota(jnp.int32, (stacked, blk), 0) % blk
                    == lax.broadcasted_iota(jnp.int32, (stacked, blk), 1)).astype(BF16)
    band_bias = (query_onehot, key_mask(2 * blk, blk))
    causal_bias = (query_onehot, key_mask(blk, 0))

    def load_q(rows):
        return [q_ref[0, g, rows, :] for g in range(n_slabs)]

    def load_state(ref, rows):
        return [tuple(ref[a * n_slabs + g, rows, :] for a in range(STATE_ARRAYS)) for g in range(n_slabs)]

    def store_state(ref, rows, state):
        for g in range(n_slabs):
            for a in range(STATE_ARRAYS):
                ref[a * n_slabs + g, rows, :] = state[g][a]

    def store_out(rows, state):
        for g in range(n_slabs):
            o_ref[0, g, rows, :] = (state[g][STATE_U] / state[g][STATE_L]).astype(o_ref.dtype)

    def run(plans):
        states = _attn_blocks([(load_q(rows), k_ref[0, 0, keys, :], v_ref[0, 0, keys, :], bias, load_old)
                               for rows, keys, bias, load_old, _ in plans])
        for (rows, _, _, _, store), state in zip(plans, states):
            store(rows, state)

    def paired(n_blocks, plan, per_trip=2):
        def trip(i, carry):
            run([plan(per_trip * i + j) for j in range(per_trip)])
            return carry
        lax.fori_loop(0, n_blocks // per_trip, trip, 0)

    def d16_plan(r16):
        rows = pl.ds(r16, blk, stride=16)
        dest = pl.ds((r16 % 4) * sub4 + r16 // 4, blk, stride=4)
        return rows, rows, causal_bias, None, lambda _, state: store_state(st4_ref, dest, state)

    paired(16, d16_plan, per_trip=4)

    def to_token_order(rows, state):
        store_state(stt_ref, rows, state)

    def d4_first_plan(r4):
        rows = pl.ds(r4, blk, stride=4)
        src = pl.ds(pl.multiple_of(r4 * sub4, blk), blk)
        return rows, rows, causal_bias, lambda: load_state(st4_ref, src), to_token_order

    paired(4, d4_first_plan, per_trip=4)
    later_blocks = sub4 // blk - 1

    def d4_plan(idx):
        r4 = idx // later_blocks
        n = 1 + idx % later_blocks
        rows = pl.ds(r4 + 4 * blk * n, blk, stride=4)
        keys = pl.ds(r4 + 4 * blk * (n - 1), 2 * blk, stride=4)
        src = pl.ds(pl.multiple_of(r4 * sub4 + n * blk, blk), blk)
        return rows, keys, band_bias, lambda: load_state(st4_ref, src), to_token_order

    paired(4 * later_blocks, d4_plan, per_trip=3)

    def d1_plan(n):
        start = n * blk if isinstance(n, int) else pl.multiple_of(n * blk, blk)
        rows = pl.ds(start, blk)
        if isinstance(n, int) and n == 0:
            return rows, rows, causal_bias, lambda: load_state(stt_ref, rows), store_out
        return rows, pl.ds(start - blk, 2 * blk), band_bias, lambda: load_state(stt_ref, rows), store_out

    last = seq // blk - 1
    run([d1_plan(0)])
    paired(last, lambda i: d1_plan(i + 1), per_trip=3)


def _attention(q, k, v):
    batch, _, seq, _ = q.shape
    slabs_per_pair = Q_SLABS // KV_PAIRS
    q_spec = pl.BlockSpec((1, slabs_per_pair, seq, LANES), lambda b, p: (b, p, 0, 0))
    kv_spec = pl.BlockSpec((1, 1, seq, LANES), lambda b, p: (b, p, 0, 0))
    state_shape = (STATE_ARRAYS * slabs_per_pair, seq, LANES)
    return pl.pallas_call(
        _attn_kernel,
        grid=(batch, KV_PAIRS),
        in_specs=[q_spec, kv_spec, kv_spec],
        out_specs=q_spec,
        out_shape=jax.ShapeDtypeStruct(q.shape, BF16),
        scratch_shapes=[pltpu.VMEM(state_shape, F32), pltpu.VMEM(state_shape, F32)],
        compiler_params=pltpu.CompilerParams(dimension_semantics=("parallel", "parallel"),
                                             vmem_limit_bytes=56 * MIB),
        name="attention",
    )(q, k, v)


def _split_bf16(x, pieces):
    parts = []
    for _ in range(pieces):
        part = x.astype(BF16)
        parts.append(part)
        x = x - part.astype(F32)
    return parts


def _ssd_kernel(xs_ref, b_ref, c_ref, dt_ref, z_ref, arow_ref, dskip_ref, norm_ref, tri_ref, expand_ref,
                y_ref, state_ref):
    chunk = SSD_CHUNK
    lanes = range(xs_ref.shape[0])
    heads_per_group = SSM_HEADS // SSM_GROUPS

    @pl.when(pl.program_id(1) == 0)
    def _():
        state_ref[...] = jnp.zeros(state_ref.shape, F32)

    row = lax.broadcasted_iota(jnp.int32, (chunk, chunk), 0)
    col = lax.broadcasted_iota(jnp.int32, (chunk, chunk), 1)
    causal = row >= col
    low = lax.broadcasted_iota(jnp.int32, (chunk, LANES), 1) < SSM_HEAD_DIM
    zero = jnp.zeros((), BF16)

    def group_cols(g):
        return slice(g * SSM_STATE, (g + 1) * SSM_STATE), slice(g * GROUP_WIDTH, (g + 1) * GROUP_WIDTH)

    dts = [dt_ref[i] for i in lanes]
    a3s = [jnp.dot(tri_ref[...], jnp.concatenate(_split_bf16(dt * arow_ref[...], 3), axis=1),
                   preferred_element_type=F32) for dt in dts]
    a_css = [a3[:, :LANES] + a3[:, LANES:2 * LANES] + a3[:, 2 * LANES:] for a3 in a3s]

    def expand(stat):
        return jnp.dot(jnp.concatenate(_split_bf16(stat, 2), axis=1), expand_ref[...], preferred_element_type=F32)

    w_states = [expand(jnp.exp(a_cs[chunk - 1:chunk, :] - a_cs) * dt) for a_cs, dt in zip(a_css, dts)]
    scale_offs = [expand(jnp.exp(a_cs)) for a_cs in a_css]
    a_cs_ts = [a_cs.T for a_cs in a_css]
    dt_ts = [dt.T for dt in dts]

    cbs, y_offs, new_states = {}, {}, {}
    for i in lanes:
        xw_bf = (xs_ref[i] * w_states[i]).astype(BF16)
        for g in range(SSM_GROUPS):
            gcols, wide = group_cols(g)
            b_g = b_ref[i, :, gcols]
            c_g = c_ref[i, :, gcols]
            cbs[i, g] = lax.dot_general(c_g, b_g, (((1,), (1,)), ((), ())), preferred_element_type=F32)
            y_offs[i, g] = jnp.dot(c_g, state_ref[i, :, wide].astype(BF16), preferred_element_type=F32)
            b_t = b_g.astype(F32).T.astype(BF16)
            new_states[i, g] = jnp.dot(b_t, xw_bf[:, wide], preferred_element_type=F32)

    y_diags = {}
    for i in lanes:
        xs_bf = xs_ref[i].astype(BF16)
        for g in range(SSM_GROUPS):
            parts = []
            for pair in range(heads_per_group // 2):
                mats = []
                for e in range(2):
                    h = g * heads_per_group + 2 * pair + e
                    seg = a_css[i][:, h:h + 1] - a_cs_ts[i][h:h + 1, :]
                    decay = jnp.exp(jnp.where(causal, seg, -jnp.inf))
                    mats.append((cbs[i, g] * decay * dt_ts[i][h:h + 1, :]).astype(BF16))
                first = (g * heads_per_group + 2 * pair) * SSM_HEAD_DIM
                x_pair = xs_bf[:, first:first + 2 * SSM_HEAD_DIM]
                rhs = jnp.concatenate([jnp.where(low, x_pair, zero), jnp.where(low, zero, x_pair)], axis=0)
                parts.append(jnp.dot(jnp.concatenate(mats, axis=1), rhs, preferred_element_type=F32))
            y_diags[i, g] = jnp.concatenate(parts, axis=1)

    for i in lanes:
        chunk_decay = scale_offs[i][chunk - 1:chunk, :]
        y_parts = []
        for g in range(SSM_GROUPS):
            _, wide = group_cols(g)
            y_parts.append(y_diags[i, g] + y_offs[i, g] * scale_offs[i][:, wide])
            state_ref[i, :, wide] = state_ref[i, :, wide] * chunk_decay[:, wide] + new_states[i, g]
        y = jnp.concatenate(y_parts, axis=1) + dskip_ref[...] * xs_ref[i]
        y = y * jax.nn.silu(z_ref[i])
        normed = []
        for g in range(SSM_GROUPS):
            y_g = y[:, g * GROUP_WIDTH:(g + 1) * GROUP_WIDTH]
            normed.append(y_g * lax.rsqrt(jnp.mean(y_g * y_g, axis=-1, keepdims=True) + NORM_EPS))
        y_ref[i] = (jnp.concatenate(normed, axis=1) * norm_ref[...]).astype(y_ref.dtype)


def _ssd(xs, bm, cm, dt, z, arow, dskip, norm, tri, expand):
    batch, seq, _ = xs.shape

    def tile(width):
        return pl.BlockSpec((SSD_BATCH, SSD_CHUNK, width), lambda b, c: (b, c, 0))

    return pl.pallas_call(
        _ssd_kernel,
        grid=(batch // SSD_BATCH, seq // SSD_CHUNK),
        in_specs=[tile(SSM_WIDTH), tile(BC_WIDTH), tile(BC_WIDTH), tile(LANES), tile(SSM_WIDTH),
                  _resident((1, LANES)), _resident((1, SSM_WIDTH)), _resident((1, SSM_WIDTH)),
                  _resident((SSD_CHUNK, SSD_CHUNK)), _resident((2 * LANES, SSM_WIDTH))],
        out_specs=tile(SSM_WIDTH),
        out_shape=jax.ShapeDtypeStruct((batch, seq, SSM_WIDTH), BF16),
        scratch_shapes=[pltpu.VMEM((SSD_BATCH, SSM_STATE, SSM_WIDTH), F32)],
        compiler_params=pltpu.CompilerParams(dimension_semantics=("parallel", "arbitrary"),
                                             vmem_limit_bytes=32 * MIB),
        name="ssd",
    )(xs, bm, cm, dt, z, arow, dskip, norm, tri, expand)


def _out_ffn_kernel(x_ref, attn_ref, y_ref, wo_attn_ref, wo_ssm_ref, mixpost_ref, pre_ref, wg_ref, wu_ref, wd_ref,
                    post_ref, o_ref):
    def mixer_residual(t):
        rows = slice(t * FFN_ROWS, (t + 1) * FFN_ROWS)
        attn = jnp.concatenate([attn_ref[0, j, rows, :] for j in range(Q_SLABS)], axis=1)
        mixed = (jnp.dot(attn, wo_attn_ref[...], preferred_element_type=F32)
                 + jnp.dot(y_ref[0, rows, :], wo_ssm_ref[...], preferred_element_type=F32))
        return x_ref[0, rows, :] + _rmsnorm(mixed, mixpost_ref[...])

    tiles = [functools.partial(mixer_residual, t) for t in range(OUT_FFN_TILES_PER_STEP)]
    outs = _swiglu_half_steps(tiles, pre_ref[...], wg_ref, wu_ref, wd_ref, post_ref[...])
    for t, out in enumerate(outs):
        o_ref[0, t * FFN_ROWS:(t + 1) * FFN_ROWS, :] = out


def _out_ffn(x3d, attn, y3d, wo_attn, wo_ssm, mixpost, pre, wg, wu, wd, post):
    batch, seq, _ = x3d.shape
    rows = FFN_ROWS * OUT_FFN_TILES_PER_STEP
    row_spec = pl.BlockSpec((1, rows, D_MODEL), lambda b, j: (b, j, 0))
    slab_spec = pl.BlockSpec((1, Q_SLABS, rows, LANES), lambda b, j: (b, 0, j, 0))
    return pl.pallas_call(
        _out_ffn_kernel,
        grid=(batch, seq // rows),
        in_specs=[row_spec, slab_spec, row_spec, _resident((ATTN_WIDTH, D_MODEL)), _resident((SSM_WIDTH, D_MODEL)),
                  _resident((1, D_MODEL)), _resident((1, D_MODEL)), _resident((D_MODEL, D_FF)),
                  _resident((D_MODEL, D_FF)), _resident((D_FF, D_MODEL)), _resident((1, D_MODEL))],
        out_specs=row_spec,
        out_shape=jax.ShapeDtypeStruct((batch, seq, D_MODEL), F32),
        compiler_params=pltpu.CompilerParams(dimension_semantics=("parallel", "parallel"),
                                             vmem_limit_bytes=58 * MIB),
        name="out_ffn",
    )(x3d, attn, y3d, wo_attn, wo_ssm, mixpost, pre, wg, wu, wd, post)


def _to_slab_head_order(w, axis):
    shape = w.shape
    split = shape[:axis] + (KV_PAIRS, 2, Q_PER_KV, HEAD_DIM) + shape[axis + 1:]
    return jnp.swapaxes(w.reshape(split), axis + 1, axis + 2).reshape(shape)


def _pad_lanes(v):
    return jnp.pad(v, [(0, 0)] * (v.ndim - 1) + [(0, LANES - v.shape[-1])])


def _layer(x, pos3d, invf, tri, expand, p):
    batch, seq, _ = x.shape
    tokens = batch * seq
    row = lambda v: v.reshape(1, -1)
    bf = lambda w: w.astype(BF16)

    x1 = _ffn(x.reshape(tokens, D_MODEL), row(p["ffn1_pre_norm"]), bf(p["ffn1_w_gate"]), bf(p["ffn1_w_up"]),
              bf(p["ffn1_w_down"]), row(p["ffn1_post_norm"])).reshape(batch, seq, D_MODEL)

    w_in = p["w_in"]
    o_k = ATTN_WIDTH
    o_v = o_k + KV_WIDTH
    o_x = o_v + KV_WIDTH
    o_z = o_x + CONV_CHANNELS
    o_dt = o_z + SSM_WIDTH
    q, k, v, xs, bm, cm, z, dt = _in_proj(
        x1, pos3d, invf, row(p["mix_pre_norm"]),
        bf(_to_slab_head_order(w_in[:, :o_k], 1)), bf(w_in[:, o_k:o_v]), bf(w_in[:, o_v:o_x]),
        bf(w_in[:, o_x:o_z]), bf(w_in[:, o_z:o_dt]), bf(_pad_lanes(w_in[:, o_dt:])),
        p["conv_w"], row(p["conv_b"]), _pad_lanes(row(p["dt_bias"])))

    attn = _attention(q, k, v)

    arow = _pad_lanes(row(-jnp.exp(p["a_log"])))
    dskip = row(jnp.repeat(p["d_skip"], SSM_HEAD_DIM))
    y = _ssd(xs, bm, cm, dt, z, arow, dskip, row(p["ssm_norm"]), tri, expand)

    w_out = p["w_out"]
    return _out_ffn(x1, attn, y, bf(_to_slab_head_order(w_out[:ATTN_WIDTH], 0)), bf(w_out[ATTN_WIDTH:]),
                    row(p["mix_post_norm"]), row(p["ffn2_pre_norm"]),
                    bf(p["ffn2_w_gate"]), bf(p["ffn2_w_up"]), bf(p["ffn2_w_down"]), row(p["ffn2_post_norm"]))


def kernel(x, positions, ffn1_pre_norm, ffn1_w_gate, ffn1_w_up, ffn1_w_down, ffn1_post_norm, mix_pre_norm, w_in, conv_w, conv_b, dt_bias, a_log, d_skip, ssm_norm, w_out, mix_post_norm, ffn2_pre_norm, ffn2_w_gate, ffn2_w_up, ffn2_w_down, ffn2_post_norm):
    params = dict(ffn1_pre_norm=ffn1_pre_norm, ffn1_w_gate=ffn1_w_gate, ffn1_w_up=ffn1_w_up,
                  ffn1_w_down=ffn1_w_down, ffn1_post_norm=ffn1_post_norm, mix_pre_norm=mix_pre_norm, w_in=w_in,
                  conv_w=conv_w, conv_b=conv_b, dt_bias=dt_bias, a_log=a_log, d_skip=d_skip, ssm_norm=ssm_norm,
                  w_out=w_out, mix_post_norm=mix_post_norm, ffn2_pre_norm=ffn2_pre_norm, ffn2_w_gate=ffn2_w_gate,
                  ffn2_w_up=ffn2_w_up, ffn2_w_down=ffn2_w_down, ffn2_post_norm=ffn2_post_norm)
    depth = w_in.shape[0]
    batch, seq, _ = x.shape
    inv_freq = ROPE_THETA ** (-jnp.arange(0, HEAD_DIM, 2, dtype=F32) / HEAD_DIM)
    invf = inv_freq.reshape(HEAD_DIM // 2, 1)
    pos3d = positions.reshape(batch, 1, seq)
    idx = jnp.arange(SSD_CHUNK)
    tri = (idx[:, None] >= idx[None, :]).astype(BF16)
    expand = (jnp.arange(2 * LANES)[:, None] % LANES == jnp.arange(SSM_WIDTH)[None, :] // SSM_HEAD_DIM).astype(BF16)
    for i in range(depth):
        x = _layer(x, pos3d, invf, tri, expand, {name: w[i] for name, w in params.items()})
    return x
```

```python
import functools
import math

import jax
import jax.numpy as jnp
from jax import lax
from jax.experimental import pallas as pl
from jax.experimental.pallas import tpu as pltpu

F32 = jnp.float32
BF16 = jnp.bfloat16

D_MODEL = 1024
D_FF = 2816
HEAD_DIM = 64
N_Q_HEADS = 16
N_KV_HEADS = 4
Q_PER_KV = N_Q_HEADS // N_KV_HEADS
ATTN_WIDTH = N_Q_HEADS * HEAD_DIM
KV_WIDTH = N_KV_HEADS * HEAD_DIM
ATTN_BLOCK = 128
ROPE_THETA = 10000.0
SSM_HEADS = 16
SSM_HEAD_DIM = 64
SSM_WIDTH = SSM_HEADS * SSM_HEAD_DIM
SSM_STATE = 128
SSM_GROUPS = 2
GROUP_WIDTH = SSM_WIDTH // SSM_GROUPS
BC_WIDTH = SSM_GROUPS * SSM_STATE
CONV_WIDTH = 4
CONV_CHANNELS = SSM_WIDTH + 2 * BC_WIDTH
SSD_CHUNK = 128
MACARON_WEIGHT = 0.5
NORM_EPS = 1e-6
LOG2_E = math.log2(math.e)

LANES = 128
CARRY_ROWS = 8
MIB = 1024 * 1024

FFN_ROWS = 512
FFN_TILES_PER_STEP = 2
OUT_FFN_TILES_PER_STEP = 2
FF_CHUNK = 256
assert D_FF % FF_CHUNK == 0
PROJ_ROWS = 512
SSD_BATCH = 4

Q_SLABS = ATTN_WIDTH // LANES
KV_SLABS = KV_WIDTH // LANES
KV_PAIRS = N_KV_HEADS // 2


def _rmsnorm(x, gain):
    return x * lax.rsqrt(jnp.mean(x * x, axis=-1, keepdims=True) + NORM_EPS) * gain


def _resident(shape):
    return pl.BlockSpec(shape, lambda *_: (0,) * len(shape), pipeline_mode=pl.Buffered(1))


def _swiglu_half_steps(load_tiles, pre, wg_ref, wu_ref, wd_ref, post, emit):
    finish = None
    x = load_tiles[0]()
    xn = _rmsnorm(x, pre).astype(BF16)
    for t in range(len(load_tiles)):
        acc = jnp.zeros(x.shape, F32)
        x_next = xn_next = None
        for c in range(D_FF // FF_CHUNK):
            cols = slice(c * FF_CHUNK, (c + 1) * FF_CHUNK)
            gate = jnp.dot(xn, wg_ref[:, cols], preferred_element_type=F32)
            up = jnp.dot(xn, wu_ref[:, cols], preferred_element_type=F32)
            hidden = (jax.nn.silu(gate) * up).astype(BF16)
            acc = acc + jnp.dot(hidden, wd_ref[cols, :], preferred_element_type=F32)
            if c == 1 and finish is not None:
                finish()
            if c == 2 and t + 1 < len(load_tiles):
                x_next = load_tiles[t + 1]()
                xn_next = _rmsnorm(x_next, pre).astype(BF16)
        finish = functools.partial(lambda t_, x_t, acc_t: emit(t_, x_t + MACARON_WEIGHT * _rmsnorm(acc_t, post)),
                                   t, x, acc)
        x, xn = x_next, xn_next
    finish()


def _ffn_kernel(x_ref, pre_ref, wg_ref, wu_ref, wd_ref, post_ref, o_ref):
    tiles = [functools.partial(lambda t: x_ref[t * FFN_ROWS:(t + 1) * FFN_ROWS, :], t)
             for t in range(FFN_TILES_PER_STEP)]

    def emit(t, out):
        o_ref[t * FFN_ROWS:(t + 1) * FFN_ROWS, :] = out

    _swiglu_half_steps(tiles, pre_ref[...], wg_ref, wu_ref, wd_ref, post_ref[...], emit)


def _ffn(x2d, pre, wg, wu, wd, post):
    tokens = x2d.shape[0]
    step_rows = FFN_ROWS * FFN_TILES_PER_STEP
    row_spec = pl.BlockSpec((step_rows, D_MODEL), lambda i: (i, 0))
    return pl.pallas_call(
        _ffn_kernel,
        grid=(tokens // step_rows,),
        in_specs=[row_spec, _resident((1, D_MODEL)), _resident((D_MODEL, D_FF)), _resident((D_MODEL, D_FF)),
                  _resident((D_FF, D_MODEL)), _resident((1, D_MODEL))],
        out_specs=row_spec,
        out_shape=jax.ShapeDtypeStruct((tokens, D_MODEL), F32),
        compiler_params=pltpu.CompilerParams(dimension_semantics=("parallel",), vmem_limit_bytes=48 * MIB),
        name="ffn",
    )(x2d, pre, wg, wu, wd, post)


def _softplus(x):
    return jnp.maximum(x, 0.0) + jnp.log1p(jnp.exp(-jnp.abs(x)))


def _inproj_kernel(x_ref, pos_ref, invf_ref, gain_ref, wq_ref, wk_ref, wv_ref, wx_ref, wz_ref, wdt_ref,
                   convw_ref, convb_ref, dtb_ref,
                   q_ref, k_ref, v_ref, xs_ref, b_ref, c_ref, z_ref, dt_ref, ubuf_ref):
    rows = x_ref.shape[1]
    xn = _rmsnorm(x_ref[0], gain_ref[...]).astype(BF16)
    wide = 2 * LANES

    def project(w_ref, c):
        return jnp.dot(xn, w_ref[:, c * wide:(c + 1) * wide], preferred_element_type=F32)

    n_slabs = CONV_CHANNELS // LANES

    @pl.when(pl.program_id(1) == 0)
    def _():
        ubuf_ref[:, 0:CARRY_ROWS, :] = jnp.zeros((n_slabs, CARRY_ROWS, LANES), F32)

    def conv_chunk(c, u):
        for j in (2 * c, 2 * c + 1):
            cols = slice(j * LANES, (j + 1) * LANES)
            ubuf_ref[j, CARRY_ROWS:CARRY_ROWS + rows, :] = u[:, (j % 2) * LANES:(j % 2 + 1) * LANES]
            conv = (convb_ref[:, cols]
                    + ubuf_ref[j, CARRY_ROWS:CARRY_ROWS + rows, :] * convw_ref[CONV_WIDTH - 1:CONV_WIDTH, cols])
            for tap in range(CONV_WIDTH - 1):
                start = CARRY_ROWS - (CONV_WIDTH - 1) + tap
                conv = conv + ubuf_ref[j, start:start + rows, :] * convw_ref[tap:tap + 1, cols]
            act = jax.nn.silu(conv)
            if j < SSM_WIDTH // LANES:
                xs_ref[0, :, cols] = act
            elif j < (SSM_WIDTH + BC_WIDTH) // LANES:
                b_ref[0, :, j * LANES - SSM_WIDTH:(j + 1) * LANES - SSM_WIDTH] = act.astype(BF16)
            else:
                first = SSM_WIDTH + BC_WIDTH
                c_ref[0, :, j * LANES - first:(j + 1) * LANES - first] = act.astype(BF16)
            ubuf_ref[j, 0:CARRY_ROWS, :] = ubuf_ref[j, rows:rows + CARRY_ROWS, :]

    def z_chunk(c):
        z_ref[0, :, c * wide:(c + 1) * wide] = project(wz_ref, c)

    u = project(wx_ref, 0)
    z_chunk(0)

    ang_t = invf_ref[...] * pos_ref[0].astype(F32)
    cos_t = jnp.cos(ang_t)
    sin_t = jnp.sin(ang_t)
    copies = LANES // HEAD_DIM
    cos = jnp.concatenate([cos_t, cos_t] * copies, axis=0).T
    sin = jnp.concatenate([-sin_t, sin_t] * copies, axis=0).T
    lane = lax.broadcasted_iota(jnp.int32, (rows, LANES), 1)
    first_half = (lane % HEAD_DIM) < (HEAD_DIM // 2)

    def rope(t):
        back = pltpu.roll(t, HEAD_DIM // 2, 1)
        fwd = pltpu.roll(t, LANES - HEAD_DIM // 2, 1)
        return t * cos + jnp.where(first_half, fwd, back) * sin

    def q_chunk(c):
        q = project(wq_ref, c)
        for half in range(2):
            q_ref[0, 2 * c + half] = rope(q[:, half * LANES:(half + 1) * LANES]) * (HEAD_DIM ** -0.5 * LOG2_E)

    def kv_chunk(c):
        k = project(wk_ref, c)
        v = project(wv_ref, c)
        for half in range(2):
            k_ref[0, 2 * c + half] = rope(k[:, half * LANES:(half + 1) * LANES])
            v_ref[0, 2 * c + half] = v[:, half * LANES:(half + 1) * LANES]

    dt_ref[0] = _softplus(jnp.dot(xn, wdt_ref[...], preferred_element_type=F32) + dtb_ref[...])
    n_conv = CONV_CHANNELS // wide
    fillers = ([functools.partial(q_chunk, c) for c in range(ATTN_WIDTH // wide)]
               + [functools.partial(kv_chunk, c) for c in range(KV_WIDTH // wide)]
               + [functools.partial(z_chunk, c) for c in range(1, SSM_WIDTH // wide)])
    for c in range(n_conv):
        u_next = project(wx_ref, c + 1) if c + 1 < n_conv else None
        conv_chunk(c, u)
        fillers.pop(0)()
        u = u_next
    for filler in fillers:
        filler()


def _in_proj(x3d, pos3d, invf, gain, wq, w_in, wdt, convw, convb, dtb):
    batch, seq, _ = x3d.shape
    rows = PROJ_ROWS

    def columns(start, width):
        assert start % width == 0
        return pl.BlockSpec((D_MODEL, width), lambda *_: (0, start // width), pipeline_mode=pl.Buffered(1))

    o_k = ATTN_WIDTH
    o_v = o_k + KV_WIDTH
    o_x = o_v + KV_WIDTH
    o_z = o_x + CONV_CHANNELS

    def tile(width):
        return pl.BlockSpec((1, rows, width), lambda b, j: (b, j, 0))

    def slabs(n):
        return pl.BlockSpec((1, n, rows, LANES), lambda b, j: (b, 0, j, 0))

    def out(width, dtype):
        return jax.ShapeDtypeStruct((batch, seq, width), dtype)

    def out_slabs(n):
        return jax.ShapeDtypeStruct((batch, n, seq, LANES), F32)

    return pl.pallas_call(
        _inproj_kernel,
        grid=(batch, seq // rows),
        in_specs=[tile(D_MODEL), pl.BlockSpec((1, 1, rows), lambda b, j: (b, 0, j)),
                  _resident((HEAD_DIM // 2, 1)), _resident((1, D_MODEL)),
                  _resident((D_MODEL, ATTN_WIDTH)), columns(o_k, KV_WIDTH), columns(o_v, KV_WIDTH),
                  columns(o_x, CONV_CHANNELS), columns(o_z, SSM_WIDTH),
                  _resident((D_MODEL, LANES)), _resident((CONV_WIDTH, CONV_CHANNELS)),
                  _resident((1, CONV_CHANNELS)), _resident((1, LANES))],
        out_specs=[slabs(Q_SLABS), slabs(KV_SLABS), slabs(KV_SLABS), tile(SSM_WIDTH), tile(BC_WIDTH),
                   tile(BC_WIDTH), tile(SSM_WIDTH), tile(LANES)],
        out_shape=[out_slabs(Q_SLABS), out_slabs(KV_SLABS), out_slabs(KV_SLABS), out(SSM_WIDTH, F32),
                   out(BC_WIDTH, BF16), out(BC_WIDTH, BF16), out(SSM_WIDTH, F32), out(LANES, F32)],
        scratch_shapes=[pltpu.VMEM((CONV_CHANNELS // LANES, rows + CARRY_ROWS, LANES), F32)],
        compiler_params=pltpu.CompilerParams(dimension_semantics=("parallel", "arbitrary"),
                                             vmem_limit_bytes=56 * MIB),
        name="in_proj",
    )(x3d, pos3d, invf, gain, wq, w_in, w_in, w_in, w_in, wdt, convw, convb, dtb)


STATE_ARRAYS = 3
MASKED = -2.0 ** 100
STATE_U, STATE_M, STATE_L = range(STATE_ARRAYS)


def _attn_blocks(blocks):
    blk = ATTN_BLOCK
    low = lax.broadcasted_iota(jnp.int32, (blk, LANES), 1) < HEAD_DIM
    zero = jnp.zeros((), BF16)

    scores = []
    for q_slabs, k_blk, _, (query_onehot, key_mask), _ in blocks:
        n_keys = k_blk.shape[0]
        kb = k_blk.astype(BF16)
        key_low = lax.broadcasted_iota(jnp.int32, kb.shape, 1) < HEAD_DIM
        keys = jnp.concatenate([jnp.concatenate([jnp.where(key_low, kb, zero), key_mask], axis=1),
                                jnp.concatenate([jnp.where(key_low, zero, kb), key_mask], axis=1)], axis=0)
        lhs = jnp.concatenate([jnp.concatenate([q.astype(BF16) for q in q_slabs], axis=0), query_onehot], axis=1)
        s = lax.dot_general(lhs, keys, (((1,), (1,)), ((), ())), preferred_element_type=F32)
        scores += [s[:, :n_keys], s[:, n_keys:]]
    maxes = [jnp.max(s, axis=-1, keepdims=True) for s in scores]
    probs = [jnp.exp2((s - m).astype(BF16)) for s, m in zip(scores, maxes)]
    results = []
    for i, (_, _, v_blk, _, _) in enumerate(blocks):
        vb = v_blk.astype(BF16)
        key_low = lax.broadcasted_iota(jnp.int32, vb.shape, 1) < HEAD_DIM
        ones_low = jnp.where(key_low, 1.0, 0.0).astype(BF16)
        ones_high = jnp.where(key_low, 0.0, 1.0).astype(BF16)
        rhs = jnp.concatenate([
            jnp.concatenate([jnp.where(key_low, vb, zero), ones_low], axis=1),
            jnp.concatenate([jnp.where(key_low, zero, vb), ones_high], axis=1)], axis=0)
        lhs = jnp.concatenate([probs[2 * i], probs[2 * i + 1]], axis=1)
        results.append(jnp.dot(lhs, rhs, preferred_element_type=F32))
    states = []
    for i, (q_slabs, _, _, _, load_old) in enumerate(blocks):
        result, m_a, m_b = results[i], maxes[2 * i], maxes[2 * i + 1]
        old = None if load_old is None else load_old()
        state = []
        for g in range(len(q_slabs)):
            rows = slice(g * blk, (g + 1) * blk)
            u_new = result[rows, :LANES]
            l_new = result[rows, LANES:]
            m_new = jnp.where(low, m_a[rows], m_b[rows])
            if old is not None:
                u_run, m_run, l_run = old[g]
                m_tot = jnp.maximum(m_run, m_new)
                a = jnp.exp2(m_run - m_tot)
                b = jnp.exp2(m_new - m_tot)
                u_new = u_run * a + u_new * b
                l_new = l_run * a + l_new * b
                m_new = m_tot
            state.append((u_new, m_new, l_new))
        states.append(state)
    return states


def _attn_kernel(q_ref, k_ref, v_ref, o_ref, st4_ref, stt_ref):
    blk = ATTN_BLOCK
    seq = q_ref.shape[2]
    n_slabs = q_ref.shape[1]
    sub4 = seq // 4

    def key_mask(n_keys, shift):
        si = lax.broadcasted_iota(jnp.int32, (n_keys, blk), 0)
        qi = lax.broadcasted_iota(jnp.int32, (n_keys, blk), 1) + shift
        return jnp.where((si <= qi) & (si >= qi - blk), 0.0, MASKED).astype(BF16)

    stacked = n_slabs * blk
    query_onehot = (lax.broadcasted_iota(jnp.int32, (stacked, blk), 0) % blk
                    == lax.broadcasted_iota(jnp.int32, (stacked, blk), 1)).astype(BF16)
    band_bias = (query_onehot, key_mask(2 * blk, blk))
    causal_bias = (query_onehot, key_mask(blk, 0))

    def load_q(rows):
        return [q_ref[0, g, rows, :] for g in range(n_slabs)]

    def load_state(ref, rows):
        return [tuple(ref[a * n_slabs + g, rows, :] for a in range(STATE_ARRAYS)) for g in range(n_slabs)]

    def store_state(ref, rows, state):
        for g in range(n_slabs):
            for a in range(STATE_ARRAYS):
                ref[a * n_slabs + g, rows, :] = state[g][a]

    def store_out(rows, state):
        for g in range(n_slabs):
            o_ref[0, g, rows, :] = (state[g][STATE_U] / state[g][STATE_L]).astype(o_ref.dtype)

    def run(plans):
        states = _attn_blocks([(load_q(rows), k_ref[0, 0, keys, :], v_ref[0, 0, keys, :], bias, load_old)
                               for rows, keys, bias, load_old, _ in plans])
        for (rows, _, _, _, store), state in zip(plans, states):
            store(rows, state)

    def paired(n_blocks, plan, per_trip=2):
        def trip(i, carry):
            run([plan(per_trip * i + j) for j in range(per_trip)])
            return carry
        lax.fori_loop(0, n_blocks // per_trip, trip, 0)

    def d16_plan(r16):
        rows = pl.ds(r16, blk, stride=16)
        dest = pl.ds((r16 % 4) * sub4 + r16 // 4, blk, stride=4)
        return rows, rows, causal_bias, None, lambda _, state: store_state(st4_ref, dest, state)

    paired(16, d16_plan, per_trip=8)

    def to_token_order(rows, state):
        store_state(stt_ref, rows, state)

    def d4_first_plan(r4):
        rows = pl.ds(r4, blk, stride=4)
        src = pl.ds(pl.multiple_of(r4 * sub4, blk), blk)
        return rows, rows, causal_bias, lambda: load_state(st4_ref, src), to_token_order

    paired(4, d4_first_plan, per_trip=4)
    later_blocks = sub4 // blk - 1

    def d4_plan(idx):
        r4 = idx // later_blocks
        n = 1 + idx % later_blocks
        rows = pl.ds(r4 + 4 * blk * n, blk, stride=4)
        keys = pl.ds(r4 + 4 * blk * (n - 1), 2 * blk, stride=4)
        src = pl.ds(pl.multiple_of(r4 * sub4 + n * blk, blk), blk)
        return rows, keys, band_bias, lambda: load_state(st4_ref, src), to_token_order

    paired(4 * later_blocks, d4_plan, per_trip=3)

    def d1_plan(n):
        start = n * blk if isinstance(n, int) else pl.multiple_of(n * blk, blk)
        rows = pl.ds(start, blk)
        if isinstance(n, int) and n == 0:
            return rows, rows, causal_bias, lambda: load_state(stt_ref, rows), store_out
        return rows, pl.ds(start - blk, 2 * blk), band_bias, lambda: load_state(stt_ref, rows), store_out

    last = seq // blk - 1
    run([d1_plan(0)])
    paired(last, lambda i: d1_plan(i + 1), per_trip=3)


def _attention(q, k, v):
    batch, _, seq, _ = q.shape
    slabs_per_pair = Q_SLABS // KV_PAIRS
    q_spec = pl.BlockSpec((1, slabs_per_pair, seq, LANES), lambda b, p: (b, p, 0, 0))
    kv_spec = pl.BlockSpec((1, 1, seq, LANES), lambda b, p: (b, p, 0, 0))
    state_shape = (STATE_ARRAYS * slabs_per_pair, seq, LANES)
    return pl.pallas_call(
        _attn_kernel,
        grid=(batch, KV_PAIRS),
        in_specs=[q_spec, kv_spec, kv_spec],
        out_specs=q_spec,
        out_shape=jax.ShapeDtypeStruct(q.shape, BF16),
        scratch_shapes=[pltpu.VMEM(state_shape, F32), pltpu.VMEM(state_shape, F32)],
        compiler_params=pltpu.CompilerParams(dimension_semantics=("parallel", "parallel"),
                                             vmem_limit_bytes=56 * MIB),
        name="attention",
    )(q, k, v)


def _split_bf16(x, pieces):
    parts = []
    for _ in range(pieces):
        part = x.astype(BF16)
        parts.append(part)
        x = x - part.astype(F32)
    return parts


def _ssd_kernel(xs_ref, b_ref, c_ref, dt_ref, z_ref, arow_ref, dskip_ref, norm_ref, tri_ref, expand_ref,
                y_ref, state_ref):
    chunk = SSD_CHUNK
    lanes = range(xs_ref.shape[0])
    heads_per_group = SSM_HEADS // SSM_GROUPS

    @pl.when(pl.program_id(1) == 0)
    def _():
        state_ref[...] = jnp.zeros(state_ref.shape, F32)

    row = lax.broadcasted_iota(jnp.int32, (chunk, chunk), 0)
    col = lax.broadcasted_iota(jnp.int32, (chunk, chunk), 1)
    causal = row >= col
    low = lax.broadcasted_iota(jnp.int32, (chunk, LANES), 1) < SSM_HEAD_DIM
    zero = jnp.zeros((), BF16)

    def group_cols(g):
        return slice(g * SSM_STATE, (g + 1) * SSM_STATE), slice(g * GROUP_WIDTH, (g + 1) * GROUP_WIDTH)

    dts = [dt_ref[i] for i in lanes]
    a3s = [jnp.dot(tri_ref[...], jnp.concatenate(_split_bf16(dt * arow_ref[...], 3), axis=1),
                   preferred_element_type=F32) for dt in dts]
    a_css = [a3[:, :LANES] + a3[:, LANES:2 * LANES] + a3[:, 2 * LANES:] for a3 in a3s]

    def expand(stat):
        return jnp.dot(jnp.concatenate(_split_bf16(stat, 2), axis=1), expand_ref[...], preferred_element_type=F32)

    w_states = [expand(jnp.exp2(a_cs[chunk - 1:chunk, :] - a_cs) * dt) for a_cs, dt in zip(a_css, dts)]
    scale_offs = [expand(jnp.exp2(a_cs)) for a_cs in a_css]
    source_ts = [(a_cs - jnp.log(dt) * LOG2_E).T for a_cs, dt in zip(a_css, dts)]

    cbs, y_offs, new_states = {}, {}, {}
    for i in lanes:
        xw_bf = (xs_ref[i] * w_states[i]).astype(BF16)
        for g in range(SSM_GROUPS):
            gcols, wide = group_cols(g)
            b_g = b_ref[i, :, gcols]
            c_g = c_ref[i, :, gcols]
            cbs[i, g] = lax.dot_general(c_g, b_g, (((1,), (1,)), ((), ())), preferred_element_type=F32)
            y_offs[i, g] = jnp.dot(c_g, state_ref[i, :, wide].astype(BF16), preferred_element_type=F32)
            b_t = b_g.astype(F32).T.astype(BF16)
            new_states[i, g] = jnp.dot(b_t, xw_bf[:, wide], preferred_element_type=F32)

    y_diags = {}
    for i in lanes:
        xs_bf = xs_ref[i].astype(BF16)
        for g in range(SSM_GROUPS):
            parts = []
            for pair in range(heads_per_group // 2):
                mats = []
                for e in range(2):
                    h = g * heads_per_group + 2 * pair + e
                    seg = a_css[i][:, h:h + 1] - source_ts[i][h:h + 1, :]
                    mats.append((cbs[i, g] * jnp.exp2(jnp.where(causal, seg, -jnp.inf))).astype(BF16))
                first = (g * heads_per_group + 2 * pair) * SSM_HEAD_DIM
                x_pair = xs_bf[:, first:first + 2 * SSM_HEAD_DIM]
                rhs = jnp.concatenate([jnp.where(low, x_pair, zero), jnp.where(low, zero, x_pair)], axis=0)
                parts.append(jnp.dot(jnp.concatenate(mats, axis=1), rhs, preferred_element_type=F32))
            y_diags[i, g] = jnp.concatenate(parts, axis=1)

    for i in lanes:
        chunk_decay = scale_offs[i][chunk - 1:chunk, :]
        y_parts = []
        for g in range(SSM_GROUPS):
            _, wide = group_cols(g)
            y_parts.append(y_diags[i, g] + y_offs[i, g] * scale_offs[i][:, wide])
            state_ref[i, :, wide] = state_ref[i, :, wide] * chunk_decay[:, wide] + new_states[i, g]
        y = jnp.concatenate(y_parts, axis=1) + dskip_ref[...] * xs_ref[i]
        y = y * jax.nn.silu(z_ref[i])
        normed = []
        for g in range(SSM_GROUPS):
            y_g = y[:, g * GROUP_WIDTH:(g + 1) * GROUP_WIDTH]
            normed.append(y_g * lax.rsqrt(jnp.mean(y_g * y_g, axis=-1, keepdims=True) + NORM_EPS))
        y_ref[i] = (jnp.concatenate(normed, axis=1) * norm_ref[...]).astype(y_ref.dtype)


def _ssd(xs, bm, cm, dt, z, arow, dskip, norm, tri, expand):
    batch, seq, _ = xs.shape

    def tile(width):
        return pl.BlockSpec((SSD_BATCH, SSD_CHUNK, width), lambda b, c: (b, c, 0))

    return pl.pallas_call(
        _ssd_kernel,
        grid=(batch // SSD_BATCH, seq // SSD_CHUNK),
        in_specs=[tile(SSM_WIDTH), tile(BC_WIDTH), tile(BC_WIDTH), tile(LANES), tile(SSM_WIDTH),
                  _resident((1, LANES)), _resident((1, SSM_WIDTH)), _resident((1, SSM_WIDTH)),
                  _resident((SSD_CHUNK, SSD_CHUNK)), _resident((2 * LANES, SSM_WIDTH))],
        out_specs=tile(SSM_WIDTH),
        out_shape=jax.ShapeDtypeStruct((batch, seq, SSM_WIDTH), BF16),
        scratch_shapes=[pltpu.VMEM((SSD_BATCH, SSM_STATE, SSM_WIDTH), F32)],
        compiler_params=pltpu.CompilerParams(dimension_semantics=("parallel", "arbitrary"),
                                             vmem_limit_bytes=32 * MIB),
        name="ssd",
    )(xs, bm, cm, dt, z, arow, dskip, norm, tri, expand)


def _out_ffn_kernel(x_ref, attn_ref, y_ref, wo_attn_ref, wo_ssm_ref, mixpost_ref, pre_ref, wg_ref, wu_ref, wd_ref,
                    post_ref, o_ref):
    def mixer_residual(t):
        rows = slice(t * FFN_ROWS, (t + 1) * FFN_ROWS)
        attn = jnp.concatenate([attn_ref[0, j, rows, :] for j in range(Q_SLABS)], axis=1)
        mixed = (jnp.dot(attn, wo_attn_ref[...], preferred_element_type=F32)
                 + jnp.dot(y_ref[0, rows, :], wo_ssm_ref[...], preferred_element_type=F32))
        return x_ref[0, rows, :] + _rmsnorm(mixed, mixpost_ref[...])

    def emit(t, out):
        o_ref[0, t * FFN_ROWS:(t + 1) * FFN_ROWS, :] = out

    tiles = [functools.partial(mixer_residual, t) for t in range(OUT_FFN_TILES_PER_STEP)]
    _swiglu_half_steps(tiles, pre_ref[...], wg_ref, wu_ref, wd_ref, post_ref[...], emit)


def _out_ffn(x3d, attn, y3d, wo_attn, wo_ssm, mixpost, pre, wg, wu, wd, post):
    batch, seq, _ = x3d.shape
    rows = FFN_ROWS * OUT_FFN_TILES_PER_STEP
    row_spec = pl.BlockSpec((1, rows, D_MODEL), lambda b, j: (b, j, 0))
    slab_spec = pl.BlockSpec((1, Q_SLABS, rows, LANES), lambda b, j: (b, 0, j, 0))
    return pl.pallas_call(
        _out_ffn_kernel,
        grid=(batch, seq // rows),
        in_specs=[row_spec, slab_spec, row_spec, _resident((ATTN_WIDTH, D_MODEL)), _resident((SSM_WIDTH, D_MODEL)),
                  _resident((1, D_MODEL)), _resident((1, D_MODEL)), _resident((D_MODEL, D_FF)),
                  _resident((D_MODEL, D_FF)), _resident((D_FF, D_MODEL)), _resident((1, D_MODEL))],
        out_specs=row_spec,
        out_shape=jax.ShapeDtypeStruct((batch, seq, D_MODEL), F32),
        compiler_params=pltpu.CompilerParams(dimension_semantics=("parallel", "parallel"),
                                             vmem_limit_bytes=58 * MIB),
        name="out_ffn",
    )(x3d, attn, y3d, wo_attn, wo_ssm, mixpost, pre, wg, wu, wd, post)


def _to_slab_head_order(w, axis):
    shape = w.shape
    split = shape[:axis] + (KV_PAIRS, 2, Q_PER_KV, HEAD_DIM) + shape[axis + 1:]
    return jnp.swapaxes(w.reshape(split), axis + 1, axis + 2).reshape(shape)


def _pad_lanes(v):
    return jnp.pad(v, [(0, 0)] * (v.ndim - 1) + [(0, LANES - v.shape[-1])])


def _layer(x, pos3d, invf, tri, expand, p):
    batch, seq, _ = x.shape
    tokens = batch * seq
    row = lambda v: v.reshape(1, -1)
    bf = lambda w: w.astype(BF16)

    x1 = _ffn(x.reshape(tokens, D_MODEL), row(p["ffn1_pre_norm"]), bf(p["ffn1_w_gate"]), bf(p["ffn1_w_up"]),
              bf(p["ffn1_w_down"]), row(p["ffn1_post_norm"])).reshape(batch, seq, D_MODEL)

    w_in = p["w_in"]
    o_dt = ATTN_WIDTH + 2 * KV_WIDTH + CONV_CHANNELS + SSM_WIDTH
    q, k, v, xs, bm, cm, z, dt = _in_proj(
        x1, pos3d, invf, row(p["mix_pre_norm"]),
        bf(_to_slab_head_order(w_in[:, :ATTN_WIDTH], 1)), bf(w_in), bf(_pad_lanes(w_in[:, o_dt:])),
        p["conv_w"], row(p["conv_b"]), _pad_lanes(row(p["dt_bias"])))

    attn = _attention(q, k, v)

    arow = _pad_lanes(row(-jnp.exp(p["a_log"]) * LOG2_E))
    dskip = row(jnp.repeat(p["d_skip"], SSM_HEAD_DIM))
    y = _ssd(xs, bm, cm, dt, z, arow, dskip, row(p["ssm_norm"]), tri, expand)

    w_out = p["w_out"]
    return _out_ffn(x1, attn, y, bf(_to_slab_head_order(w_out[:ATTN_WIDTH], 0)), bf(w_out[ATTN_WIDTH:]),
                    row(p["mix_post_norm"]), row(p["ffn2_pre_norm"]),
                    bf(p["ffn2_w_gate"]), bf(p["ffn2_w_up"]), bf(p["ffn2_w_down"]), row(p["ffn2_post_norm"]))


def kernel(x, positions, ffn1_pre_norm, ffn1_w_gate, ffn1_w_up, ffn1_w_down, ffn1_post_norm, mix_pre_norm, w_in, conv_w, conv_b, dt_bias, a_log, d_skip, ssm_norm, w_out, mix_post_norm, ffn2_pre_norm, ffn2_w_gate, ffn2_w_up, ffn2_w_down, ffn2_post_norm):
    params = dict(ffn1_pre_norm=ffn1_pre_norm, ffn1_w_gate=ffn1_w_gate, ffn1_w_up=ffn1_w_up,
                  ffn1_w_down=ffn1_w_down, ffn1_post_norm=ffn1_post_norm, mix_pre_norm=mix_pre_norm, w_in=w_in,
                  conv_w=conv_w, conv_b=conv_b, dt_bias=dt_bias, a_log=a_log, d_skip=d_skip, ssm_norm=ssm_norm,
                  w_out=w_out, mix_post_norm=mix_post_norm, ffn2_pre_norm=ffn2_pre_norm, ffn2_w_gate=ffn2_w_gate,
                  ffn2_w_up=ffn2_w_up, ffn2_w_down=ffn2_w_down, ffn2_post_norm=ffn2_post_norm)
    depth = w_in.shape[0]
    batch, seq, _ = x.shape
    inv_freq = ROPE_THETA ** (-jnp.arange(0, HEAD_DIM, 2, dtype=F32) / HEAD_DIM)
    invf = inv_freq.reshape(HEAD_DIM // 2, 1)
    pos3d = positions.reshape(batch, 1, seq)
    idx = jnp.arange(SSD_CHUNK)
    tri = (idx[:, None] >= idx[None, :]).astype(BF16)
    expand = (jnp.arange(2 * LANES)[:, None] % LANES == jnp.arange(SSM_WIDTH)[None, :] // SSM_HEAD_DIM).astype(BF16)
    for i in range(depth):
        x = _layer(x, pos3d, invf, tri, expand, {name: w[i] for name, w in params.items()})
    return x
```

```python
import functools
import math

import jax
import jax.numpy as jnp
from jax import lax
from jax.experimental import pallas as pl
from jax.experimental.pallas import tpu as pltpu

F32 = jnp.float32
BF16 = jnp.bfloat16

D_MODEL = 1024
D_FF = 2816
HEAD_DIM = 64
N_Q_HEADS = 16
N_KV_HEADS = 4
Q_PER_KV = N_Q_HEADS // N_KV_HEADS
ATTN_WIDTH = N_Q_HEADS * HEAD_DIM
KV_WIDTH = N_KV_HEADS * HEAD_DIM
ATTN_BLOCK = 128
ROPE_THETA = 10000.0
SSM_HEADS = 16
SSM_HEAD_DIM = 64
SSM_WIDTH = SSM_HEADS * SSM_HEAD_DIM
SSM_STATE = 128
SSM_GROUPS = 2
GROUP_WIDTH = SSM_WIDTH // SSM_GROUPS
BC_WIDTH = SSM_GROUPS * SSM_STATE
CONV_WIDTH = 4
CONV_CHANNELS = SSM_WIDTH + 2 * BC_WIDTH
SSD_CHUNK = 128
MACARON_WEIGHT = 0.5
NORM_EPS = 1e-6
LOG2_E = math.log2(math.e)

LANES = 128
CARRY_ROWS = 8
MIB = 1024 * 1024

FFN_ROWS = 512
FFN_TILES_PER_STEP = 2
OUT_FFN_TILES_PER_STEP = 2
FF_CHUNK = 256
assert D_FF % FF_CHUNK == 0
PROJ_ROWS = 512
SSD_BATCH = 4

Q_SLABS = ATTN_WIDTH // LANES
KV_SLABS = KV_WIDTH // LANES
KV_PAIRS = N_KV_HEADS // 2


def _rmsnorm(x, gain):
    return x * lax.rsqrt(jnp.mean(x * x, axis=-1, keepdims=True) + NORM_EPS) * gain


def _resident(shape):
    return pl.BlockSpec(shape, lambda *_: (0,) * len(shape), pipeline_mode=pl.Buffered(1))


def _swiglu_half_steps(load_tiles, pre, wg_ref, wu_ref, wd_ref, post, emit):
    finish = None
    x = load_tiles[0]()
    xn = _rmsnorm(x, pre).astype(BF16)
    for t in range(len(load_tiles)):
        acc = jnp.zeros(x.shape, F32)
        x_next = xn_next = None
        for c in range(D_FF // FF_CHUNK):
            cols = slice(c * FF_CHUNK, (c + 1) * FF_CHUNK)
            gate = jnp.dot(xn, wg_ref[:, cols], preferred_element_type=F32)
            up = jnp.dot(xn, wu_ref[:, cols], preferred_element_type=F32)
            hidden = (jax.nn.silu(gate) * up).astype(BF16)
            acc = acc + jnp.dot(hidden, wd_ref[cols, :], preferred_element_type=F32)
            if c == 1 and finish is not None:
                finish()
            if c == 2 and t + 1 < len(load_tiles):
                x_next = load_tiles[t + 1]()
                xn_next = _rmsnorm(x_next, pre).astype(BF16)
        finish = functools.partial(lambda t_, x_t, acc_t: emit(t_, x_t + MACARON_WEIGHT * _rmsnorm(acc_t, post)),
                                   t, x, acc)
        x, xn = x_next, xn_next
    finish()


def _ffn_kernel(x_ref, pre_ref, wg_ref, wu_ref, wd_ref, post_ref, o_ref):
    tiles = [functools.partial(lambda t: x_ref[t * FFN_ROWS:(t + 1) * FFN_ROWS, :], t)
             for t in range(FFN_TILES_PER_STEP)]

    def emit(t, out):
        o_ref[t * FFN_ROWS:(t + 1) * FFN_ROWS, :] = out

    _swiglu_half_steps(tiles, pre_ref[...], wg_ref, wu_ref, wd_ref, post_ref[...], emit)


def _ffn(x2d, pre, wg, wu, wd, post):
    tokens = x2d.shape[0]
    step_rows = FFN_ROWS * FFN_TILES_PER_STEP
    row_spec = pl.BlockSpec((step_rows, D_MODEL), lambda i: (i, 0))
    return pl.pallas_call(
        _ffn_kernel,
        grid=(tokens // step_rows,),
        in_specs=[row_spec, _resident((1, D_MODEL)), _resident((D_MODEL, D_FF)), _resident((D_MODEL, D_FF)),
                  _resident((D_FF, D_MODEL)), _resident((1, D_MODEL))],
        out_specs=row_spec,
        out_shape=jax.ShapeDtypeStruct((tokens, D_MODEL), F32),
        compiler_params=pltpu.CompilerParams(dimension_semantics=("parallel",), vmem_limit_bytes=48 * MIB),
        name="ffn",
    )(x2d, pre, wg, wu, wd, post)


def _softplus(x):
    return jnp.maximum(x, 0.0) + jnp.log1p(jnp.exp(-jnp.abs(x)))


def _inproj_kernel(x_ref, pos_ref, invf_ref, gain_ref, wq_ref, wk_ref, wv_ref, wx_ref, wz_ref, wdt_ref,
                   convw_ref, convb_ref, dtb_ref,
                   q_ref, k_ref, v_ref, xs_ref, b_ref, c_ref, z_ref, dt_ref, ubuf_ref):
    rows = x_ref.shape[1]
    xn = _rmsnorm(x_ref[0], gain_ref[...]).astype(BF16)
    wide = 2 * LANES

    def project(w_ref, c):
        return jnp.dot(xn, w_ref[:, c * wide:(c + 1) * wide], preferred_element_type=F32)

    n_slabs = CONV_CHANNELS // LANES

    @pl.when(pl.program_id(1) == 0)
    def _():
        ubuf_ref[:, 0:CARRY_ROWS, :] = jnp.zeros((n_slabs, CARRY_ROWS, LANES), F32)

    def conv_chunk(c, u):
        for j in (2 * c, 2 * c + 1):
            cols = slice(j * LANES, (j + 1) * LANES)
            ubuf_ref[j, CARRY_ROWS:CARRY_ROWS + rows, :] = u[:, (j % 2) * LANES:(j % 2 + 1) * LANES]
            conv = (convb_ref[:, cols]
                    + ubuf_ref[j, CARRY_ROWS:CARRY_ROWS + rows, :] * convw_ref[CONV_WIDTH - 1:CONV_WIDTH, cols])
            for tap in range(CONV_WIDTH - 1):
                start = CARRY_ROWS - (CONV_WIDTH - 1) + tap
                conv = conv + ubuf_ref[j, start:start + rows, :] * convw_ref[tap:tap + 1, cols]
            act = jax.nn.silu(conv)
            if j < SSM_WIDTH // LANES:
                xs_ref[0, :, cols] = act
            elif j < (SSM_WIDTH + BC_WIDTH) // LANES:
                b_ref[0, :, j * LANES - SSM_WIDTH:(j + 1) * LANES - SSM_WIDTH] = act.astype(BF16)
            else:
                first = SSM_WIDTH + BC_WIDTH
                c_ref[0, :, j * LANES - first:(j + 1) * LANES - first] = act.astype(BF16)
            ubuf_ref[j, 0:CARRY_ROWS, :] = ubuf_ref[j, rows:rows + CARRY_ROWS, :]

    def z_chunk(c):
        z_ref[0, :, c * wide:(c + 1) * wide] = project(wz_ref, c)

    u = project(wx_ref, 0)
    z_chunk(0)

    ang_t = invf_ref[...] * pos_ref[0].astype(F32)
    cos_t = jnp.cos(ang_t)
    sin_t = jnp.sin(ang_t)
    copies = LANES // HEAD_DIM
    cos = jnp.concatenate([cos_t, cos_t] * copies, axis=0).T
    sin = jnp.concatenate([-sin_t, sin_t] * copies, axis=0).T
    lane = lax.broadcasted_iota(jnp.int32, (rows, LANES), 1)
    first_half = (lane % HEAD_DIM) < (HEAD_DIM // 2)

    def rope(t):
        back = pltpu.roll(t, HEAD_DIM // 2, 1)
        fwd = pltpu.roll(t, LANES - HEAD_DIM // 2, 1)
        return t * cos + jnp.where(first_half, fwd, back) * sin

    def q_chunk(c):
        q = project(wq_ref, c)
        for half in range(2):
            q_ref[0, 2 * c + half] = rope(q[:, half * LANES:(half + 1) * LANES]) * (HEAD_DIM ** -0.5 * LOG2_E)

    def kv_chunk(c):
        k = project(wk_ref, c)
        v = project(wv_ref, c)
        for half in range(2):
            k_ref[0, 2 * c + half] = rope(k[:, half * LANES:(half + 1) * LANES])
            v_ref[0, 2 * c + half] = v[:, half * LANES:(half + 1) * LANES]

    dt_ref[0] = _softplus(jnp.dot(xn, wdt_ref[...], preferred_element_type=F32) + dtb_ref[...])
    n_conv = CONV_CHANNELS // wide
    fillers = ([functools.partial(q_chunk, c) for c in range(ATTN_WIDTH // wide)]
               + [functools.partial(kv_chunk, c) for c in range(KV_WIDTH // wide)]
               + [functools.partial(z_chunk, c) for c in range(1, SSM_WIDTH // wide)])
    for c in range(n_conv):
        u_next = project(wx_ref, c + 1) if c + 1 < n_conv else None
        conv_chunk(c, u)
        fillers.pop(0)()
        u = u_next
    for filler in fillers:
        filler()


def _in_proj(x3d, pos3d, invf, gain, wq, w_in, wdt, convw, convb, dtb):
    batch, seq, _ = x3d.shape
    rows = PROJ_ROWS

    def columns(start, width):
        assert start % width == 0
        return pl.BlockSpec((D_MODEL, width), lambda *_: (0, start // width), pipeline_mode=pl.Buffered(1))

    o_k = ATTN_WIDTH
    o_v = o_k + KV_WIDTH
    o_x = o_v + KV_WIDTH
    o_z = o_x + CONV_CHANNELS

    def tile(width):
        return pl.BlockSpec((1, rows, width), lambda b, j: (b, j, 0))

    def slabs(n):
        return pl.BlockSpec((1, n, rows, LANES), lambda b, j: (b, 0, j, 0))

    def out(width, dtype):
        return jax.ShapeDtypeStruct((batch, seq, width), dtype)

    def out_slabs(n):
        return jax.ShapeDtypeStruct((batch, n, seq, LANES), F32)

    return pl.pallas_call(
        _inproj_kernel,
        grid=(batch, seq // rows),
        in_specs=[tile(D_MODEL), pl.BlockSpec((1, 1, rows), lambda b, j: (b, 0, j)),
                  _resident((HEAD_DIM // 2, 1)), _resident((1, D_MODEL)),
                  _resident((D_MODEL, ATTN_WIDTH)), columns(o_k, KV_WIDTH), columns(o_v, KV_WIDTH),
                  columns(o_x, CONV_CHANNELS), columns(o_z, SSM_WIDTH),
                  _resident((D_MODEL, LANES)), _resident((CONV_WIDTH, CONV_CHANNELS)),
                  _resident((1, CONV_CHANNELS)), _resident((1, LANES))],
        out_specs=[slabs(Q_SLABS), slabs(KV_SLABS), slabs(KV_SLABS), tile(SSM_WIDTH), tile(BC_WIDTH),
                   tile(BC_WIDTH), tile(SSM_WIDTH), tile(LANES)],
        out_shape=[out_slabs(Q_SLABS), out_slabs(KV_SLABS), out_slabs(KV_SLABS), out(SSM_WIDTH, F32),
                   out(BC_WIDTH, BF16), out(BC_WIDTH, BF16), out(SSM_WIDTH, F32), out(LANES, F32)],
        scratch_shapes=[pltpu.VMEM((CONV_CHANNELS // LANES, rows + CARRY_ROWS, LANES), F32)],
        compiler_params=pltpu.CompilerParams(dimension_semantics=("parallel", "arbitrary"),
                                             vmem_limit_bytes=56 * MIB),
        name="in_proj",
    )(x3d, pos3d, invf, gain, wq, w_in, w_in, w_in, w_in, wdt, convw, convb, dtb)


STATE_ARRAYS = 3
MASKED = -2.0 ** 100
STATE_U, STATE_M, STATE_L = range(STATE_ARRAYS)


def _attn_blocks(blocks):
    blk = ATTN_BLOCK
    low = lax.broadcasted_iota(jnp.int32, (blk, LANES), 1) < HEAD_DIM
    zero = jnp.zeros((), BF16)

    scores = []
    for q_slabs, k_blk, _, (query_onehot, key_mask), _ in blocks:
        n_keys = k_blk.shape[0]
        kb = k_blk.astype(BF16)
        key_low = lax.broadcasted_iota(jnp.int32, kb.shape, 1) < HEAD_DIM
        keys = jnp.concatenate([jnp.concatenate([jnp.where(key_low, kb, zero), key_mask], axis=1),
                                jnp.concatenate([jnp.where(key_low, zero, kb), key_mask], axis=1)], axis=0)
        lhs = jnp.concatenate([jnp.concatenate([q.astype(BF16) for q in q_slabs], axis=0), query_onehot], axis=1)
        s = lax.dot_general(lhs, keys, (((1,), (1,)), ((), ())), preferred_element_type=F32)
        scores += [s[:, :n_keys], s[:, n_keys:]]
    maxes = [jnp.max(s, axis=-1, keepdims=True) for s in scores]
    probs = [jnp.exp2((s - m).astype(BF16)) for s, m in zip(scores, maxes)]
    results = []
    for i, (_, _, v_blk, _, _) in enumerate(blocks):
        vb = v_blk.astype(BF16)
        key_low = lax.broadcasted_iota(jnp.int32, vb.shape, 1) < HEAD_DIM
        ones_low = jnp.where(key_low, 1.0, 0.0).astype(BF16)
        ones_high = jnp.where(key_low, 0.0, 1.0).astype(BF16)
        rhs = jnp.concatenate([
            jnp.concatenate([jnp.where(key_low, vb, zero), ones_low], axis=1),
            jnp.concatenate([jnp.where(key_low, zero, vb), ones_high], axis=1)], axis=0)
        lhs = jnp.concatenate([probs[2 * i], probs[2 * i + 1]], axis=1)
        results.append(jnp.dot(lhs, rhs, preferred_element_type=F32))
    states = []
    for i, (q_slabs, _, _, _, load_old) in enumerate(blocks):
        result, m_a, m_b = results[i], maxes[2 * i], maxes[2 * i + 1]
        old = None if load_old is None else load_old()
        state = []
        for g in range(len(q_slabs)):
            rows = slice(g * blk, (g + 1) * blk)
            u_new = result[rows, :LANES]
            l_new = result[rows, LANES:]
            m_new = jnp.where(low, m_a[rows], m_b[rows])
            if old is not None:
                u_run, m_run, l_run = old[g]
                m_tot = jnp.maximum(m_run, m_new)
                a = jnp.exp2(m_run - m_tot)
                b = jnp.exp2(m_new - m_tot)
                u_new = u_run * a + u_new * b
                l_new = l_run * a + l_new * b
                m_new = m_tot
            state.append((u_new, m_new, l_new))
        states.append(state)
    return states


def _attn_kernel(q_ref, k_ref, v_ref, o_ref, st4_ref, stt_ref):
    blk = ATTN_BLOCK
    seq = q_ref.shape[2]
    n_slabs = q_ref.shape[1]
    sub4 = seq // 4

    def key_mask(n_keys, shift):
        si = lax.broadcasted_iota(jnp.int32, (n_keys, blk), 0)
        qi = lax.broadcasted_iota(jnp.int32, (n_keys, blk), 1) + shift
        return jnp.where((si <= qi) & (si >= qi - blk), 0.0, MASKED).astype(BF16)

    stacked = n_slabs * blk
    query_onehot = (lax.broadcasted_iota(jnp.int32, (stacked, blk), 0) % blk
                    == lax.broadcasted_iota(jnp.int32, (stacked, blk), 1)).astype(BF16)
    band_bias = (query_onehot, key_mask(2 * blk, blk))
    causal_bias = (query_onehot, key_mask(blk, 0))

    def load_q(rows):
        return [q_ref[0, g, rows, :] for g in range(n_slabs)]

    def load_state(ref, rows):
        return [tuple(ref[a * n_slabs + g, rows, :] for a in range(STATE_ARRAYS)) for g in range(n_slabs)]

    def store_state(ref, rows, state):
        for g in range(n_slabs):
            for a in range(STATE_ARRAYS):
                ref[a * n_slabs + g, rows, :] = state[g][a]

    def store_out(rows, state):
        for g in range(n_slabs):
            o_ref[0, g, rows, :] = (state[g][STATE_U] / state[g][STATE_L]).astype(o_ref.dtype)

    def run(plans):
        states = _attn_blocks([(load_q(rows), k_ref[0, 0, keys, :], v_ref[0, 0, keys, :], bias, load_old)
                               for rows, keys, bias, load_old, _ in plans])
        for (rows, _, _, _, store), state in zip(plans, states):
            store(rows, state)

    def paired(n_blocks, plan, per_trip=2):
        def trip(i, carry):
            run([plan(per_trip * i + j) for j in range(per_trip)])
            return carry
        lax.fori_loop(0, n_blocks // per_trip, trip, 0)

    def d16_plan(r16):
        rows = pl.ds(r16, blk, stride=16)
        dest = pl.ds((r16 % 4) * sub4 + r16 // 4, blk, stride=4)
        return rows, rows, causal_bias, None, lambda _, state: store_state(st4_ref, dest, state)

    paired(16, d16_plan, per_trip=8)

    def to_token_order(rows, state):
        store_state(stt_ref, rows, state)

    def d4_first_plan(r4):
        rows = pl.ds(r4, blk, stride=4)
        src = pl.ds(pl.multiple_of(r4 * sub4, blk), blk)
        return rows, rows, causal_bias, lambda: load_state(st4_ref, src), to_token_order

    paired(4, d4_first_plan, per_trip=4)
    later_blocks = sub4 // blk - 1

    def d4_plan(idx):
        r4 = idx // later_blocks
        n = 1 + idx % later_blocks
        rows = pl.ds(r4 + 4 * blk * n, blk, stride=4)
        keys = pl.ds(r4 + 4 * blk * (n - 1), 2 * blk, stride=4)
        src = pl.ds(pl.multiple_of(r4 * sub4 + n * blk, blk), blk)
        return rows, keys, band_bias, lambda: load_state(st4_ref, src), to_token_order

    paired(4 * later_blocks, d4_plan, per_trip=3)

    def d1_plan(n):
        start = n * blk if isinstance(n, int) else pl.multiple_of(n * blk, blk)
        rows = pl.ds(start, blk)
        if isinstance(n, int) and n == 0:
            return rows, rows, causal_bias, lambda: load_state(stt_ref, rows), store_out
        return rows, pl.ds(start - blk, 2 * blk), band_bias, lambda: load_state(stt_ref, rows), store_out

    last = seq // blk - 1
    run([d1_plan(0)])
    paired(last, lambda i: d1_plan(i + 1), per_trip=3)


def _attention(q, k, v):
    batch, _, seq, _ = q.shape
    slabs_per_pair = Q_SLABS // KV_PAIRS
    q_spec = pl.BlockSpec((1, slabs_per_pair, seq, LANES), lambda b, p: (b, p, 0, 0))
    kv_spec = pl.BlockSpec((1, 1, seq, LANES), lambda b, p: (b, p, 0, 0))
    state_shape = (STATE_ARRAYS * slabs_per_pair, seq, LANES)
    return pl.pallas_call(
        _attn_kernel,
        grid=(batch, KV_PAIRS),
        in_specs=[q_spec, kv_spec, kv_spec],
        out_specs=q_spec,
        out_shape=jax.ShapeDtypeStruct(q.shape, BF16),
        scratch_shapes=[pltpu.VMEM(state_shape, F32), pltpu.VMEM(state_shape, F32)],
        compiler_params=pltpu.CompilerParams(dimension_semantics=("parallel", "parallel"),
                                             vmem_limit_bytes=56 * MIB),
        name="attention",
    )(q, k, v)


def _split_bf16(x, pieces):
    parts = []
    for _ in range(pieces):
        part = x.astype(BF16)
        parts.append(part)
        x = x - part.astype(F32)
    return parts


def _ssd_kernel(xs_ref, b_ref, c_ref, dt_ref, z_ref, arow_ref, dskip_ref, norm_ref, tri_ref, expand_ref,
                y_ref, state_ref):
    chunk = SSD_CHUNK
    lanes = range(xs_ref.shape[0])
    heads_per_group = SSM_HEADS // SSM_GROUPS

    @pl.when(pl.program_id(1) == 0)
    def _():
        state_ref[...] = jnp.zeros(state_ref.shape, F32)

    row = lax.broadcasted_iota(jnp.int32, (chunk, chunk), 0)
    col = lax.broadcasted_iota(jnp.int32, (chunk, chunk), 1)
    causal = row >= col
    low = lax.broadcasted_iota(jnp.int32, (chunk, LANES), 1) < SSM_HEAD_DIM
    zero = jnp.zeros((), BF16)

    def group_cols(g):
        return slice(g * SSM_STATE, (g + 1) * SSM_STATE), slice(g * GROUP_WIDTH, (g + 1) * GROUP_WIDTH)

    dts = [dt_ref[i] for i in lanes]
    a3s = [jnp.dot(tri_ref[...], jnp.concatenate(_split_bf16(dt * arow_ref[...], 3), axis=1),
                   preferred_element_type=F32) for dt in dts]
    a_css = [a3[:, :LANES] + a3[:, LANES:2 * LANES] + a3[:, 2 * LANES:] for a3 in a3s]

    def expand(stat):
        return jnp.dot(jnp.concatenate(_split_bf16(stat, 2), axis=1), expand_ref[...], preferred_element_type=F32)

    w_states = [expand(jnp.exp2(a_cs[chunk - 1:chunk, :] - a_cs) * dt) for a_cs, dt in zip(a_css, dts)]
    scale_offs = [expand(jnp.exp2(a_cs)) for a_cs in a_css]
    source_ts = [(a_cs - jnp.log(dt) * LOG2_E).T for a_cs, dt in zip(a_css, dts)]

    cbs, y_offs, new_states = {}, {}, {}
    for i in lanes:
        xw_bf = (xs_ref[i] * w_states[i]).astype(BF16)
        for g in range(SSM_GROUPS):
            gcols, wide = group_cols(g)
            b_g = b_ref[i, :, gcols]
            c_g = c_ref[i, :, gcols]
            cbs[i, g] = lax.dot_general(c_g, b_g, (((1,), (1,)), ((), ())), preferred_element_type=F32)
            y_offs[i, g] = jnp.dot(c_g, state_ref[i, :, wide].astype(BF16), preferred_element_type=F32)
            b_t = b_g.astype(F32).T.astype(BF16)
            new_states[i, g] = jnp.dot(b_t, xw_bf[:, wide], preferred_element_type=F32)

    y_diags = {}
    for i in lanes:
        xs_bf = xs_ref[i].astype(BF16)
        for g in range(SSM_GROUPS):
            parts = []
            for pair in range(heads_per_group // 2):
                mats = []
                for e in range(2):
                    h = g * heads_per_group + 2 * pair + e
                    seg = a_css[i][:, h:h + 1] - source_ts[i][h:h + 1, :]
                    mats.append((cbs[i, g] * jnp.exp2(jnp.where(causal, seg, -jnp.inf))).astype(BF16))
                first = (g * heads_per_group + 2 * pair) * SSM_HEAD_DIM
                x_pair = xs_bf[:, first:first + 2 * SSM_HEAD_DIM]
                rhs = jnp.concatenate([jnp.where(low, x_pair, zero), jnp.where(low, zero, x_pair)], axis=0)
                parts.append(jnp.dot(jnp.concatenate(mats, axis=1), rhs, preferred_element_type=F32))
            y_diags[i, g] = jnp.concatenate(parts, axis=1)

    for i in lanes:
        chunk_decay = scale_offs[i][chunk - 1:chunk, :]
        y_parts = []
        for g in range(SSM_GROUPS):
            _, wide = group_cols(g)
            y_parts.append(y_diags[i, g] + y_offs[i, g] * scale_offs[i][:, wide])
            state_ref[i, :, wide] = state_ref[i, :, wide] * chunk_decay[:, wide] + new_states[i, g]
        y = jnp.concatenate(y_parts, axis=1) + dskip_ref[...] * xs_ref[i]
        y = y * jax.nn.silu(z_ref[i])
        normed = []
        for g in range(SSM_GROUPS):
            y_g = y[:, g * GROUP_WIDTH:(g + 1) * GROUP_WIDTH]
            normed.append(y_g * lax.rsqrt(jnp.mean(y_g * y_g, axis=-1, keepdims=True) + NORM_EPS))
        y_ref[i] = (jnp.concatenate(normed, axis=1) * norm_ref[...]).astype(y_ref.dtype)


def _ssd(xs, bm, cm, dt, z, arow, dskip, norm, tri, expand):
    batch, seq, _ = xs.shape

    def tile(width):
        return pl.BlockSpec((SSD_BATCH, SSD_CHUNK, width), lambda b, c: (b, c, 0))

    return pl.pallas_call(
        _ssd_kernel,
        grid=(batch // SSD_BATCH, seq // SSD_CHUNK),
        in_specs=[tile(SSM_WIDTH), tile(BC_WIDTH), tile(BC_WIDTH), tile(LANES), tile(SSM_WIDTH),
                  _resident((1, LANES)), _resident((1, SSM_WIDTH)), _resident((1, SSM_WIDTH)),
                  _resident((SSD_CHUNK, SSD_CHUNK)), _resident((2 * LANES, SSM_WIDTH))],
        out_specs=tile(SSM_WIDTH),
        out_shape=jax.ShapeDtypeStruct((batch, seq, SSM_WIDTH), BF16),
        scratch_shapes=[pltpu.VMEM((SSD_BATCH, SSM_STATE, SSM_WIDTH), F32)],
        compiler_params=pltpu.CompilerParams(dimension_semantics=("parallel", "arbitrary"),
                                             vmem_limit_bytes=56 * MIB),
        name="ssd",
    )(xs, bm, cm, dt, z, arow, dskip, norm, tri, expand)


def _out_ffn_kernel(x_ref, attn_ref, y_ref, wo_attn_ref, wo_ssm_ref, mixpost_ref, pre_ref, wg_ref, wu_ref, wd_ref,
                    post_ref, o_ref):
    def mixer_residual(t):
        rows = slice(t * FFN_ROWS, (t + 1) * FFN_ROWS)
        attn = jnp.concatenate([attn_ref[0, j, rows, :] for j in range(Q_SLABS)], axis=1)
        mixed = (jnp.dot(attn, wo_attn_ref[...], preferred_element_type=F32)
                 + jnp.dot(y_ref[0, rows, :], wo_ssm_ref[...], preferred_element_type=F32))
        return x_ref[0, rows, :] + _rmsnorm(mixed, mixpost_ref[...])

    def emit(t, out):
        o_ref[0, t * FFN_ROWS:(t + 1) * FFN_ROWS, :] = out

    tiles = [functools.partial(mixer_residual, t) for t in range(OUT_FFN_TILES_PER_STEP)]
    _swiglu_half_steps(tiles, pre_ref[...], wg_ref, wu_ref, wd_ref, post_ref[...], emit)


def _out_ffn(x3d, attn, y3d, wo_attn, w_out, mixpost, pre, wg, wu, wd, post):
    batch, seq, _ = x3d.shape
    rows = FFN_ROWS * OUT_FFN_TILES_PER_STEP
    row_spec = pl.BlockSpec((1, rows, D_MODEL), lambda b, j: (b, j, 0))
    slab_spec = pl.BlockSpec((1, Q_SLABS, rows, LANES), lambda b, j: (b, 0, j, 0))
    assert ATTN_WIDTH % SSM_WIDTH == 0
    ssm_rows = pl.BlockSpec((SSM_WIDTH, D_MODEL), lambda *_: (ATTN_WIDTH // SSM_WIDTH, 0),
                            pipeline_mode=pl.Buffered(1))
    return pl.pallas_call(
        _out_ffn_kernel,
        grid=(batch, seq // rows),
        in_specs=[row_spec, slab_spec, row_spec, _resident((ATTN_WIDTH, D_MODEL)), ssm_rows,
                  _resident((1, D_MODEL)), _resident((1, D_MODEL)), _resident((D_MODEL, D_FF)),
                  _resident((D_MODEL, D_FF)), _resident((D_FF, D_MODEL)), _resident((1, D_MODEL))],
        out_specs=row_spec,
        out_shape=jax.ShapeDtypeStruct((batch, seq, D_MODEL), F32),
        compiler_params=pltpu.CompilerParams(dimension_semantics=("parallel", "parallel"),
                                             vmem_limit_bytes=58 * MIB),
        name="out_ffn",
    )(x3d, attn, y3d, wo_attn, w_out, mixpost, pre, wg, wu, wd, post)


def _to_slab_head_order(w, axis):
    shape = w.shape
    split = shape[:axis] + (KV_PAIRS, 2, Q_PER_KV, HEAD_DIM) + shape[axis + 1:]
    return jnp.swapaxes(w.reshape(split), axis + 1, axis + 2).reshape(shape)


def _pad_lanes(v):
    return jnp.pad(v, [(0, 0)] * (v.ndim - 1) + [(0, LANES - v.shape[-1])])


def _layer(x, pos3d, invf, tri, expand, p):
    batch, seq, _ = x.shape
    tokens = batch * seq
    row = lambda v: v.reshape(1, -1)
    bf = lambda w: w.astype(BF16)

    x1 = _ffn(x.reshape(tokens, D_MODEL), row(p["ffn1_pre_norm"]), bf(p["ffn1_w_gate"]), bf(p["ffn1_w_up"]),
              bf(p["ffn1_w_down"]), row(p["ffn1_post_norm"])).reshape(batch, seq, D_MODEL)

    w_in = bf(p["w_in"])
    o_dt = ATTN_WIDTH + 2 * KV_WIDTH + CONV_CHANNELS + SSM_WIDTH
    q, k, v, xs, bm, cm, z, dt = _in_proj(
        x1, pos3d, invf, row(p["mix_pre_norm"]),
        _to_slab_head_order(w_in[:, :ATTN_WIDTH], 1), w_in, _pad_lanes(w_in[:, o_dt:]),
        p["conv_w"], row(p["conv_b"]), _pad_lanes(row(p["dt_bias"])))

    attn = _attention(q, k, v)

    arow = _pad_lanes(row(-jnp.exp(p["a_log"]) * LOG2_E))
    dskip = row(jnp.repeat(p["d_skip"], SSM_HEAD_DIM))
    y = _ssd(xs, bm, cm, dt, z, arow, dskip, row(p["ssm_norm"]), tri, expand)

    w_out = bf(p["w_out"])
    return _out_ffn(x1, attn, y, _to_slab_head_order(w_out[:ATTN_WIDTH], 0), w_out,
                    row(p["mix_post_norm"]), row(p["ffn2_pre_norm"]),
                    bf(p["ffn2_w_gate"]), bf(p["ffn2_w_up"]), bf(p["ffn2_w_down"]), row(p["ffn2_post_norm"]))


def kernel(x, positions, ffn1_pre_norm, ffn1_w_gate, ffn1_w_up, ffn1_w_down, ffn1_post_norm, mix_pre_norm, w_in, conv_w, conv_b, dt_bias, a_log, d_skip, ssm_norm, w_out, mix_post_norm, ffn2_pre_norm, ffn2_w_gate, ffn2_w_up, ffn2_w_down, ffn2_post_norm):
    params = dict(ffn1_pre_norm=ffn1_pre_norm, ffn1_w_gate=ffn1_w_gate, ffn1_w_up=ffn1_w_up,
                  ffn1_w_down=ffn1_w_down, ffn1_post_norm=ffn1_post_norm, mix_pre_norm=mix_pre_norm, w_in=w_in,
                  conv_w=conv_w, conv_b=conv_b, dt_bias=dt_bias, a_log=a_log, d_skip=d_skip, ssm_norm=ssm_norm,
                  w_out=w_out, mix_post_norm=mix_post_norm, ffn2_pre_norm=ffn2_pre_norm, ffn2_w_gate=ffn2_w_gate,
                  ffn2_w_up=ffn2_w_up, ffn2_w_down=ffn2_w_down, ffn2_post_norm=ffn2_post_norm)
    depth = w_in.shape[0]
    batch, seq, _ = x.shape
    inv_freq = ROPE_THETA ** (-jnp.arange(0, HEAD_DIM, 2, dtype=F32) / HEAD_DIM)
    invf = inv_freq.reshape(HEAD_DIM // 2, 1)
    pos3d = positions.reshape(batch, 1, seq)
    idx = jnp.arange(SSD_CHUNK)
    tri = (idx[:, None] >= idx[None, :]).astype(BF16)
    expand = (jnp.arange(2 * LANES)[:, None] % LANES == jnp.arange(SSM_WIDTH)[None, :] // SSM_HEAD_DIM).astype(BF16)
    for i in range(depth):
        x = _layer(x, pos3d, invf, tri, expand, {name: w[i] for name, w in params.items()})
    return x
```

```python
import functools
import math

import jax
import jax.numpy as jnp
from jax import lax
from jax.experimental import pallas as pl
from jax.experimental.pallas import tpu as pltpu

F32 = jnp.float32
BF16 = jnp.bfloat16

D_MODEL = 1024
D_FF = 2816
HEAD_DIM = 64
N_Q_HEADS = 16
N_KV_HEADS = 4
Q_PER_KV = N_Q_HEADS // N_KV_HEADS
ATTN_WIDTH = N_Q_HEADS * HEAD_DIM
KV_WIDTH = N_KV_HEADS * HEAD_DIM
ATTN_BLOCK = 128
ROPE_THETA = 10000.0
SSM_HEADS = 16
SSM_HEAD_DIM = 64
SSM_WIDTH = SSM_HEADS * SSM_HEAD_DIM
SSM_STATE = 128
SSM_GROUPS = 2
GROUP_WIDTH = SSM_WIDTH // SSM_GROUPS
BC_WIDTH = SSM_GROUPS * SSM_STATE
CONV_WIDTH = 4
CONV_CHANNELS = SSM_WIDTH + 2 * BC_WIDTH
SSD_CHUNK = 128
MACARON_WEIGHT = 0.5
NORM_EPS = 1e-6
LOG2_E = math.log2(math.e)

LANES = 128
CARRY_ROWS = 8
MIB = 1024 * 1024

FFN_ROWS = 512
FFN_TILES_PER_STEP = 2
OUT_FFN_TILES_PER_STEP = 2
FF_CHUNK = 256
assert D_FF % FF_CHUNK == 0
PROJ_ROWS = 512
SSD_BATCH = 4

Q_SLABS = ATTN_WIDTH // LANES
KV_SLABS = KV_WIDTH // LANES
KV_PAIRS = N_KV_HEADS // 2


def _rmsnorm(x, gain):
    return x * lax.rsqrt(jnp.mean(x * x, axis=-1, keepdims=True) + NORM_EPS) * gain


def _resident(shape):
    return pl.BlockSpec(shape, lambda *_: (0,) * len(shape), pipeline_mode=pl.Buffered(1))


def _swiglu_half_steps(load_tiles, pre, wg_ref, wu_ref, wd_ref, post, emit):
    finish = None
    x = load_tiles[0]()
    xn = _rmsnorm(x, pre).astype(BF16)
    for t in range(len(load_tiles)):
        acc = jnp.zeros(x.shape, F32)
        x_next = xn_next = None
        for c in range(D_FF // FF_CHUNK):
            cols = slice(c * FF_CHUNK, (c + 1) * FF_CHUNK)
            gate = jnp.dot(xn, wg_ref[:, cols], preferred_element_type=F32)
            up = jnp.dot(xn, wu_ref[:, cols], preferred_element_type=F32)
            hidden = (jax.nn.silu(gate) * up).astype(BF16)
            acc = acc + jnp.dot(hidden, wd_ref[cols, :], preferred_element_type=F32)
            if c == 1 and finish is not None:
                finish()
            if c == 2 and t + 1 < len(load_tiles):
                x_next = load_tiles[t + 1]()
                xn_next = _rmsnorm(x_next, pre).astype(BF16)
        finish = functools.partial(lambda t_, x_t, acc_t: emit(t_, x_t + MACARON_WEIGHT * _rmsnorm(acc_t, post)),
                                   t, x, acc)
        x, xn = x_next, xn_next
    finish()


def _ffn_kernel(x_ref, pre_ref, wg_ref, wu_ref, wd_ref, post_ref, o_ref):
    tiles = [functools.partial(lambda t: x_ref[t * FFN_ROWS:(t + 1) * FFN_ROWS, :], t)
             for t in range(FFN_TILES_PER_STEP)]

    def emit(t, out):
        o_ref[t * FFN_ROWS:(t + 1) * FFN_ROWS, :] = out

    _swiglu_half_steps(tiles, pre_ref[...], wg_ref, wu_ref, wd_ref, post_ref[...], emit)


def _ffn(x2d, pre, wg, wu, wd, post):
    tokens = x2d.shape[0]
    step_rows = FFN_ROWS * FFN_TILES_PER_STEP
    row_spec = pl.BlockSpec((step_rows, D_MODEL), lambda i: (i, 0))
    return pl.pallas_call(
        _ffn_kernel,
        grid=(tokens // step_rows,),
        in_specs=[row_spec, _resident((1, D_MODEL)), _resident((D_MODEL, D_FF)), _resident((D_MODEL, D_FF)),
                  _resident((D_FF, D_MODEL)), _resident((1, D_MODEL))],
        out_specs=row_spec,
        out_shape=jax.ShapeDtypeStruct((tokens, D_MODEL), F32),
        compiler_params=pltpu.CompilerParams(dimension_semantics=("parallel",), vmem_limit_bytes=48 * MIB),
        name="ffn",
    )(x2d, pre, wg, wu, wd, post)


def _softplus(x):
    return jnp.maximum(x, 0.0) + jnp.log1p(jnp.exp(-jnp.abs(x)))


def _inproj_kernel(x_ref, pos_ref, invf_ref, gain_ref, wq_ref, wk_ref, wv_ref, wx_ref, wz_ref, wdt_ref,
                   convw_ref, convb_ref, dtb_ref,
                   q_ref, k_ref, v_ref, xs_ref, b_ref, c_ref, z_ref, dt_ref, ubuf_ref):
    rows = x_ref.shape[1]
    xn = _rmsnorm(x_ref[0], gain_ref[...]).astype(BF16)
    wide = 2 * LANES

    def project(w_ref, c):
        return jnp.dot(xn, w_ref[:, c * wide:(c + 1) * wide], preferred_element_type=F32)

    n_slabs = CONV_CHANNELS // LANES

    @pl.when(pl.program_id(1) == 0)
    def _():
        ubuf_ref[:, 0:CARRY_ROWS, :] = jnp.zeros((n_slabs, CARRY_ROWS, LANES), F32)

    def conv_chunk(c, u):
        for j in (2 * c, 2 * c + 1):
            cols = slice(j * LANES, (j + 1) * LANES)
            ubuf_ref[j, CARRY_ROWS:CARRY_ROWS + rows, :] = u[:, (j % 2) * LANES:(j % 2 + 1) * LANES]
            conv = (convb_ref[:, cols]
                    + ubuf_ref[j, CARRY_ROWS:CARRY_ROWS + rows, :] * convw_ref[CONV_WIDTH - 1:CONV_WIDTH, cols])
            for tap in range(CONV_WIDTH - 1):
                start = CARRY_ROWS - (CONV_WIDTH - 1) + tap
                conv = conv + ubuf_ref[j, start:start + rows, :] * convw_ref[tap:tap + 1, cols]
            act = jax.nn.silu(conv)
            if j < SSM_WIDTH // LANES:
                xs_ref[0, :, cols] = act
            elif j < (SSM_WIDTH + BC_WIDTH) // LANES:
                b_ref[0, :, j * LANES - SSM_WIDTH:(j + 1) * LANES - SSM_WIDTH] = act.astype(BF16)
            else:
                first = SSM_WIDTH + BC_WIDTH
                c_ref[0, :, j * LANES - first:(j + 1) * LANES - first] = act.astype(BF16)
            ubuf_ref[j, 0:CARRY_ROWS, :] = ubuf_ref[j, rows:rows + CARRY_ROWS, :]

    def z_chunk(c):
        z_ref[0, :, c * wide:(c + 1) * wide] = project(wz_ref, c)

    u = project(wx_ref, 0)
    z_chunk(0)

    ang_t = invf_ref[...] * pos_ref[0].astype(F32)
    cos_t = jnp.cos(ang_t)
    sin_t = jnp.sin(ang_t)
    copies = LANES // HEAD_DIM
    cos = jnp.concatenate([cos_t, cos_t] * copies, axis=0).T
    sin = jnp.concatenate([-sin_t, sin_t] * copies, axis=0).T
    lane = lax.broadcasted_iota(jnp.int32, (rows, LANES), 1)
    first_half = (lane % HEAD_DIM) < (HEAD_DIM // 2)

    def rope(t):
        back = pltpu.roll(t, HEAD_DIM // 2, 1)
        fwd = pltpu.roll(t, LANES - HEAD_DIM // 2, 1)
        return t * cos + jnp.where(first_half, fwd, back) * sin

    def q_chunk(c):
        q = project(wq_ref, c)
        for half in range(2):
            q_ref[0, 2 * c + half] = rope(q[:, half * LANES:(half + 1) * LANES]) * (HEAD_DIM ** -0.5 * LOG2_E)

    def kv_chunk(c):
        k = project(wk_ref, c)
        v = project(wv_ref, c)
        for half in range(2):
            k_ref[0, 2 * c + half] = rope(k[:, half * LANES:(half + 1) * LANES])
            v_ref[0, 2 * c + half] = v[:, half * LANES:(half + 1) * LANES]

    dt_ref[0] = _softplus(jnp.dot(xn, wdt_ref[...], preferred_element_type=F32) + dtb_ref[...])
    n_conv = CONV_CHANNELS // wide
    fillers = ([functools.partial(q_chunk, c) for c in range(ATTN_WIDTH // wide)]
               + [functools.partial(kv_chunk, c) for c in range(KV_WIDTH // wide)]
               + [functools.partial(z_chunk, c) for c in range(1, SSM_WIDTH // wide)])
    for c in range(n_conv):
        u_next = project(wx_ref, c + 1) if c + 1 < n_conv else None
        conv_chunk(c, u)
        fillers.pop(0)()
        u = u_next
    for filler in fillers:
        filler()


def _in_proj(x3d, pos3d, invf, gain, wq, w_in, wdt, convw, convb, dtb):
    batch, seq, _ = x3d.shape
    rows = PROJ_ROWS

    def columns(start, width):
        assert start % width == 0
        return pl.BlockSpec((D_MODEL, width), lambda *_: (0, start // width), pipeline_mode=pl.Buffered(1))

    o_k = ATTN_WIDTH
    o_v = o_k + KV_WIDTH
    o_x = o_v + KV_WIDTH
    o_z = o_x + CONV_CHANNELS

    def tile(width):
        return pl.BlockSpec((1, rows, width), lambda b, j: (b, j, 0))

    def slabs(n):
        return pl.BlockSpec((1, n, rows, LANES), lambda b, j: (b, 0, j, 0))

    def out(width, dtype):
        return jax.ShapeDtypeStruct((batch, seq, width), dtype)

    def out_slabs(n):
        return jax.ShapeDtypeStruct((batch, n, seq, LANES), F32)

    return pl.pallas_call(
        _inproj_kernel,
        grid=(batch, seq // rows),
        in_specs=[tile(D_MODEL), pl.BlockSpec((1, 1, rows), lambda b, j: (b, 0, j)),
                  _resident((HEAD_DIM // 2, 1)), _resident((1, D_MODEL)),
                  _resident((D_MODEL, ATTN_WIDTH)), columns(o_k, KV_WIDTH), columns(o_v, KV_WIDTH),
                  columns(o_x, CONV_CHANNELS), columns(o_z, SSM_WIDTH),
                  _resident((D_MODEL, LANES)), _resident((CONV_WIDTH, CONV_CHANNELS)),
                  _resident((1, CONV_CHANNELS)), _resident((1, LANES))],
        out_specs=[slabs(Q_SLABS), slabs(KV_SLABS), slabs(KV_SLABS), tile(SSM_WIDTH), tile(BC_WIDTH),
                   tile(BC_WIDTH), tile(SSM_WIDTH), tile(LANES)],
        out_shape=[out_slabs(Q_SLABS), out_slabs(KV_SLABS), out_slabs(KV_SLABS), out(SSM_WIDTH, F32),
                   out(BC_WIDTH, BF16), out(BC_WIDTH, BF16), out(SSM_WIDTH, F32), out(LANES, F32)],
        scratch_shapes=[pltpu.VMEM((CONV_CHANNELS // LANES, rows + CARRY_ROWS, LANES), F32)],
        compiler_params=pltpu.CompilerParams(dimension_semantics=("parallel", "arbitrary"),
                                             vmem_limit_bytes=56 * MIB),
        name="in_proj",
    )(x3d, pos3d, invf, gain, wq, w_in, w_in, w_in, w_in, wdt, convw, convb, dtb)


STATE_ARRAYS = 3
MASKED = -2.0 ** 100
STATE_U, STATE_M, STATE_L = range(STATE_ARRAYS)


def _attn_blocks(blocks):
    blk = ATTN_BLOCK
    low = lax.broadcasted_iota(jnp.int32, (blk, LANES), 1) < HEAD_DIM
    zero = jnp.zeros((), BF16)

    scores = []
    for q_slabs, k_blk, _, (query_onehot, key_mask), _ in blocks:
        n_keys = k_blk.shape[0]
        kb = k_blk.astype(BF16)
        key_low = lax.broadcasted_iota(jnp.int32, kb.shape, 1) < HEAD_DIM
        keys = jnp.concatenate([jnp.concatenate([jnp.where(key_low, kb, zero), key_mask], axis=1),
                                jnp.concatenate([jnp.where(key_low, zero, kb), key_mask], axis=1)], axis=0)
        lhs = jnp.concatenate([jnp.concatenate([q.astype(BF16) for q in q_slabs], axis=0), query_onehot], axis=1)
        s = lax.dot_general(lhs, keys, (((1,), (1,)), ((), ())), preferred_element_type=F32)
        scores += [s[:, :n_keys], s[:, n_keys:]]
    maxes = [jnp.max(s, axis=-1, keepdims=True) for s in scores]
    probs = [jnp.exp2((s - m).astype(BF16)) for s, m in zip(scores, maxes)]
    results = []
    for i, (_, _, v_blk, _, _) in enumerate(blocks):
        vb = v_blk.astype(BF16)
        key_low = lax.broadcasted_iota(jnp.int32, vb.shape, 1) < HEAD_DIM
        ones_low = jnp.where(key_low, 1.0, 0.0).astype(BF16)
        ones_high = jnp.where(key_low, 0.0, 1.0).astype(BF16)
        rhs = jnp.concatenate([
            jnp.concatenate([jnp.where(key_low, vb, zero), ones_low], axis=1),
            jnp.concatenate([jnp.where(key_low, zero, vb), ones_high], axis=1)], axis=0)
        lhs = jnp.concatenate([probs[2 * i], probs[2 * i + 1]], axis=1)
        results.append(jnp.dot(lhs, rhs, preferred_element_type=F32))
    states = []
    for i, (q_slabs, _, _, _, load_old) in enumerate(blocks):
        result, m_a, m_b = results[i], maxes[2 * i], maxes[2 * i + 1]
        old = None if load_old is None else load_old()
        state = []
        for g in range(len(q_slabs)):
            rows = slice(g * blk, (g + 1) * blk)
            u_new = result[rows, :LANES]
            l_new = result[rows, LANES:]
            m_new = jnp.where(low, m_a[rows], m_b[rows])
            if old is not None:
                u_run, m_run, l_run = old[g]
                m_tot = jnp.maximum(m_run, m_new)
                a = jnp.exp2(m_run - m_tot)
                b = jnp.exp2(m_new - m_tot)
                u_new = u_run * a + u_new * b
                l_new = l_run * a + l_new * b
                m_new = m_tot
            state.append((u_new, m_new, l_new))
        states.append(state)
    return states


def _attn_kernel(q_ref, k_ref, v_ref, o_ref, st4_ref, stt_ref):
    blk = ATTN_BLOCK
    seq = q_ref.shape[2]
    n_slabs = q_ref.shape[1]
    sub4 = seq // 4

    def key_mask(n_keys, shift):
        si = lax.broadcasted_iota(jnp.int32, (n_keys, blk), 0)
        qi = lax.broadcasted_iota(jnp.int32, (n_keys, blk), 1) + shift
        return jnp.where((si <= qi) & (si >= qi - blk), 0.0, MASKED).astype(BF16)

    stacked = n_slabs * blk
    query_onehot = (lax.broadcasted_iota(jnp.int32, (stacked, blk), 0) % blk
                    == lax.broadcasted_iota(jnp.int32, (stacked, blk), 1)).astype(BF16)
    band_bias = (query_onehot, key_mask(2 * blk, blk))
    causal_bias = (query_onehot, key_mask(blk, 0))

    def load_q(rows):
        return [q_ref[0, g, rows, :] for g in range(n_slabs)]

    def load_state(ref, rows):
        return [tuple(ref[a * n_slabs + g, rows, :] for a in range(STATE_ARRAYS)) for g in range(n_slabs)]

    def store_state(ref, rows, state):
        for g in range(n_slabs):
            for a in range(STATE_ARRAYS):
                ref[a * n_slabs + g, rows, :] = state[g][a]

    def store_out(rows, state):
        for g in range(n_slabs):
            o_ref[0, g, rows, :] = (state[g][STATE_U] / state[g][STATE_L]).astype(o_ref.dtype)

    def run(plans):
        states = _attn_blocks([(load_q(rows), k_ref[0, 0, keys, :], v_ref[0, 0, keys, :], bias, load_old)
                               for rows, keys, bias, load_old, _ in plans])
        for (rows, _, _, _, store), state in zip(plans, states):
            store(rows, state)

    def paired(n_blocks, plan, per_trip=2):
        def trip(i, carry):
            run([plan(per_trip * i + j) for j in range(per_trip)])
            return carry
        lax.fori_loop(0, n_blocks // per_trip, trip, 0)

    def d16_plan(r16):
        rows = pl.ds(r16, blk, stride=16)
        dest = pl.ds((r16 % 4) * sub4 + r16 // 4, blk, stride=4)
        return rows, rows, causal_bias, None, lambda _, state: store_state(st4_ref, dest, state)

    paired(16, d16_plan, per_trip=8)

    def to_token_order(rows, state):
        store_state(stt_ref, rows, state)

    def d4_first_plan(r4):
        rows = pl.ds(r4, blk, stride=4)
        src = pl.ds(pl.multiple_of(r4 * sub4, blk), blk)
        return rows, rows, causal_bias, lambda: load_state(st4_ref, src), to_token_order

    paired(4, d4_first_plan, per_trip=4)
    later_blocks = sub4 // blk - 1

    def d4_plan(idx):
        r4 = idx // later_blocks
        n = 1 + idx % later_blocks
        rows = pl.ds(r4 + 4 * blk * n, blk, stride=4)
        keys = pl.ds(r4 + 4 * blk * (n - 1), 2 * blk, stride=4)
        src = pl.ds(pl.multiple_of(r4 * sub4 + n * blk, blk), blk)
        return rows, keys, band_bias, lambda: load_state(st4_ref, src), to_token_order

    paired(4 * later_blocks, d4_plan, per_trip=3)

    def d1_plan(n):
        start = n * blk if isinstance(n, int) else pl.multiple_of(n * blk, blk)
        rows = pl.ds(start, blk)
        if isinstance(n, int) and n == 0:
            return rows, rows, causal_bias, lambda: load_state(stt_ref, rows), store_out
        return rows, pl.ds(start - blk, 2 * blk), band_bias, lambda: load_state(stt_ref, rows), store_out

    last = seq // blk - 1
    run([d1_plan(0)])
    paired(last, lambda i: d1_plan(i + 1), per_trip=3)


def _attention(q, k, v):
    batch, _, seq, _ = q.shape
    slabs_per_pair = Q_SLABS // KV_PAIRS
    q_spec = pl.BlockSpec((1, slabs_per_pair, seq, LANES), lambda b, p: (b, p, 0, 0))
    kv_spec = pl.BlockSpec((1, 1, seq, LANES), lambda b, p: (b, p, 0, 0))
    state_shape = (STATE_ARRAYS * slabs_per_pair, seq, LANES)
    return pl.pallas_call(
        _attn_kernel,
        grid=(batch, KV_PAIRS),
        in_specs=[q_spec, kv_spec, kv_spec],
        out_specs=q_spec,
        out_shape=jax.ShapeDtypeStruct(q.shape, BF16),
        scratch_shapes=[pltpu.VMEM(state_shape, F32), pltpu.VMEM(state_shape, F32)],
        compiler_params=pltpu.CompilerParams(dimension_semantics=("parallel", "parallel"),
                                             vmem_limit_bytes=56 * MIB),
        name="attention",
    )(q, k, v)


def _split_bf16(x, pieces):
    parts = []
    for _ in range(pieces):
        part = x.astype(BF16)
        parts.append(part)
        x = x - part.astype(F32)
    return parts


def _ssd_kernel(xs_ref, b_ref, c_ref, dt_ref, z_ref, arow_ref, dskip_ref, norm_ref, tri_ref, expand_ref,
                y_ref, state_ref):
    chunk = SSD_CHUNK
    lanes = range(xs_ref.shape[0])
    heads_per_group = SSM_HEADS // SSM_GROUPS

    @pl.when(pl.program_id(1) == 0)
    def _():
        state_ref[...] = jnp.zeros(state_ref.shape, F32)

    row = lax.broadcasted_iota(jnp.int32, (chunk, chunk), 0)
    col = lax.broadcasted_iota(jnp.int32, (chunk, chunk), 1)
    causal = row >= col
    low = lax.broadcasted_iota(jnp.int32, (chunk, LANES), 1) < SSM_HEAD_DIM
    zero = jnp.zeros((), BF16)

    def group_cols(g):
        return slice(g * SSM_STATE, (g + 1) * SSM_STATE), slice(g * GROUP_WIDTH, (g + 1) * GROUP_WIDTH)

    dts = [dt_ref[i] for i in lanes]
    a3s = [jnp.dot(tri_ref[...], jnp.concatenate(_split_bf16(dt * arow_ref[...], 3), axis=1),
                   preferred_element_type=F32) for dt in dts]
    a_css = [a3[:, :LANES] + a3[:, LANES:2 * LANES] + a3[:, 2 * LANES:] for a3 in a3s]

    def expand(stat):
        return jnp.dot(jnp.concatenate(_split_bf16(stat, 2), axis=1), expand_ref[...], preferred_element_type=F32)

    w_states = [expand(jnp.exp2(a_cs[chunk - 1:chunk, :] - a_cs) * dt) for a_cs, dt in zip(a_css, dts)]
    scale_offs = [expand(jnp.exp2(a_cs)) for a_cs in a_css]
    source_ts = [(a_cs - jnp.log(dt) * LOG2_E).T for a_cs, dt in zip(a_css, dts)]

    cbs, y_offs, new_states = {}, {}, {}
    for i in lanes:
        xw_bf = (xs_ref[i] * w_states[i]).astype(BF16)
        for g in range(SSM_GROUPS):
            gcols, wide = group_cols(g)
            b_g = b_ref[i, :, gcols]
            c_g = c_ref[i, :, gcols]
            cbs[i, g] = lax.dot_general(c_g, b_g, (((1,), (1,)), ((), ())), preferred_element_type=F32)
            y_offs[i, g] = jnp.dot(c_g, state_ref[i, :, wide].astype(BF16), preferred_element_type=F32)
            b_t = b_g.astype(F32).T.astype(BF16)
            new_states[i, g] = jnp.dot(b_t, xw_bf[:, wide], preferred_element_type=F32)

    y_diags = {}
    for i in lanes:
        xs_bf = xs_ref[i].astype(BF16)
        for g in range(SSM_GROUPS):
            parts = []
            for pair in range(heads_per_group // 2):
                mats = []
                for e in range(2):
                    h = g * heads_per_group + 2 * pair + e
                    seg = a_css[i][:, h:h + 1] - source_ts[i][h:h + 1, :]
                    mats.append((cbs[i, g] * jnp.exp2(jnp.where(causal, seg, -jnp.inf))).astype(BF16))
                first = (g * heads_per_group + 2 * pair) * SSM_HEAD_DIM
                x_pair = xs_bf[:, first:first + 2 * SSM_HEAD_DIM]
                rhs = jnp.concatenate([jnp.where(low, x_pair, zero), jnp.where(low, zero, x_pair)], axis=0)
                parts.append(jnp.dot(jnp.concatenate(mats, axis=1), rhs, preferred_element_type=F32))
            y_diags[i, g] = jnp.concatenate(parts, axis=1)

    for i in lanes:
        chunk_decay = scale_offs[i][chunk - 1:chunk, :]
        y_parts = []
        for g in range(SSM_GROUPS):
            _, wide = group_cols(g)
            y_parts.append(y_diags[i, g] + y_offs[i, g] * scale_offs[i][:, wide])
            state_ref[i, :, wide] = state_ref[i, :, wide] * chunk_decay[:, wide] + new_states[i, g]
        y = jnp.concatenate(y_parts, axis=1) + dskip_ref[...] * xs_ref[i]
        y = y * jax.nn.silu(z_ref[i])
        normed = []
        for g in range(SSM_GROUPS):
            y_g = y[:, g * GROUP_WIDTH:(g + 1) * GROUP_WIDTH]
            normed.append(y_g * lax.rsqrt(jnp.mean(y_g * y_g, axis=-1, keepdims=True) + NORM_EPS))
        y_ref[i] = (jnp.concatenate(normed, axis=1) * norm_ref[...]).astype(y_ref.dtype)


def _ssd(xs, bm, cm, dt, z, arow, dskip, norm, tri, expand):
    batch, seq, _ = xs.shape

    def tile(width):
        return pl.BlockSpec((SSD_BATCH, SSD_CHUNK, width), lambda b, c: (b, c, 0))

    return pl.pallas_call(
        _ssd_kernel,
        grid=(batch // SSD_BATCH, seq // SSD_CHUNK),
        in_specs=[tile(SSM_WIDTH), tile(BC_WIDTH), tile(BC_WIDTH), tile(LANES), tile(SSM_WIDTH),
                  _resident((1, LANES)), _resident((1, SSM_WIDTH)), _resident((1, SSM_WIDTH)),
                  _resident((SSD_CHUNK, SSD_CHUNK)), _resident((2 * LANES, SSM_WIDTH))],
        out_specs=tile(SSM_WIDTH),
        out_shape=jax.ShapeDtypeStruct((batch, seq, SSM_WIDTH), BF16),
        scratch_shapes=[pltpu.VMEM((SSD_BATCH, SSM_STATE, SSM_WIDTH), F32)],
        compiler_params=pltpu.CompilerParams(dimension_semantics=("parallel", "arbitrary"),
                                             vmem_limit_bytes=56 * MIB),
        name="ssd",
    )(xs, bm, cm, dt, z, arow, dskip, norm, tri, expand)


def _out_ffn_kernel(x_ref, attn_ref, y_ref, wo_attn_ref, wo_ssm_ref, mixpost_ref, pre_ref, wg_ref, wu_ref, wd_ref,
                    post_ref, o_ref):
    def mixer_residual(t):
        rows = slice(t * FFN_ROWS, (t + 1) * FFN_ROWS)
        attn = jnp.concatenate([attn_ref[0, j, rows, :] for j in range(Q_SLABS)], axis=1)
        mixed = (jnp.dot(attn, wo_attn_ref[...], preferred_element_type=F32)
                 + jnp.dot(y_ref[0, rows, :], wo_ssm_ref[...], preferred_element_type=F32))
        return x_ref[0, rows, :] + _rmsnorm(mixed, mixpost_ref[...])

    def emit(t, out):
        o_ref[0, t * FFN_ROWS:(t + 1) * FFN_ROWS, :] = out

    tiles = [functools.partial(mixer_residual, t) for t in range(OUT_FFN_TILES_PER_STEP)]
    _swiglu_half_steps(tiles, pre_ref[...], wg_ref, wu_ref, wd_ref, post_ref[...], emit)


def _out_ffn(x3d, attn, y3d, wo_attn, w_out, mixpost, pre, wg, wu, wd, post):
    batch, seq, _ = x3d.shape
    rows = FFN_ROWS * OUT_FFN_TILES_PER_STEP
    row_spec = pl.BlockSpec((1, rows, D_MODEL), lambda b, j: (b, j, 0))
    slab_spec = pl.BlockSpec((1, Q_SLABS, rows, LANES), lambda b, j: (b, 0, j, 0))
    assert ATTN_WIDTH % SSM_WIDTH == 0
    ssm_rows = pl.BlockSpec((SSM_WIDTH, D_MODEL), lambda *_: (ATTN_WIDTH // SSM_WIDTH, 0),
                            pipeline_mode=pl.Buffered(1))
    return pl.pallas_call(
        _out_ffn_kernel,
        grid=(batch, seq // rows),
        in_specs=[row_spec, slab_spec, row_spec, _resident((ATTN_WIDTH, D_MODEL)), ssm_rows,
                  _resident((1, D_MODEL)), _resident((1, D_MODEL)), _resident((D_MODEL, D_FF)),
                  _resident((D_MODEL, D_FF)), _resident((D_FF, D_MODEL)), _resident((1, D_MODEL))],
        out_specs=row_spec,
        out_shape=jax.ShapeDtypeStruct((batch, seq, D_MODEL), F32),
        compiler_params=pltpu.CompilerParams(dimension_semantics=("parallel", "parallel"),
                                             vmem_limit_bytes=58 * MIB),
        name="out_ffn",
    )(x3d, attn, y3d, wo_attn, w_out, mixpost, pre, wg, wu, wd, post)


def _to_slab_head_order(w, axis):
    shape = w.shape
    split = shape[:axis] + (KV_PAIRS, 2, Q_PER_KV, HEAD_DIM) + shape[axis + 1:]
    return jnp.swapaxes(w.reshape(split), axis + 1, axis + 2).reshape(shape)


def _pad_lanes(v):
    return jnp.pad(v, [(0, 0)] * (v.ndim - 1) + [(0, LANES - v.shape[-1])])


def _layer(x, pos3d, invf, tri, expand, p):
    batch, seq, _ = x.shape
    tokens = batch * seq
    row = lambda v: v.reshape(1, -1)
    bf = lambda w: w.astype(BF16)

    x1 = _ffn(x.reshape(tokens, D_MODEL), row(p["ffn1_pre_norm"]), bf(p["ffn1_w_gate"]), bf(p["ffn1_w_up"]),
              bf(p["ffn1_w_down"]), row(p["ffn1_post_norm"])).reshape(batch, seq, D_MODEL)

    o_dt = ATTN_WIDTH + 2 * KV_WIDTH + CONV_CHANNELS + SSM_WIDTH
    w_in = bf(p["w_in"][:, :o_dt])
    q, k, v, xs, bm, cm, z, dt = _in_proj(
        x1, pos3d, invf, row(p["mix_pre_norm"]),
        _to_slab_head_order(w_in[:, :ATTN_WIDTH], 1), w_in, bf(_pad_lanes(p["w_in"][:, o_dt:])),
        p["conv_w"], row(p["conv_b"]), _pad_lanes(row(p["dt_bias"])))

    attn = _attention(q, k, v)

    arow = _pad_lanes(row(-jnp.exp(p["a_log"]) * LOG2_E))
    dskip = row(jnp.repeat(p["d_skip"], SSM_HEAD_DIM))
    y = _ssd(xs, bm, cm, dt, z, arow, dskip, row(p["ssm_norm"]), tri, expand)

    w_out = bf(p["w_out"])
    return _out_ffn(x1, attn, y, _to_slab_head_order(w_out[:ATTN_WIDTH], 0), w_out,
                    row(p["mix_post_norm"]), row(p["ffn2_pre_norm"]),
                    bf(p["ffn2_w_gate"]), bf(p["ffn2_w_up"]), bf(p["ffn2_w_down"]), row(p["ffn2_post_norm"]))


def kernel(x, positions, ffn1_pre_norm, ffn1_w_gate, ffn1_w_up, ffn1_w_down, ffn1_post_norm, mix_pre_norm, w_in, conv_w, conv_b, dt_bias, a_log, d_skip, ssm_norm, w_out, mix_post_norm, ffn2_pre_norm, ffn2_w_gate, ffn2_w_up, ffn2_w_down, ffn2_post_norm):
    params = dict(ffn1_pre_norm=ffn1_pre_norm, ffn1_w_gate=ffn1_w_gate, ffn1_w_up=ffn1_w_up,
                  ffn1_w_down=ffn1_w_down, ffn1_post_norm=ffn1_post_norm, mix_pre_norm=mix_pre_norm, w_in=w_in,
                  conv_w=conv_w, conv_b=conv_b, dt_bias=dt_bias, a_log=a_log, d_skip=d_skip, ssm_norm=ssm_norm,
                  w_out=w_out, mix_post_norm=mix_post_norm, ffn2_pre_norm=ffn2_pre_norm, ffn2_w_gate=ffn2_w_gate,
                  ffn2_w_up=ffn2_w_up, ffn2_w_down=ffn2_w_down, ffn2_post_norm=ffn2_post_norm)
    depth = w_in.shape[0]
    batch, seq, _ = x.shape
    inv_freq = ROPE_THETA ** (-jnp.arange(0, HEAD_DIM, 2, dtype=F32) / HEAD_DIM)
    invf = inv_freq.reshape(HEAD_DIM // 2, 1)
    pos3d = positions.reshape(batch, 1, seq)
    idx = jnp.arange(SSD_CHUNK)
    tri = (idx[:, None] >= idx[None, :]).astype(BF16)
    expand = (jnp.arange(2 * LANES)[:, None] % LANES == jnp.arange(SSM_WIDTH)[None, :] // SSM_HEAD_DIM).astype(BF16)
    for i in range(depth):
        x = _layer(x, pos3d, invf, tri, expand, {name: w[i] for name, w in params.items()})
    return x
```

```python
import functools
import math

import jax
import jax.numpy as jnp
from jax import lax
from jax.experimental import pallas as pl
from jax.experimental.pallas import tpu as pltpu

F32 = jnp.float32
BF16 = jnp.bfloat16

D_MODEL = 1024
D_FF = 2816
HEAD_DIM = 64
N_Q_HEADS = 16
N_KV_HEADS = 4
Q_PER_KV = N_Q_HEADS // N_KV_HEADS
ATTN_WIDTH = N_Q_HEADS * HEAD_DIM
KV_WIDTH = N_KV_HEADS * HEAD_DIM
ATTN_BLOCK = 128
ROPE_THETA = 10000.0
SSM_HEADS = 16
SSM_HEAD_DIM = 64
SSM_WIDTH = SSM_HEADS * SSM_HEAD_DIM
SSM_STATE = 128
SSM_GROUPS = 2
GROUP_WIDTH = SSM_WIDTH // SSM_GROUPS
BC_WIDTH = SSM_GROUPS * SSM_STATE
CONV_WIDTH = 4
CONV_CHANNELS = SSM_WIDTH + 2 * BC_WIDTH
SSD_CHUNK = 128
MACARON_WEIGHT = 0.5
NORM_EPS = 1e-6
LOG2_E = math.log2(math.e)

LANES = 128
CARRY_ROWS = 8
MIB = 1024 * 1024

FFN_ROWS = 512
FFN_TILES_PER_STEP = 2
OUT_FFN_TILES_PER_STEP = 2
FF_CHUNK = 256
assert D_FF % FF_CHUNK == 0
PROJ_ROWS = 512
SSD_BATCH = 8

Q_SLABS = ATTN_WIDTH // LANES
KV_SLABS = KV_WIDTH // LANES
KV_PAIRS = N_KV_HEADS // 2


def _rmsnorm(x, gain):
    return x * lax.rsqrt(jnp.mean(x * x, axis=-1, keepdims=True) + NORM_EPS) * gain


def _resident(shape):
    return pl.BlockSpec(shape, lambda *_: (0,) * len(shape), pipeline_mode=pl.Buffered(1))


def _swiglu_half_steps(load_tiles, pre, wg_ref, wu_ref, wd_ref, post, emit):
    finish = None
    x = load_tiles[0]()
    xn = _rmsnorm(x, pre).astype(BF16)
    for t in range(len(load_tiles)):
        acc = jnp.zeros(x.shape, F32)
        x_next = xn_next = None
        for c in range(D_FF // FF_CHUNK):
            cols = slice(c * FF_CHUNK, (c + 1) * FF_CHUNK)
            gate = jnp.dot(xn, wg_ref[:, cols], preferred_element_type=F32)
            up = jnp.dot(xn, wu_ref[:, cols], preferred_element_type=F32)
            hidden = (jax.nn.silu(gate) * up).astype(BF16)
            acc = acc + jnp.dot(hidden, wd_ref[cols, :], preferred_element_type=F32)
            if c == 1 and finish is not None:
                finish()
            if c == 2 and t + 1 < len(load_tiles):
                x_next = load_tiles[t + 1]()
                xn_next = _rmsnorm(x_next, pre).astype(BF16)
        finish = functools.partial(lambda t_, x_t, acc_t: emit(t_, x_t + MACARON_WEIGHT * _rmsnorm(acc_t, post)),
                                   t, x, acc)
        x, xn = x_next, xn_next
    finish()


def _ffn_kernel(x_ref, pre_ref, wg_ref, wu_ref, wd_ref, post_ref, o_ref):
    tiles = [functools.partial(lambda t: x_ref[t * FFN_ROWS:(t + 1) * FFN_ROWS, :], t)
             for t in range(FFN_TILES_PER_STEP)]

    def emit(t, out):
        o_ref[t * FFN_ROWS:(t + 1) * FFN_ROWS, :] = out

    _swiglu_half_steps(tiles, pre_ref[...], wg_ref, wu_ref, wd_ref, post_ref[...], emit)


def _ffn(x2d, pre, wg, wu, wd, post):
    tokens = x2d.shape[0]
    step_rows = FFN_ROWS * FFN_TILES_PER_STEP
    row_spec = pl.BlockSpec((step_rows, D_MODEL), lambda i: (i, 0))
    return pl.pallas_call(
        _ffn_kernel,
        grid=(tokens // step_rows,),
        in_specs=[row_spec, _resident((1, D_MODEL)), _resident((D_MODEL, D_FF)), _resident((D_MODEL, D_FF)),
                  _resident((D_FF, D_MODEL)), _resident((1, D_MODEL))],
        out_specs=row_spec,
        out_shape=jax.ShapeDtypeStruct((tokens, D_MODEL), F32),
        compiler_params=pltpu.CompilerParams(dimension_semantics=("parallel",), vmem_limit_bytes=48 * MIB),
        name="ffn",
    )(x2d, pre, wg, wu, wd, post)


def _softplus(x):
    return jnp.maximum(x, 0.0) + jnp.log1p(jnp.exp(-jnp.abs(x)))


def _inproj_kernel(x_ref, pos_ref, invf_ref, gain_ref, wq_ref, wk_ref, wv_ref, wx_ref, wz_ref, wdt_ref,
                   convw_ref, convb_ref, dtb_ref,
                   q_ref, k_ref, v_ref, xs_ref, b_ref, c_ref, z_ref, dt_ref, ubuf_ref):
    rows = x_ref.shape[1]
    xn = _rmsnorm(x_ref[0], gain_ref[...]).astype(BF16)
    wide = 2 * LANES

    def project(w_ref, c):
        return jnp.dot(xn, w_ref[:, c * wide:(c + 1) * wide], preferred_element_type=F32)

    n_slabs = CONV_CHANNELS // LANES

    @pl.when(pl.program_id(1) == 0)
    def _():
        ubuf_ref[:, 0:CARRY_ROWS, :] = jnp.zeros((n_slabs, CARRY_ROWS, LANES), F32)

    def conv_chunk(c, u):
        for j in (2 * c, 2 * c + 1):
            cols = slice(j * LANES, (j + 1) * LANES)
            ubuf_ref[j, CARRY_ROWS:CARRY_ROWS + rows, :] = u[:, (j % 2) * LANES:(j % 2 + 1) * LANES]
            conv = (convb_ref[:, cols]
                    + ubuf_ref[j, CARRY_ROWS:CARRY_ROWS + rows, :] * convw_ref[CONV_WIDTH - 1:CONV_WIDTH, cols])
            for tap in range(CONV_WIDTH - 1):
                start = CARRY_ROWS - (CONV_WIDTH - 1) + tap
                conv = conv + ubuf_ref[j, start:start + rows, :] * convw_ref[tap:tap + 1, cols]
            act = jax.nn.silu(conv)
            if j < SSM_WIDTH // LANES:
                xs_ref[0, :, cols] = act
            elif j < (SSM_WIDTH + BC_WIDTH) // LANES:
                b_ref[0, :, j * LANES - SSM_WIDTH:(j + 1) * LANES - SSM_WIDTH] = act.astype(BF16)
            else:
                first = SSM_WIDTH + BC_WIDTH
                c_ref[0, :, j * LANES - first:(j + 1) * LANES - first] = act.astype(BF16)
            ubuf_ref[j, 0:CARRY_ROWS, :] = ubuf_ref[j, rows:rows + CARRY_ROWS, :]

    def z_chunk(c):
        z_ref[0, :, c * wide:(c + 1) * wide] = project(wz_ref, c)

    u = project(wx_ref, 0)
    z_chunk(0)

    ang_t = invf_ref[...] * pos_ref[0].astype(F32)
    cos_t = jnp.cos(ang_t)
    sin_t = jnp.sin(ang_t)
    copies = LANES // HEAD_DIM
    cos = jnp.concatenate([cos_t, cos_t] * copies, axis=0).T
    sin = jnp.concatenate([-sin_t, sin_t] * copies, axis=0).T
    lane = lax.broadcasted_iota(jnp.int32, (rows, LANES), 1)
    first_half = (lane % HEAD_DIM) < (HEAD_DIM // 2)

    def rope(t):
        back = pltpu.roll(t, HEAD_DIM // 2, 1)
        fwd = pltpu.roll(t, LANES - HEAD_DIM // 2, 1)
        return t * cos + jnp.where(first_half, fwd, back) * sin

    def q_chunk(c):
        q = project(wq_ref, c)
        for half in range(2):
            q_ref[0, 2 * c + half] = rope(q[:, half * LANES:(half + 1) * LANES]) * (HEAD_DIM ** -0.5 * LOG2_E)

    def kv_chunk(c):
        k = project(wk_ref, c)
        v = project(wv_ref, c)
        for half in range(2):
            k_ref[0, 2 * c + half] = rope(k[:, half * LANES:(half + 1) * LANES])
            v_ref[0, 2 * c + half] = v[:, half * LANES:(half + 1) * LANES]

    dt_ref[0] = _softplus(jnp.dot(xn, wdt_ref[...], preferred_element_type=F32) + dtb_ref[...])
    n_conv = CONV_CHANNELS // wide
    fillers = ([functools.partial(q_chunk, c) for c in range(ATTN_WIDTH // wide)]
               + [functools.partial(kv_chunk, c) for c in range(KV_WIDTH // wide)]
               + [functools.partial(z_chunk, c) for c in range(1, SSM_WIDTH // wide)])
    for c in range(n_conv):
        u_next = project(wx_ref, c + 1) if c + 1 < n_conv else None
        conv_chunk(c, u)
        fillers.pop(0)()
        u = u_next
    for filler in fillers:
        filler()


def _in_proj(x3d, pos3d, invf, gain, wq, w_in, wdt, convw, convb, dtb):
    batch, seq, _ = x3d.shape
    rows = PROJ_ROWS

    def columns(start, width):
        assert start % width == 0
        return pl.BlockSpec((D_MODEL, width), lambda *_: (0, start // width), pipeline_mode=pl.Buffered(1))

    o_k = ATTN_WIDTH
    o_v = o_k + KV_WIDTH
    o_x = o_v + KV_WIDTH
    o_z = o_x + CONV_CHANNELS

    def tile(width):
        return pl.BlockSpec((1, rows, width), lambda b, j: (b, j, 0))

    def slabs(n):
        return pl.BlockSpec((1, n, rows, LANES), lambda b, j: (b, 0, j, 0))

    def out(width, dtype):
        return jax.ShapeDtypeStruct((batch, seq, width), dtype)

    def out_slabs(n):
        return jax.ShapeDtypeStruct((batch, n, seq, LANES), F32)

    return pl.pallas_call(
        _inproj_kernel,
        grid=(batch, seq // rows),
        in_specs=[tile(D_MODEL), pl.BlockSpec((1, 1, rows), lambda b, j: (b, 0, j)),
                  _resident((HEAD_DIM // 2, 1)), _resident((1, D_MODEL)),
                  _resident((D_MODEL, ATTN_WIDTH)), columns(o_k, KV_WIDTH), columns(o_v, KV_WIDTH),
                  columns(o_x, CONV_CHANNELS), columns(o_z, SSM_WIDTH),
                  _resident((D_MODEL, LANES)), _resident((CONV_WIDTH, CONV_CHANNELS)),
                  _resident((1, CONV_CHANNELS)), _resident((1, LANES))],
        out_specs=[slabs(Q_SLABS), slabs(KV_SLABS), slabs(KV_SLABS), tile(SSM_WIDTH), tile(BC_WIDTH),
                   tile(BC_WIDTH), tile(SSM_WIDTH), tile(LANES)],
        out_shape=[out_slabs(Q_SLABS), out_slabs(KV_SLABS), out_slabs(KV_SLABS), out(SSM_WIDTH, F32),
                   out(BC_WIDTH, BF16), out(BC_WIDTH, BF16), out(SSM_WIDTH, F32), out(LANES, F32)],
        scratch_shapes=[pltpu.VMEM((CONV_CHANNELS // LANES, rows + CARRY_ROWS, LANES), F32)],
        compiler_params=pltpu.CompilerParams(dimension_semantics=("parallel", "arbitrary"),
                                             vmem_limit_bytes=56 * MIB),
        name="in_proj",
    )(x3d, pos3d, invf, gain, wq, w_in, w_in, w_in, w_in, wdt, convw, convb, dtb)


STATE_ARRAYS = 3
MASKED = -2.0 ** 100
STATE_U, STATE_M, STATE_L = range(STATE_ARRAYS)


def _attn_blocks(blocks):
    blk = ATTN_BLOCK
    low = lax.broadcasted_iota(jnp.int32, (blk, LANES), 1) < HEAD_DIM
    zero = jnp.zeros((), BF16)

    scores = []
    for q_slabs, k_blk, _, (query_onehot, key_mask), _ in blocks:
        n_keys = k_blk.shape[0]
        kb = k_blk.astype(BF16)
        key_low = lax.broadcasted_iota(jnp.int32, kb.shape, 1) < HEAD_DIM
        keys = jnp.concatenate([jnp.concatenate([jnp.where(key_low, kb, zero), key_mask], axis=1),
                                jnp.concatenate([jnp.where(key_low, zero, kb), key_mask], axis=1)], axis=0)
        lhs = jnp.concatenate([jnp.concatenate([q.astype(BF16) for q in q_slabs], axis=0), query_onehot], axis=1)
        s = lax.dot_general(lhs, keys, (((1,), (1,)), ((), ())), preferred_element_type=F32)
        scores += [s[:, :n_keys], s[:, n_keys:]]
    maxes = [jnp.max(s, axis=-1, keepdims=True) for s in scores]
    probs = [jnp.exp2((s - m).astype(BF16)) for s, m in zip(scores, maxes)]
    results = []
    for i, (_, _, v_blk, _, _) in enumerate(blocks):
        vb = v_blk.astype(BF16)
        key_low = lax.broadcasted_iota(jnp.int32, vb.shape, 1) < HEAD_DIM
        ones_low = jnp.where(key_low, 1.0, 0.0).astype(BF16)
        ones_high = jnp.where(key_low, 0.0, 1.0).astype(BF16)
        rhs = jnp.concatenate([
            jnp.concatenate([jnp.where(key_low, vb, zero), ones_low], axis=1),
            jnp.concatenate([jnp.where(key_low, zero, vb), ones_high], axis=1)], axis=0)
        lhs = jnp.concatenate([probs[2 * i], probs[2 * i + 1]], axis=1)
        results.append(jnp.dot(lhs, rhs, preferred_element_type=F32))
    states = []
    for i, (q_slabs, _, _, _, load_old) in enumerate(blocks):
        result, m_a, m_b = results[i], maxes[2 * i], maxes[2 * i + 1]
        old = None if load_old is None else load_old()
        state = []
        for g in range(len(q_slabs)):
            rows = slice(g * blk, (g + 1) * blk)
            u_new = result[rows, :LANES]
            l_new = result[rows, LANES:]
            m_new = jnp.where(low, m_a[rows], m_b[rows])
            if old is not None:
                u_run, m_run, l_run = old[g]
                m_tot = jnp.maximum(m_run, m_new)
                a = jnp.exp2(m_run - m_tot)
                b = jnp.exp2(m_new - m_tot)
                u_new = u_run * a + u_new * b
                l_new = l_run * a + l_new * b
                m_new = m_tot
            state.append((u_new, m_new, l_new))
        states.append(state)
    return states


def _attn_kernel(q_ref, k_ref, v_ref, o_ref, st4_ref, stt_ref):
    blk = ATTN_BLOCK
    seq = q_ref.shape[2]
    n_slabs = q_ref.shape[1]
    sub4 = seq // 4

    def key_mask(n_keys, shift):
        si = lax.broadcasted_iota(jnp.int32, (n_keys, blk), 0)
        qi = lax.broadcasted_iota(jnp.int32, (n_keys, blk), 1) + shift
        return jnp.where((si <= qi) & (si >= qi - blk), 0.0, MASKED).astype(BF16)

    stacked = n_slabs * blk
    query_onehot = (lax.broadcasted_iota(jnp.int32, (stacked, blk), 0) % blk
                    == lax.broadcasted_iota(jnp.int32, (stacked, blk), 1)).astype(BF16)
    band_bias = (query_onehot, key_mask(2 * blk, blk))
    causal_bias = (query_onehot, key_mask(blk, 0))

    def load_q(rows):
        return [q_ref[0, g, rows, :] for g in range(n_slabs)]

    def load_state(ref, rows):
        return [tuple(ref[a * n_slabs + g, rows, :] for a in range(STATE_ARRAYS)) for g in range(n_slabs)]

    def store_state(ref, rows, state):
        for g in range(n_slabs):
            for a in range(STATE_ARRAYS):
                ref[a * n_slabs + g, rows, :] = state[g][a]

    def store_out(rows, state):
        for g in range(n_slabs):
            o_ref[0, g, rows, :] = (state[g][STATE_U] / state[g][STATE_L]).astype(o_ref.dtype)

    def run(plans):
        states = _attn_blocks([(load_q(rows), k_ref[0, 0, keys, :], v_ref[0, 0, keys, :], bias, load_old)
                               for rows, keys, bias, load_old, _ in plans])
        for (rows, _, _, _, store), state in zip(plans, states):
            store(rows, state)

    def paired(n_blocks, plan, per_trip=2):
        def trip(i, carry):
            run([plan(per_trip * i + j) for j in range(per_trip)])
            return carry
        lax.fori_loop(0, n_blocks // per_trip, trip, 0)

    def d16_plan(r16):
        rows = pl.ds(r16, blk, stride=16)
        dest = pl.ds((r16 % 4) * sub4 + r16 // 4, blk, stride=4)
        return rows, rows, causal_bias, None, lambda _, state: store_state(st4_ref, dest, state)

    paired(16, d16_plan, per_trip=8)

    def to_token_order(rows, state):
        store_state(stt_ref, rows, state)

    def d4_first_plan(r4):
        rows = pl.ds(r4, blk, stride=4)
        src = pl.ds(pl.multiple_of(r4 * sub4, blk), blk)
        return rows, rows, causal_bias, lambda: load_state(st4_ref, src), to_token_order

    paired(4, d4_first_plan, per_trip=4)
    later_blocks = sub4 // blk - 1

    def d4_plan(idx):
        r4 = idx // later_blocks
        n = 1 + idx % later_blocks
        rows = pl.ds(r4 + 4 * blk * n, blk, stride=4)
        keys = pl.ds(r4 + 4 * blk * (n - 1), 2 * blk, stride=4)
        src = pl.ds(pl.multiple_of(r4 * sub4 + n * blk, blk), blk)
        return rows, keys, band_bias, lambda: load_state(st4_ref, src), to_token_order

    paired(4 * later_blocks, d4_plan, per_trip=3)

    def d1_plan(n):
        start = n * blk if isinstance(n, int) else pl.multiple_of(n * blk, blk)
        rows = pl.ds(start, blk)
        if isinstance(n, int) and n == 0:
            return rows, rows, causal_bias, lambda: load_state(stt_ref, rows), store_out
        return rows, pl.ds(start - blk, 2 * blk), band_bias, lambda: load_state(stt_ref, rows), store_out

    last = seq // blk - 1
    run([d1_plan(0)])
    paired(last, lambda i: d1_plan(i + 1), per_trip=3)


def _attention(q, k, v):
    batch, _, seq, _ = q.shape
    slabs_per_pair = Q_SLABS // KV_PAIRS
    q_spec = pl.BlockSpec((1, slabs_per_pair, seq, LANES), lambda b, p: (b, p, 0, 0))
    kv_spec = pl.BlockSpec((1, 1, seq, LANES), lambda b, p: (b, p, 0, 0))
    state_shape = (STATE_ARRAYS * slabs_per_pair, seq, LANES)
    return pl.pallas_call(
        _attn_kernel,
        grid=(batch, KV_PAIRS),
        in_specs=[q_spec, kv_spec, kv_spec],
        out_specs=q_spec,
        out_shape=jax.ShapeDtypeStruct(q.shape, BF16),
        scratch_shapes=[pltpu.VMEM(state_shape, F32), pltpu.VMEM(state_shape, F32)],
        compiler_params=pltpu.CompilerParams(dimension_semantics=("parallel", "parallel"),
                                             vmem_limit_bytes=56 * MIB),
        name="attention",
    )(q, k, v)


def _split_bf16(x, pieces):
    parts = []
    for _ in range(pieces):
        part = x.astype(BF16)
        parts.append(part)
        x = x - part.astype(F32)
    return parts


def _ssd_kernel(xs_ref, b_ref, c_ref, dt_ref, z_ref, arow_ref, dskip_ref, norm_ref, tri_ref, expand_ref,
                y_ref, state_ref):
    chunk = SSD_CHUNK
    lanes = range(xs_ref.shape[0])
    heads_per_group = SSM_HEADS // SSM_GROUPS

    @pl.when(pl.program_id(1) == 0)
    def _():
        state_ref[...] = jnp.zeros(state_ref.shape, F32)

    row = lax.broadcasted_iota(jnp.int32, (chunk, chunk), 0)
    col = lax.broadcasted_iota(jnp.int32, (chunk, chunk), 1)
    causal = row >= col
    low = lax.broadcasted_iota(jnp.int32, (chunk, LANES), 1) < SSM_HEAD_DIM
    zero = jnp.zeros((), BF16)

    def group_cols(g):
        return slice(g * SSM_STATE, (g + 1) * SSM_STATE), slice(g * GROUP_WIDTH, (g + 1) * GROUP_WIDTH)

    dts = [dt_ref[i] for i in lanes]
    a3s = [jnp.dot(tri_ref[...], jnp.concatenate(_split_bf16(dt * arow_ref[...], 3), axis=1),
                   preferred_element_type=F32) for dt in dts]
    a_css = [a3[:, :LANES] + a3[:, LANES:2 * LANES] + a3[:, 2 * LANES:] for a3 in a3s]

    def expand(stat):
        return jnp.dot(jnp.concatenate(_split_bf16(stat, 2), axis=1), expand_ref[...], preferred_element_type=F32)

    w_states = [expand(jnp.exp2(a_cs[chunk - 1:chunk, :] - a_cs) * dt) for a_cs, dt in zip(a_css, dts)]
    scale_offs = [expand(jnp.exp2(a_cs)) for a_cs in a_css]
    source_ts = [(a_cs - jnp.log(dt) * LOG2_E).T for a_cs, dt in zip(a_css, dts)]

    cbs, y_offs, new_states = {}, {}, {}
    for i in lanes:
        xw_bf = (xs_ref[i] * w_states[i]).astype(BF16)
        for g in range(SSM_GROUPS):
            gcols, wide = group_cols(g)
            b_g = b_ref[i, :, gcols]
            c_g = c_ref[i, :, gcols]
            cbs[i, g] = lax.dot_general(c_g, b_g, (((1,), (1,)), ((), ())), preferred_element_type=F32)
            y_offs[i, g] = jnp.dot(c_g, state_ref[i, :, wide].astype(BF16), preferred_element_type=F32)
            b_t = b_g.astype(F32).T.astype(BF16)
            new_states[i, g] = jnp.dot(b_t, xw_bf[:, wide], preferred_element_type=F32)

    y_diags = {}
    for i in lanes:
        xs_bf = xs_ref[i].astype(BF16)
        for g in range(SSM_GROUPS):
            parts = []
            for pair in range(heads_per_group // 2):
                mats = []
                for e in range(2):
                    h = g * heads_per_group + 2 * pair + e
                    seg = a_css[i][:, h:h + 1] - source_ts[i][h:h + 1, :]
                    mats.append((cbs[i, g] * jnp.exp2(jnp.where(causal, seg, -jnp.inf))).astype(BF16))
                first = (g * heads_per_group + 2 * pair) * SSM_HEAD_DIM
                x_pair = xs_bf[:, first:first + 2 * SSM_HEAD_DIM]
                rhs = jnp.concatenate([jnp.where(low, x_pair, zero), jnp.where(low, zero, x_pair)], axis=0)
                parts.append(jnp.dot(jnp.concatenate(mats, axis=1), rhs, preferred_element_type=F32))
            y_diags[i, g] = jnp.concatenate(parts, axis=1)

    for i in lanes:
        chunk_decay = scale_offs[i][chunk - 1:chunk, :]
        y_parts = []
        for g in range(SSM_GROUPS):
            _, wide = group_cols(g)
            y_parts.append(y_diags[i, g] + y_offs[i, g] * scale_offs[i][:, wide])
            state_ref[i, :, wide] = state_ref[i, :, wide] * chunk_decay[:, wide] + new_states[i, g]
        y = jnp.concatenate(y_parts, axis=1) + dskip_ref[...] * xs_ref[i]
        y = y * jax.nn.silu(z_ref[i])
        normed = []
        for g in range(SSM_GROUPS):
            y_g = y[:, g * GROUP_WIDTH:(g + 1) * GROUP_WIDTH]
            normed.append(y_g * lax.rsqrt(jnp.mean(y_g * y_g, axis=-1, keepdims=True) + NORM_EPS))
        y_ref[i] = (jnp.concatenate(normed, axis=1) * norm_ref[...]).astype(y_ref.dtype)


def _ssd(xs, bm, cm, dt, z, arow, dskip, norm, tri, expand):
    batch, seq, _ = xs.shape

    def tile(width):
        return pl.BlockSpec((SSD_BATCH, SSD_CHUNK, width), lambda b, c: (b, c, 0))

    return pl.pallas_call(
        _ssd_kernel,
        grid=(batch // SSD_BATCH, seq // SSD_CHUNK),
        in_specs=[tile(SSM_WIDTH), tile(BC_WIDTH), tile(BC_WIDTH), tile(LANES), tile(SSM_WIDTH),
                  _resident((1, LANES)), _resident((1, SSM_WIDTH)), _resident((1, SSM_WIDTH)),
                  _resident((SSD_CHUNK, SSD_CHUNK)), _resident((2 * LANES, SSM_WIDTH))],
        out_specs=tile(SSM_WIDTH),
        out_shape=jax.ShapeDtypeStruct((batch, seq, SSM_WIDTH), BF16),
        scratch_shapes=[pltpu.VMEM((SSD_BATCH, SSM_STATE, SSM_WIDTH), F32)],
        compiler_params=pltpu.CompilerParams(dimension_semantics=("parallel", "arbitrary"),
                                             vmem_limit_bytes=56 * MIB),
        name="ssd",
    )(xs, bm, cm, dt, z, arow, dskip, norm, tri, expand)


def _out_ffn_kernel(x_ref, attn_ref, y_ref, wo_attn_ref, wo_ssm_ref, mixpost_ref, pre_ref, wg_ref, wu_ref, wd_ref,
                    post_ref, o_ref):
    def mixer_residual(t):
        rows = slice(t * FFN_ROWS, (t + 1) * FFN_ROWS)
        attn = jnp.concatenate([attn_ref[0, j, rows, :] for j in range(Q_SLABS)], axis=1)
        mixed = (jnp.dot(attn, wo_attn_ref[...], preferred_element_type=F32)
                 + jnp.dot(y_ref[0, rows, :], wo_ssm_ref[...], preferred_element_type=F32))
        return x_ref[0, rows, :] + _rmsnorm(mixed, mixpost_ref[...])

    def emit(t, out):
        o_ref[0, t * FFN_ROWS:(t + 1) * FFN_ROWS, :] = out

    tiles = [functools.partial(mixer_residual, t) for t in range(OUT_FFN_TILES_PER_STEP)]
    _swiglu_half_steps(tiles, pre_ref[...], wg_ref, wu_ref, wd_ref, post_ref[...], emit)


def _out_ffn(x3d, attn, y3d, wo_attn, w_out, mixpost, pre, wg, wu, wd, post):
    batch, seq, _ = x3d.shape
    rows = FFN_ROWS * OUT_FFN_TILES_PER_STEP
    row_spec = pl.BlockSpec((1, rows, D_MODEL), lambda b, j: (b, j, 0))
    slab_spec = pl.BlockSpec((1, Q_SLABS, rows, LANES), lambda b, j: (b, 0, j, 0))
    assert ATTN_WIDTH % SSM_WIDTH == 0
    ssm_rows = pl.BlockSpec((SSM_WIDTH, D_MODEL), lambda *_: (ATTN_WIDTH // SSM_WIDTH, 0),
                            pipeline_mode=pl.Buffered(1))
    return pl.pallas_call(
        _out_ffn_kernel,
        grid=(batch, seq // rows),
        in_specs=[row_spec, slab_spec, row_spec, _resident((ATTN_WIDTH, D_MODEL)), ssm_rows,
                  _resident((1, D_MODEL)), _resident((1, D_MODEL)), _resident((D_MODEL, D_FF)),
                  _resident((D_MODEL, D_FF)), _resident((D_FF, D_MODEL)), _resident((1, D_MODEL))],
        out_specs=row_spec,
        out_shape=jax.ShapeDtypeStruct((batch, seq, D_MODEL), F32),
        compiler_params=pltpu.CompilerParams(dimension_semantics=("parallel", "parallel"),
                                             vmem_limit_bytes=58 * MIB),
        name="out_ffn",
    )(x3d, attn, y3d, wo_attn, w_out, mixpost, pre, wg, wu, wd, post)


def _to_slab_head_order(w, axis):
    shape = w.shape
    split = shape[:axis] + (KV_PAIRS, 2, Q_PER_KV, HEAD_DIM) + shape[axis + 1:]
    return jnp.swapaxes(w.reshape(split), axis + 1, axis + 2).reshape(shape)


def _pad_lanes(v):
    return jnp.pad(v, [(0, 0)] * (v.ndim - 1) + [(0, LANES - v.shape[-1])])


def _layer(x, pos3d, invf, tri, expand, p):
    batch, seq, _ = x.shape
    tokens = batch * seq
    row = lambda v: v.reshape(1, -1)
    bf = lambda w: w.astype(BF16)

    x1 = _ffn(x.reshape(tokens, D_MODEL), row(p["ffn1_pre_norm"]), bf(p["ffn1_w_gate"]), bf(p["ffn1_w_up"]),
              bf(p["ffn1_w_down"]), row(p["ffn1_post_norm"])).reshape(batch, seq, D_MODEL)

    w_in = bf(p["w_in"])
    o_dt = ATTN_WIDTH + 2 * KV_WIDTH + CONV_CHANNELS + SSM_WIDTH
    q, k, v, xs, bm, cm, z, dt = _in_proj(
        x1, pos3d, invf, row(p["mix_pre_norm"]),
        _to_slab_head_order(w_in[:, :ATTN_WIDTH], 1), w_in, _pad_lanes(w_in[:, o_dt:]),
        p["conv_w"], row(p["conv_b"]), _pad_lanes(row(p["dt_bias"])))

    attn = _attention(q, k, v)

    arow = _pad_lanes(row(-jnp.exp(p["a_log"]) * LOG2_E))
    dskip = row(jnp.repeat(p["d_skip"], SSM_HEAD_DIM))
    y = _ssd(xs, bm, cm, dt, z, arow, dskip, row(p["ssm_norm"]), tri, expand)

    w_out = bf(p["w_out"])
    return _out_ffn(x1, attn, y, _to_slab_head_order(w_out[:ATTN_WIDTH], 0), w_out,
                    row(p["mix_post_norm"]), row(p["ffn2_pre_norm"]),
                    bf(p["ffn2_w_gate"]), bf(p["ffn2_w_up"]), bf(p["ffn2_w_down"]), row(p["ffn2_post_norm"]))


def kernel(x, positions, ffn1_pre_norm, ffn1_w_gate, ffn1_w_up, ffn1_w_down, ffn1_post_norm, mix_pre_norm, w_in, conv_w, conv_b, dt_bias, a_log, d_skip, ssm_norm, w_out, mix_post_norm, ffn2_pre_norm, ffn2_w_gate, ffn2_w_up, ffn2_w_down, ffn2_post_norm):
    params = dict(ffn1_pre_norm=ffn1_pre_norm, ffn1_w_gate=ffn1_w_gate, ffn1_w_up=ffn1_w_up,
                  ffn1_w_down=ffn1_w_down, ffn1_post_norm=ffn1_post_norm, mix_pre_norm=mix_pre_norm, w_in=w_in,
                  conv_w=conv_w, conv_b=conv_b, dt_bias=dt_bias, a_log=a_log, d_skip=d_skip, ssm_norm=ssm_norm,
                  w_out=w_out, mix_post_norm=mix_post_norm, ffn2_pre_norm=ffn2_pre_norm, ffn2_w_gate=ffn2_w_gate,
                  ffn2_w_up=ffn2_w_up, ffn2_w_down=ffn2_w_down, ffn2_post_norm=ffn2_post_norm)
    depth = w_in.shape[0]
    batch, seq, _ = x.shape
    inv_freq = ROPE_THETA ** (-jnp.arange(0, HEAD_DIM, 2, dtype=F32) / HEAD_DIM)
    invf = inv_freq.reshape(HEAD_DIM // 2, 1)
    pos3d = positions.reshape(batch, 1, seq)
    idx = jnp.arange(SSD_CHUNK)
    tri = (idx[:, None] >= idx[None, :]).astype(BF16)
    expand = (jnp.arange(2 * LANES)[:, None] % LANES == jnp.arange(SSM_WIDTH)[None, :] // SSM_HEAD_DIM).astype(BF16)
    for i in range(depth):
        x = _layer(x, pos3d, invf, tri, expand, {name: w[i] for name, w in params.items()})
    return x
```

```python
import functools
import math

import jax
import jax.numpy as jnp
from jax import lax
from jax.experimental import pallas as pl
from jax.experimental.pallas import tpu as pltpu

F32 = jnp.float32
BF16 = jnp.bfloat16

D_MODEL = 1024
D_FF = 2816
HEAD_DIM = 64
N_Q_HEADS = 16
N_KV_HEADS = 4
Q_PER_KV = N_Q_HEADS // N_KV_HEADS
ATTN_WIDTH = N_Q_HEADS * HEAD_DIM
KV_WIDTH = N_KV_HEADS * HEAD_DIM
ATTN_BLOCK = 128
ROPE_THETA = 10000.0
SSM_HEADS = 16
SSM_HEAD_DIM = 64
SSM_WIDTH = SSM_HEADS * SSM_HEAD_DIM
SSM_STATE = 128
SSM_GROUPS = 2
GROUP_WIDTH = SSM_WIDTH // SSM_GROUPS
BC_WIDTH = SSM_GROUPS * SSM_STATE
CONV_WIDTH = 4
CONV_CHANNELS = SSM_WIDTH + 2 * BC_WIDTH
SSD_CHUNK = 128
MACARON_WEIGHT = 0.5
NORM_EPS = 1e-6
LOG2_E = math.log2(math.e)

LANES = 128
CARRY_ROWS = 8
MIB = 1024 * 1024

FFN_ROWS = 512
FFN_TILES_PER_STEP = 2
OUT_FFN_TILES_PER_STEP = 2
FF_CHUNK = 256
assert D_FF % FF_CHUNK == 0
PROJ_ROWS = 512
SSD_BATCH = 8

Q_SLABS = ATTN_WIDTH // LANES
KV_SLABS = KV_WIDTH // LANES
KV_PAIRS = N_KV_HEADS // 2


def _rmsnorm(x, gain):
    return x * lax.rsqrt(jnp.mean(x * x, axis=-1, keepdims=True) + NORM_EPS) * gain


def _resident(shape):
    return pl.BlockSpec(shape, lambda *_: (0,) * len(shape), pipeline_mode=pl.Buffered(1))


def _swiglu_half_steps(load_tiles, pre, wg_ref, wu_ref, wd_ref, post, emit):
    finish = None
    x = load_tiles[0]()
    xn = _rmsnorm(x, pre).astype(BF16)
    for t in range(len(load_tiles)):
        acc = jnp.zeros(x.shape, F32)
        x_next = xn_next = None
        for c in range(D_FF // FF_CHUNK):
            cols = slice(c * FF_CHUNK, (c + 1) * FF_CHUNK)
            gate = jnp.dot(xn, wg_ref[:, cols], preferred_element_type=F32)
            up = jnp.dot(xn, wu_ref[:, cols], preferred_element_type=F32)
            hidden = (jax.nn.silu(gate) * up).astype(BF16)
            acc = acc + jnp.dot(hidden, wd_ref[cols, :], preferred_element_type=F32)
            if c == 1 and finish is not None:
                finish()
            if c == 2 and t + 1 < len(load_tiles):
                x_next = load_tiles[t + 1]()
                xn_next = _rmsnorm(x_next, pre).astype(BF16)
        finish = functools.partial(lambda t_, x_t, acc_t: emit(t_, x_t + MACARON_WEIGHT * _rmsnorm(acc_t, post)),
                                   t, x, acc)
        x, xn = x_next, xn_next
    finish()


def _ffn_kernel(x_ref, pre_ref, wg_ref, wu_ref, wd_ref, post_ref, o_ref):
    tiles = [functools.partial(lambda t: x_ref[t * FFN_ROWS:(t + 1) * FFN_ROWS, :], t)
             for t in range(FFN_TILES_PER_STEP)]

    def emit(t, out):
        o_ref[t * FFN_ROWS:(t + 1) * FFN_ROWS, :] = out

    _swiglu_half_steps(tiles, pre_ref[...], wg_ref, wu_ref, wd_ref, post_ref[...], emit)


def _ffn(x2d, pre, wg, wu, wd, post):
    tokens = x2d.shape[0]
    step_rows = FFN_ROWS * FFN_TILES_PER_STEP
    row_spec = pl.BlockSpec((step_rows, D_MODEL), lambda i: (i, 0))
    return pl.pallas_call(
        _ffn_kernel,
        grid=(tokens // step_rows,),
        in_specs=[row_spec, _resident((1, D_MODEL)), _resident((D_MODEL, D_FF)), _resident((D_MODEL, D_FF)),
                  _resident((D_FF, D_MODEL)), _resident((1, D_MODEL))],
        out_specs=row_spec,
        out_shape=jax.ShapeDtypeStruct((tokens, D_MODEL), F32),
        compiler_params=pltpu.CompilerParams(dimension_semantics=("parallel",), vmem_limit_bytes=48 * MIB),
        name="ffn",
    )(x2d, pre, wg, wu, wd, post)


def _softplus(x):
    return jnp.maximum(x, 0.0) + jnp.log1p(jnp.exp(-jnp.abs(x)))


def _inproj_kernel(x_ref, pos_ref, invf_ref, gain_ref, wq_ref, wk_ref, wv_ref, wx_ref, wz_ref, wdt_ref,
                   convw_ref, convb_ref, dtb_ref,
                   q_ref, k_ref, v_ref, xs_ref, b_ref, c_ref, z_ref, dt_ref, ubuf_ref):
    rows = x_ref.shape[1]
    xn = _rmsnorm(x_ref[0], gain_ref[...]).astype(BF16)
    wide = 2 * LANES

    def project(w_ref, c):
        return jnp.dot(xn, w_ref[:, c * wide:(c + 1) * wide], preferred_element_type=F32)

    n_slabs = CONV_CHANNELS // LANES

    @pl.when(pl.program_id(1) == 0)
    def _():
        ubuf_ref[:, 0:CARRY_ROWS, :] = jnp.zeros((n_slabs, CARRY_ROWS, LANES), F32)

    def conv_chunk(c, u):
        for j in (2 * c, 2 * c + 1):
            cols = slice(j * LANES, (j + 1) * LANES)
            ubuf_ref[j, CARRY_ROWS:CARRY_ROWS + rows, :] = u[:, (j % 2) * LANES:(j % 2 + 1) * LANES]
            conv = (convb_ref[:, cols]
                    + ubuf_ref[j, CARRY_ROWS:CARRY_ROWS + rows, :] * convw_ref[CONV_WIDTH - 1:CONV_WIDTH, cols])
            for tap in range(CONV_WIDTH - 1):
                start = CARRY_ROWS - (CONV_WIDTH - 1) + tap
                conv = conv + ubuf_ref[j, start:start + rows, :] * convw_ref[tap:tap + 1, cols]
            act = jax.nn.silu(conv)
            if j < SSM_WIDTH // LANES:
                xs_ref[0, :, cols] = act
            elif j < (SSM_WIDTH + BC_WIDTH) // LANES:
                b_ref[0, :, j * LANES - SSM_WIDTH:(j + 1) * LANES - SSM_WIDTH] = act.astype(BF16)
            else:
                first = SSM_WIDTH + BC_WIDTH
                c_ref[0, :, j * LANES - first:(j + 1) * LANES - first] = act.astype(BF16)
            ubuf_ref[j, 0:CARRY_ROWS, :] = ubuf_ref[j, rows:rows + CARRY_ROWS, :]

    def z_chunk(c):
        z_ref[0, :, c * wide:(c + 1) * wide] = project(wz_ref, c)

    u = project(wx_ref, 0)
    z_chunk(0)

    ang_t = invf_ref[...] * pos_ref[0].astype(F32)
    cos_t = jnp.cos(ang_t)
    sin_t = jnp.sin(ang_t)
    copies = LANES // HEAD_DIM
    cos = jnp.concatenate([cos_t, cos_t] * copies, axis=0).T
    sin = jnp.concatenate([-sin_t, sin_t] * copies, axis=0).T
    lane = lax.broadcasted_iota(jnp.int32, (rows, LANES), 1)
    first_half = (lane % HEAD_DIM) < (HEAD_DIM // 2)

    def rope(t):
        back = pltpu.roll(t, HEAD_DIM // 2, 1)
        fwd = pltpu.roll(t, LANES - HEAD_DIM // 2, 1)
        return t * cos + jnp.where(first_half, fwd, back) * sin

    def q_chunk(c):
        q = project(wq_ref, c)
        for half in range(2):
            q_ref[0, 2 * c + half] = rope(q[:, half * LANES:(half + 1) * LANES]) * (HEAD_DIM ** -0.5 * LOG2_E)

    def kv_chunk(c):
        k = project(wk_ref, c)
        v = project(wv_ref, c)
        for half in range(2):
            k_ref[0, 2 * c + half] = rope(k[:, half * LANES:(half + 1) * LANES])
            v_ref[0, 2 * c + half] = v[:, half * LANES:(half + 1) * LANES]

    dt_ref[0] = _softplus(jnp.dot(xn, wdt_ref[...], preferred_element_type=F32) + dtb_ref[...])
    n_conv = CONV_CHANNELS // wide
    fillers = ([functools.partial(q_chunk, c) for c in range(ATTN_WIDTH // wide)]
               + [functools.partial(kv_chunk, c) for c in range(KV_WIDTH // wide)]
               + [functools.partial(z_chunk, c) for c in range(1, SSM_WIDTH // wide)])
    for c in range(n_conv):
        u_next = project(wx_ref, c + 1) if c + 1 < n_conv else None
        conv_chunk(c, u)
        fillers.pop(0)()
        u = u_next
    for filler in fillers:
        filler()


def _in_proj(x3d, pos3d, invf, gain, wq, w_in, wdt, convw, convb, dtb):
    batch, seq, _ = x3d.shape
    rows = PROJ_ROWS

    def columns(start, width):
        assert start % width == 0
        return pl.BlockSpec((D_MODEL, width), lambda *_: (0, start // width), pipeline_mode=pl.Buffered(1))

    o_k = ATTN_WIDTH
    o_v = o_k + KV_WIDTH
    o_x = o_v + KV_WIDTH
    o_z = o_x + CONV_CHANNELS

    def tile(width):
        return pl.BlockSpec((1, rows, width), lambda b, j: (b, j, 0))

    def slabs(n):
        return pl.BlockSpec((1, n, rows, LANES), lambda b, j: (b, 0, j, 0))

    def out(width, dtype):
        return jax.ShapeDtypeStruct((batch, seq, width), dtype)

    def out_slabs(n):
        return jax.ShapeDtypeStruct((batch, n, seq, LANES), F32)

    return pl.pallas_call(
        _inproj_kernel,
        grid=(batch, seq // rows),
        in_specs=[tile(D_MODEL), pl.BlockSpec((1, 1, rows), lambda b, j: (b, 0, j)),
                  _resident((HEAD_DIM // 2, 1)), _resident((1, D_MODEL)),
                  _resident((D_MODEL, ATTN_WIDTH)), columns(o_k, KV_WIDTH), columns(o_v, KV_WIDTH),
                  columns(o_x, CONV_CHANNELS), columns(o_z, SSM_WIDTH),
                  _resident((D_MODEL, LANES)), _resident((CONV_WIDTH, CONV_CHANNELS)),
                  _resident((1, CONV_CHANNELS)), _resident((1, LANES))],
        out_specs=[slabs(Q_SLABS), slabs(KV_SLABS), slabs(KV_SLABS), tile(SSM_WIDTH), tile(BC_WIDTH),
                   tile(BC_WIDTH), tile(SSM_WIDTH), tile(LANES)],
        out_shape=[out_slabs(Q_SLABS), out_slabs(KV_SLABS), out_slabs(KV_SLABS), out(SSM_WIDTH, F32),
                   out(BC_WIDTH, BF16), out(BC_WIDTH, BF16), out(SSM_WIDTH, F32), out(LANES, F32)],
        scratch_shapes=[pltpu.VMEM((CONV_CHANNELS // LANES, rows + CARRY_ROWS, LANES), F32)],
        compiler_params=pltpu.CompilerParams(dimension_semantics=("parallel", "arbitrary"),
                                             vmem_limit_bytes=56 * MIB),
        name="in_proj",
    )(x3d, pos3d, invf, gain, wq, w_in, w_in, w_in, w_in, wdt, convw, convb, dtb)


STATE_ARRAYS = 3
MASKED = -2.0 ** 100
STATE_U, STATE_M, STATE_L = range(STATE_ARRAYS)


def _attn_blocks(blocks):
    blk = ATTN_BLOCK
    low = lax.broadcasted_iota(jnp.int32, (blk, LANES), 1) < HEAD_DIM
    zero = jnp.zeros((), BF16)

    scores = []
    for q_slabs, k_blk, _, (query_onehot, key_mask), _ in blocks:
        n_keys = k_blk.shape[0]
        kb = k_blk.astype(BF16)
        key_low = lax.broadcasted_iota(jnp.int32, kb.shape, 1) < HEAD_DIM
        keys = jnp.concatenate([jnp.concatenate([jnp.where(key_low, kb, zero), key_mask], axis=1),
                                jnp.concatenate([jnp.where(key_low, zero, kb), key_mask], axis=1)], axis=0)
        lhs = jnp.concatenate([jnp.concatenate([q.astype(BF16) for q in q_slabs], axis=0), query_onehot], axis=1)
        s = lax.dot_general(lhs, keys, (((1,), (1,)), ((), ())), preferred_element_type=F32)
        scores += [s[:, :n_keys], s[:, n_keys:]]
    maxes = [jnp.max(s, axis=-1, keepdims=True) for s in scores]
    probs = [jnp.exp2((s - m).astype(BF16)) for s, m in zip(scores, maxes)]
    results = []
    for i, (_, _, v_blk, _, _) in enumerate(blocks):
        vb = v_blk.astype(BF16)
        key_low = lax.broadcasted_iota(jnp.int32, vb.shape, 1) < HEAD_DIM
        ones_low = jnp.where(key_low, 1.0, 0.0).astype(BF16)
        ones_high = jnp.where(key_low, 0.0, 1.0).astype(BF16)
        rhs = jnp.concatenate([
            jnp.concatenate([jnp.where(key_low, vb, zero), ones_low], axis=1),
            jnp.concatenate([jnp.where(key_low, zero, vb), ones_high], axis=1)], axis=0)
        lhs = jnp.concatenate([probs[2 * i], probs[2 * i + 1]], axis=1)
        results.append(jnp.dot(lhs, rhs, preferred_element_type=F32))
    states = []
    for i, (q_slabs, _, _, _, load_old) in enumerate(blocks):
        result, m_a, m_b = results[i], maxes[2 * i], maxes[2 * i + 1]
        old = None if load_old is None else load_old()
        state = []
        for g in range(len(q_slabs)):
            rows = slice(g * blk, (g + 1) * blk)
            u_new = result[rows, :LANES]
            l_new = result[rows, LANES:]
            m_new = jnp.where(low, m_a[rows], m_b[rows])
            if old is not None:
                u_run, m_run, l_run = old[g]
                m_tot = jnp.maximum(m_run, m_new)
                a = jnp.exp2(m_run - m_tot)
                b = jnp.exp2(m_new - m_tot)
                u_new = u_run * a + u_new * b
                l_new = l_run * a + l_new * b
                m_new = m_tot
            state.append((u_new, m_new, l_new))
        states.append(state)
    return states


def _attn_kernel(q_ref, k_ref, v_ref, o_ref, st4_ref, stt_ref):
    blk = ATTN_BLOCK
    seq = q_ref.shape[2]
    n_slabs = q_ref.shape[1]
    sub4 = seq // 4

    def key_mask(n_keys, shift):
        si = lax.broadcasted_iota(jnp.int32, (n_keys, blk), 0)
        qi = lax.broadcasted_iota(jnp.int32, (n_keys, blk), 1) + shift
        return jnp.where((si <= qi) & (si >= qi - blk), 0.0, MASKED).astype(BF16)

    stacked = n_slabs * blk
    query_onehot = (lax.broadcasted_iota(jnp.int32, (stacked, blk), 0) % blk
                    == lax.broadcasted_iota(jnp.int32, (stacked, blk), 1)).astype(BF16)
    band_bias = (query_onehot, key_mask(2 * blk, blk))
    causal_bias = (query_onehot, key_mask(blk, 0))

    def load_q(rows):
        return [q_ref[0, g, rows, :] for g in range(n_slabs)]

    def load_state(ref, rows):
        return [tuple(ref[a * n_slabs + g, rows, :] for a in range(STATE_ARRAYS)) for g in range(n_slabs)]

    def store_state(ref, rows, state):
        for g in range(n_slabs):
            for a in range(STATE_ARRAYS):
                ref[a * n_slabs + g, rows, :] = state[g][a]

    def store_out(rows, state):
        for g in range(n_slabs):
            o_ref[0, g, rows, :] = (state[g][STATE_U] / state[g][STATE_L]).astype(o_ref.dtype)

    def run(plans):
        states = _attn_blocks([(load_q(rows), k_ref[0, 0, keys, :], v_ref[0, 0, keys, :], bias, load_old)
                               for rows, keys, bias, load_old, _ in plans])
        for (rows, _, _, _, store), state in zip(plans, states):
            store(rows, state)

    def paired(n_blocks, plan, per_trip=2):
        def trip(i, carry):
            run([plan(per_trip * i + j) for j in range(per_trip)])
            return carry
        lax.fori_loop(0, n_blocks // per_trip, trip, 0)

    def d16_plan(r16):
        rows = pl.ds(r16, blk, stride=16)
        dest = pl.ds((r16 % 4) * sub4 + r16 // 4, blk, stride=4)
        return rows, rows, causal_bias, None, lambda _, state: store_state(st4_ref, dest, state)

    paired(16, d16_plan, per_trip=16)

    def to_token_order(rows, state):
        store_state(stt_ref, rows, state)

    def d4_first_plan(r4):
        rows = pl.ds(r4, blk, stride=4)
        src = pl.ds(pl.multiple_of(r4 * sub4, blk), blk)
        return rows, rows, causal_bias, lambda: load_state(st4_ref, src), to_token_order

    paired(4, d4_first_plan, per_trip=4)
    later_blocks = sub4 // blk - 1

    def d4_plan(idx):
        r4 = idx // later_blocks
        n = 1 + idx % later_blocks
        rows = pl.ds(r4 + 4 * blk * n, blk, stride=4)
        keys = pl.ds(r4 + 4 * blk * (n - 1), 2 * blk, stride=4)
        src = pl.ds(pl.multiple_of(r4 * sub4 + n * blk, blk), blk)
        return rows, keys, band_bias, lambda: load_state(st4_ref, src), to_token_order

    paired(4 * later_blocks, d4_plan, per_trip=3)

    def d1_plan(n):
        start = n * blk if isinstance(n, int) else pl.multiple_of(n * blk, blk)
        rows = pl.ds(start, blk)
        if isinstance(n, int) and n == 0:
            return rows, rows, causal_bias, lambda: load_state(stt_ref, rows), store_out
        return rows, pl.ds(start - blk, 2 * blk), band_bias, lambda: load_state(stt_ref, rows), store_out

    last = seq // blk - 1
    run([d1_plan(0)])
    paired(last, lambda i: d1_plan(i + 1), per_trip=3)


def _attention(q, k, v):
    batch, _, seq, _ = q.shape
    slabs_per_pair = Q_SLABS // KV_PAIRS
    q_spec = pl.BlockSpec((1, slabs_per_pair, seq, LANES), lambda b, p: (b, p, 0, 0))
    kv_spec = pl.BlockSpec((1, 1, seq, LANES), lambda b, p: (b, p, 0, 0))
    state_shape = (STATE_ARRAYS * slabs_per_pair, seq, LANES)
    return pl.pallas_call(
        _attn_kernel,
        grid=(batch, KV_PAIRS),
        in_specs=[q_spec, kv_spec, kv_spec],
        out_specs=q_spec,
        out_shape=jax.ShapeDtypeStruct(q.shape, BF16),
        scratch_shapes=[pltpu.VMEM(state_shape, F32), pltpu.VMEM(state_shape, F32)],
        compiler_params=pltpu.CompilerParams(dimension_semantics=("parallel", "parallel"),
                                             vmem_limit_bytes=56 * MIB),
        name="attention",
    )(q, k, v)


def _split_bf16(x, pieces):
    parts = []
    for _ in range(pieces):
        part = x.astype(BF16)
        parts.append(part)
        x = x - part.astype(F32)
    return parts


def _ssd_kernel(xs_ref, b_ref, c_ref, dt_ref, z_ref, arow_ref, dskip_ref, norm_ref, tri_ref, expand_ref,
                y_ref, state_ref):
    chunk = SSD_CHUNK
    lanes = range(xs_ref.shape[0])
    heads_per_group = SSM_HEADS // SSM_GROUPS

    @pl.when(pl.program_id(1) == 0)
    def _():
        state_ref[...] = jnp.zeros(state_ref.shape, F32)

    row = lax.broadcasted_iota(jnp.int32, (chunk, chunk), 0)
    col = lax.broadcasted_iota(jnp.int32, (chunk, chunk), 1)
    causal = row >= col
    low = lax.broadcasted_iota(jnp.int32, (chunk, LANES), 1) < SSM_HEAD_DIM
    zero = jnp.zeros((), BF16)

    def group_cols(g):
        return slice(g * SSM_STATE, (g + 1) * SSM_STATE), slice(g * GROUP_WIDTH, (g + 1) * GROUP_WIDTH)

    dts = [dt_ref[i] for i in lanes]
    a3s = [jnp.dot(tri_ref[...], jnp.concatenate(_split_bf16(dt * arow_ref[...], 3), axis=1),
                   preferred_element_type=F32) for dt in dts]
    a_css = [a3[:, :LANES] + a3[:, LANES:2 * LANES] + a3[:, 2 * LANES:] for a3 in a3s]

    def expand(stat):
        return jnp.dot(jnp.concatenate(_split_bf16(stat, 2), axis=1), expand_ref[...], preferred_element_type=F32)

    w_states = [expand(jnp.exp2(a_cs[chunk - 1:chunk, :] - a_cs) * dt) for a_cs, dt in zip(a_css, dts)]
    scale_offs = [expand(jnp.exp2(a_cs)) for a_cs in a_css]
    source_ts = [(a_cs - jnp.log(dt) * LOG2_E).T for a_cs, dt in zip(a_css, dts)]

    cbs, y_offs, new_states = {}, {}, {}
    for i in lanes:
        xw_bf = (xs_ref[i] * w_states[i]).astype(BF16)
        for g in range(SSM_GROUPS):
            gcols, wide = group_cols(g)
            b_g = b_ref[i, :, gcols]
            c_g = c_ref[i, :, gcols]
            cbs[i, g] = lax.dot_general(c_g, b_g, (((1,), (1,)), ((), ())), preferred_element_type=F32)
            y_offs[i, g] = jnp.dot(c_g, state_ref[i, :, wide].astype(BF16), preferred_element_type=F32)
            b_t = b_g.astype(F32).T.astype(BF16)
            new_states[i, g] = jnp.dot(b_t, xw_bf[:, wide], preferred_element_type=F32)

    y_diags = {}
    for i in lanes:
        xs_bf = xs_ref[i].astype(BF16)
        for g in range(SSM_GROUPS):
            parts = []
            for pair in range(heads_per_group // 2):
                mats = []
                for e in range(2):
                    h = g * heads_per_group + 2 * pair + e
                    seg = a_css[i][:, h:h + 1] - source_ts[i][h:h + 1, :]
                    mats.append((cbs[i, g] * jnp.exp2(jnp.where(causal, seg, -jnp.inf))).astype(BF16))
                first = (g * heads_per_group + 2 * pair) * SSM_HEAD_DIM
                x_pair = xs_bf[:, first:first + 2 * SSM_HEAD_DIM]
                rhs = jnp.concatenate([jnp.where(low, x_pair, zero), jnp.where(low, zero, x_pair)], axis=0)
                parts.append(jnp.dot(jnp.concatenate(mats, axis=1), rhs, preferred_element_type=F32))
            y_diags[i, g] = jnp.concatenate(parts, axis=1)

    for i in lanes:
        chunk_decay = scale_offs[i][chunk - 1:chunk, :]
        y_parts = []
        for g in range(SSM_GROUPS):
            _, wide = group_cols(g)
            y_parts.append(y_diags[i, g] + y_offs[i, g] * scale_offs[i][:, wide])
            state_ref[i, :, wide] = state_ref[i, :, wide] * chunk_decay[:, wide] + new_states[i, g]
        y = jnp.concatenate(y_parts, axis=1) + dskip_ref[...] * xs_ref[i]
        y = y * jax.nn.silu(z_ref[i])
        normed = []
        for g in range(SSM_GROUPS):
            y_g = y[:, g * GROUP_WIDTH:(g + 1) * GROUP_WIDTH]
            normed.append(y_g * lax.rsqrt(jnp.mean(y_g * y_g, axis=-1, keepdims=True) + NORM_EPS))
        y_ref[i] = (jnp.concatenate(normed, axis=1) * norm_ref[...]).astype(y_ref.dtype)


def _ssd(xs, bm, cm, dt, z, arow, dskip, norm, tri, expand):
    batch, seq, _ = xs.shape

    def tile(width):
        return pl.BlockSpec((SSD_BATCH, SSD_CHUNK, width), lambda b, c: (b, c, 0))

    return pl.pallas_call(
        _ssd_kernel,
        grid=(batch // SSD_BATCH, seq // SSD_CHUNK),
        in_specs=[tile(SSM_WIDTH), tile(BC_WIDTH), tile(BC_WIDTH), tile(LANES), tile(SSM_WIDTH),
                  _resident((1, LANES)), _resident((1, SSM_WIDTH)), _resident((1, SSM_WIDTH)),
                  _resident((SSD_CHUNK, SSD_CHUNK)), _resident((2 * LANES, SSM_WIDTH))],
        out_specs=tile(SSM_WIDTH),
        out_shape=jax.ShapeDtypeStruct((batch, seq, SSM_WIDTH), BF16),
        scratch_shapes=[pltpu.VMEM((SSD_BATCH, SSM_STATE, SSM_WIDTH), F32)],
        compiler_params=pltpu.CompilerParams(dimension_semantics=("parallel", "arbitrary"),
                                             vmem_limit_bytes=56 * MIB),
        name="ssd",
    )(xs, bm, cm, dt, z, arow, dskip, norm, tri, expand)


def _out_ffn_kernel(x_ref, attn_ref, y_ref, wo_attn_ref, wo_ssm_ref, mixpost_ref, pre_ref, wg_ref, wu_ref, wd_ref,
                    post_ref, o_ref):
    def mixer_residual(t):
        rows = slice(t * FFN_ROWS, (t + 1) * FFN_ROWS)
        attn = jnp.concatenate([attn_ref[0, j, rows, :] for j in range(Q_SLABS)], axis=1)
        mixed = (jnp.dot(attn, wo_attn_ref[...], preferred_element_type=F32)
                 + jnp.dot(y_ref[0, rows, :], wo_ssm_ref[...], preferred_element_type=F32))
        return x_ref[0, rows, :] + _rmsnorm(mixed, mixpost_ref[...])

    def emit(t, out):
        o_ref[0, t * FFN_ROWS:(t + 1) * FFN_ROWS, :] = out

    tiles = [functools.partial(mixer_residual, t) for t in range(OUT_FFN_TILES_PER_STEP)]
    _swiglu_half_steps(tiles, pre_ref[...], wg_ref, wu_ref, wd_ref, post_ref[...], emit)


def _out_ffn(x3d, attn, y3d, wo_attn, w_out, mixpost, pre, wg, wu, wd, post):
    batch, seq, _ = x3d.shape
    rows = FFN_ROWS * OUT_FFN_TILES_PER_STEP
    row_spec = pl.BlockSpec((1, rows, D_MODEL), lambda b, j: (b, j, 0))
    slab_spec = pl.BlockSpec((1, Q_SLABS, rows, LANES), lambda b, j: (b, 0, j, 0))
    assert ATTN_WIDTH % SSM_WIDTH == 0
    ssm_rows = pl.BlockSpec((SSM_WIDTH, D_MODEL), lambda *_: (ATTN_WIDTH // SSM_WIDTH, 0),
                            pipeline_mode=pl.Buffered(1))
    return pl.pallas_call(
        _out_ffn_kernel,
        grid=(batch, seq // rows),
        in_specs=[row_spec, slab_spec, row_spec, _resident((ATTN_WIDTH, D_MODEL)), ssm_rows,
                  _resident((1, D_MODEL)), _resident((1, D_MODEL)), _resident((D_MODEL, D_FF)),
                  _resident((D_MODEL, D_FF)), _resident((D_FF, D_MODEL)), _resident((1, D_MODEL))],
        out_specs=row_spec,
        out_shape=jax.ShapeDtypeStruct((batch, seq, D_MODEL), F32),
        compiler_params=pltpu.CompilerParams(dimension_semantics=("parallel", "parallel"),
                                             vmem_limit_bytes=58 * MIB),
        name="out_ffn",
    )(x3d, attn, y3d, wo_attn, w_out, mixpost, pre, wg, wu, wd, post)


def _to_slab_head_order(w, axis):
    shape = w.shape
    split = shape[:axis] + (KV_PAIRS, 2, Q_PER_KV, HEAD_DIM) + shape[axis + 1:]
    return jnp.swapaxes(w.reshape(split), axis + 1, axis + 2).reshape(shape)


def _pad_lanes(v):
    return jnp.pad(v, [(0, 0)] * (v.ndim - 1) + [(0, LANES - v.shape[-1])])


def _layer(x, pos3d, invf, tri, expand, p):
    batch, seq, _ = x.shape
    tokens = batch * seq
    row = lambda v: v.reshape(1, -1)
    bf = lambda w: w.astype(BF16)

    x1 = _ffn(x.reshape(tokens, D_MODEL), row(p["ffn1_pre_norm"]), bf(p["ffn1_w_gate"]), bf(p["ffn1_w_up"]),
              bf(p["ffn1_w_down"]), row(p["ffn1_post_norm"])).reshape(batch, seq, D_MODEL)

    w_in = bf(p["w_in"])
    o_dt = ATTN_WIDTH + 2 * KV_WIDTH + CONV_CHANNELS + SSM_WIDTH
    q, k, v, xs, bm, cm, z, dt = _in_proj(
        x1, pos3d, invf, row(p["mix_pre_norm"]),
        _to_slab_head_order(w_in[:, :ATTN_WIDTH], 1), w_in, _pad_lanes(w_in[:, o_dt:]),
        p["conv_w"], row(p["conv_b"]), _pad_lanes(row(p["dt_bias"])))

    attn = _attention(q, k, v)

    arow = _pad_lanes(row(-jnp.exp(p["a_log"]) * LOG2_E))
    dskip = row(jnp.repeat(p["d_skip"], SSM_HEAD_DIM))
    y = _ssd(xs, bm, cm, dt, z, arow, dskip, row(p["ssm_norm"]), tri, expand)

    w_out = bf(p["w_out"])
    return _out_ffn(x1, attn, y, _to_slab_head_order(w_out[:ATTN_WIDTH], 0), w_out,
                    row(p["mix_post_norm"]), row(p["ffn2_pre_norm"]),
                    bf(p["ffn2_w_gate"]), bf(p["ffn2_w_up"]), bf(p["ffn2_w_down"]), row(p["ffn2_post_norm"]))


def kernel(x, positions, ffn1_pre_norm, ffn1_w_gate, ffn1_w_up, ffn1_w_down, ffn1_post_norm, mix_pre_norm, w_in, conv_w, conv_b, dt_bias, a_log, d_skip, ssm_norm, w_out, mix_post_norm, ffn2_pre_norm, ffn2_w_gate, ffn2_w_up, ffn2_w_down, ffn2_post_norm):
    params = dict(ffn1_pre_norm=ffn1_pre_norm, ffn1_w_gate=ffn1_w_gate, ffn1_w_up=ffn1_w_up,
                  ffn1_w_down=ffn1_w_down, ffn1_post_norm=ffn1_post_norm, mix_pre_norm=mix_pre_norm, w_in=w_in,
                  conv_w=conv_w, conv_b=conv_b, dt_bias=dt_bias, a_log=a_log, d_skip=d_skip, ssm_norm=ssm_norm,
                  w_out=w_out, mix_post_norm=mix_post_norm, ffn2_pre_norm=ffn2_pre_norm, ffn2_w_gate=ffn2_w_gate,
                  ffn2_w_up=ffn2_w_up, ffn2_w_down=ffn2_w_down, ffn2_post_norm=ffn2_post_norm)
    depth = w_in.shape[0]
    batch, seq, _ = x.shape
    inv_freq = ROPE_THETA ** (-jnp.arange(0, HEAD_DIM, 2, dtype=F32) / HEAD_DIM)
    invf = inv_freq.reshape(HEAD_DIM // 2, 1)
    pos3d = positions.reshape(batch, 1, seq)
    idx = jnp.arange(SSD_CHUNK)
    tri = (idx[:, None] >= idx[None, :]).astype(BF16)
    expand = (jnp.arange(2 * LANES)[:, None] % LANES == jnp.arange(SSM_WIDTH)[None, :] // SSM_HEAD_DIM).astype(BF16)
    for i in range(depth):
        x = _layer(x, pos3d, invf, tri, expand, {name: w[i] for name, w in params.items()})
    return x
```

```python
import functools
import math

import jax
import jax.numpy as jnp
from jax import lax
from jax.experimental import pallas as pl
from jax.experimental.pallas import tpu as pltpu

F32 = jnp.float32
BF16 = jnp.bfloat16

D_MODEL = 1024
D_FF = 2816
HEAD_DIM = 64
N_Q_HEADS = 16
N_KV_HEADS = 4
Q_PER_KV = N_Q_HEADS // N_KV_HEADS
ATTN_WIDTH = N_Q_HEADS * HEAD_DIM
KV_WIDTH = N_KV_HEADS * HEAD_DIM
ATTN_BLOCK = 128
ROPE_THETA = 10000.0
SSM_HEADS = 16
SSM_HEAD_DIM = 64
SSM_WIDTH = SSM_HEADS * SSM_HEAD_DIM
SSM_STATE = 128
SSM_GROUPS = 2
GROUP_WIDTH = SSM_WIDTH // SSM_GROUPS
BC_WIDTH = SSM_GROUPS * SSM_STATE
CONV_WIDTH = 4
CONV_CHANNELS = SSM_WIDTH + 2 * BC_WIDTH
SSD_CHUNK = 128
MACARON_WEIGHT = 0.5
NORM_EPS = 1e-6
LOG2_E = math.log2(math.e)

LANES = 128
CARRY_ROWS = 8
MIB = 1024 * 1024

FFN_ROWS = 512
FFN_TILES_PER_STEP = 2
OUT_FFN_TILES_PER_STEP = 2
FF_CHUNK = 256
assert D_FF % FF_CHUNK == 0
PROJ_ROWS = 512
SSD_BATCH = 8

Q_SLABS = ATTN_WIDTH // LANES
KV_SLABS = KV_WIDTH // LANES
KV_PAIRS = N_KV_HEADS // 2


def _rmsnorm(x, gain):
    return x * lax.rsqrt(jnp.mean(x * x, axis=-1, keepdims=True) + NORM_EPS) * gain


def _resident(shape):
    return pl.BlockSpec(shape, lambda *_: (0,) * len(shape), pipeline_mode=pl.Buffered(1))


def _swiglu_half_steps(load_tiles, pre, wg_ref, wu_ref, wd_ref, post, emit):
    finish = None
    x = load_tiles[0]()
    xn = _rmsnorm(x, pre).astype(BF16)
    for t in range(len(load_tiles)):
        acc = jnp.zeros(x.shape, F32)
        x_next = xn_next = None
        for c in range(D_FF // FF_CHUNK):
            cols = slice(c * FF_CHUNK, (c + 1) * FF_CHUNK)
            gate = jnp.dot(xn, wg_ref[:, cols], preferred_element_type=F32)
            up = jnp.dot(xn, wu_ref[:, cols], preferred_element_type=F32)
            hidden = (jax.nn.silu(gate) * up).astype(BF16)
            acc = acc + jnp.dot(hidden, wd_ref[cols, :], preferred_element_type=F32)
            if c == 1 and finish is not None:
                finish()
            if c == 2 and t + 1 < len(load_tiles):
                x_next = load_tiles[t + 1]()
                xn_next = _rmsnorm(x_next, pre).astype(BF16)
        finish = functools.partial(lambda t_, x_t, acc_t: emit(t_, x_t + MACARON_WEIGHT * _rmsnorm(acc_t, post)),
                                   t, x, acc)
        x, xn = x_next, xn_next
    finish()


def _ffn_kernel(x_ref, pre_ref, wg_ref, wu_ref, wd_ref, post_ref, o_ref):
    tiles = [functools.partial(lambda t: x_ref[t * FFN_ROWS:(t + 1) * FFN_ROWS, :], t)
             for t in range(FFN_TILES_PER_STEP)]

    def emit(t, out):
        o_ref[t * FFN_ROWS:(t + 1) * FFN_ROWS, :] = out

    _swiglu_half_steps(tiles, pre_ref[...], wg_ref, wu_ref, wd_ref, post_ref[...], emit)


def _ffn(x2d, pre, wg, wu, wd, post):
    tokens = x2d.shape[0]
    step_rows = FFN_ROWS * FFN_TILES_PER_STEP
    row_spec = pl.BlockSpec((step_rows, D_MODEL), lambda i: (i, 0))
    return pl.pallas_call(
        _ffn_kernel,
        grid=(tokens // step_rows,),
        in_specs=[row_spec, _resident((1, D_MODEL)), _resident((D_MODEL, D_FF)), _resident((D_MODEL, D_FF)),
                  _resident((D_FF, D_MODEL)), _resident((1, D_MODEL))],
        out_specs=row_spec,
        out_shape=jax.ShapeDtypeStruct((tokens, D_MODEL), F32),
        compiler_params=pltpu.CompilerParams(dimension_semantics=("parallel",), vmem_limit_bytes=48 * MIB),
        name="ffn",
    )(x2d, pre, wg, wu, wd, post)


def _softplus(x):
    return jnp.maximum(x, 0.0) + jnp.log1p(jnp.exp(-jnp.abs(x)))


def _inproj_kernel(x_ref, pos_ref, invf_ref, gain_ref, wq_ref, wk_ref, wv_ref, wx_ref, wz_ref, wdt_ref,
                   convw_ref, convb_ref, dtb_ref,
                   q_ref, k_ref, v_ref, xs_ref, b_ref, c_ref, z_ref, dt_ref, ubuf_ref):
    rows = x_ref.shape[1]
    xn = _rmsnorm(x_ref[0], gain_ref[...]).astype(BF16)
    wide = 2 * LANES

    def project(w_ref, c):
        return jnp.dot(xn, w_ref[:, c * wide:(c + 1) * wide], preferred_element_type=F32)

    n_slabs = CONV_CHANNELS // LANES

    @pl.when(pl.program_id(1) == 0)
    def _():
        ubuf_ref[:, 0:CARRY_ROWS, :] = jnp.zeros((n_slabs, CARRY_ROWS, LANES), F32)

    def conv_chunk(c, u):
        for j in (2 * c, 2 * c + 1):
            cols = slice(j * LANES, (j + 1) * LANES)
            ubuf_ref[j, CARRY_ROWS:CARRY_ROWS + rows, :] = u[:, (j % 2) * LANES:(j % 2 + 1) * LANES]
            conv = (convb_ref[:, cols]
                    + ubuf_ref[j, CARRY_ROWS:CARRY_ROWS + rows, :] * convw_ref[CONV_WIDTH - 1:CONV_WIDTH, cols])
            for tap in range(CONV_WIDTH - 1):
                start = CARRY_ROWS - (CONV_WIDTH - 1) + tap
                conv = conv + ubuf_ref[j, start:start + rows, :] * convw_ref[tap:tap + 1, cols]
            act = jax.nn.silu(conv)
            if j < SSM_WIDTH // LANES:
                xs_ref[0, :, cols] = act
            elif j < (SSM_WIDTH + BC_WIDTH) // LANES:
                b_ref[0, :, j * LANES - SSM_WIDTH:(j + 1) * LANES - SSM_WIDTH] = act.astype(BF16)
            else:
                first = SSM_WIDTH + BC_WIDTH
                c_ref[0, :, j * LANES - first:(j + 1) * LANES - first] = act.astype(BF16)
            ubuf_ref[j, 0:CARRY_ROWS, :] = ubuf_ref[j, rows:rows + CARRY_ROWS, :]

    def z_chunk(c):
        z_ref[0, :, c * wide:(c + 1) * wide] = project(wz_ref, c)

    u = project(wx_ref, 0)
    z_chunk(0)

    ang_t = invf_ref[...] * pos_ref[0].astype(F32)
    cos_t = jnp.cos(ang_t)
    sin_t = jnp.sin(ang_t)
    copies = LANES // HEAD_DIM
    cos = jnp.concatenate([cos_t, cos_t] * copies, axis=0).T
    sin = jnp.concatenate([-sin_t, sin_t] * copies, axis=0).T
    lane = lax.broadcasted_iota(jnp.int32, (rows, LANES), 1)
    first_half = (lane % HEAD_DIM) < (HEAD_DIM // 2)

    def rope(t):
        back = pltpu.roll(t, HEAD_DIM // 2, 1)
        fwd = pltpu.roll(t, LANES - HEAD_DIM // 2, 1)
        return t * cos + jnp.where(first_half, fwd, back) * sin

    def q_chunk(c):
        q = project(wq_ref, c)
        for half in range(2):
            q_ref[0, 2 * c + half] = rope(q[:, half * LANES:(half + 1) * LANES]) * (HEAD_DIM ** -0.5 * LOG2_E)

    def kv_chunk(c):
        k = project(wk_ref, c)
        v = project(wv_ref, c)
        for half in range(2):
            k_ref[0, 2 * c + half] = rope(k[:, half * LANES:(half + 1) * LANES])
            v_ref[0, 2 * c + half] = v[:, half * LANES:(half + 1) * LANES]

    dt_ref[0] = _softplus(jnp.dot(xn, wdt_ref[...], preferred_element_type=F32) + dtb_ref[...])
    n_conv = CONV_CHANNELS // wide
    fillers = ([functools.partial(q_chunk, c) for c in range(ATTN_WIDTH // wide)]
               + [functools.partial(kv_chunk, c) for c in range(KV_WIDTH // wide)]
               + [functools.partial(z_chunk, c) for c in range(1, SSM_WIDTH // wide)])
    for c in range(n_conv):
        u_next = project(wx_ref, c + 1) if c + 1 < n_conv else None
        conv_chunk(c, u)
        fillers.pop(0)()
        u = u_next
    for filler in fillers:
        filler()


def _in_proj(x3d, pos3d, invf, gain, wq, w_in, wdt, convw, convb, dtb):
    batch, seq, _ = x3d.shape
    rows = PROJ_ROWS

    def columns(start, width):
        assert start % width == 0
        return pl.BlockSpec((D_MODEL, width), lambda *_: (0, start // width), pipeline_mode=pl.Buffered(1))

    o_k = ATTN_WIDTH
    o_v = o_k + KV_WIDTH
    o_x = o_v + KV_WIDTH
    o_z = o_x + CONV_CHANNELS

    def tile(width):
        return pl.BlockSpec((1, rows, width), lambda b, j: (b, j, 0))

    def slabs(n):
        return pl.BlockSpec((1, n, rows, LANES), lambda b, j: (b, 0, j, 0))

    def out(width, dtype):
        return jax.ShapeDtypeStruct((batch, seq, width), dtype)

    def out_slabs(n):
        return jax.ShapeDtypeStruct((batch, n, seq, LANES), F32)

    return pl.pallas_call(
        _inproj_kernel,
        grid=(batch, seq // rows),
        in_specs=[tile(D_MODEL), pl.BlockSpec((1, 1, rows), lambda b, j: (b, 0, j)),
                  _resident((HEAD_DIM // 2, 1)), _resident((1, D_MODEL)),
                  _resident((D_MODEL, ATTN_WIDTH)), columns(o_k, KV_WIDTH), columns(o_v, KV_WIDTH),
                  columns(o_x, CONV_CHANNELS), columns(o_z, SSM_WIDTH),
                  _resident((D_MODEL, LANES)), _resident((CONV_WIDTH, CONV_CHANNELS)),
                  _resident((1, CONV_CHANNELS)), _resident((1, LANES))],
        out_specs=[slabs(Q_SLABS), slabs(KV_SLABS), slabs(KV_SLABS), tile(SSM_WIDTH), tile(BC_WIDTH),
                   tile(BC_WIDTH), tile(SSM_WIDTH), tile(LANES)],
        out_shape=[out_slabs(Q_SLABS), out_slabs(KV_SLABS), out_slabs(KV_SLABS), out(SSM_WIDTH, F32),
                   out(BC_WIDTH, BF16), out(BC_WIDTH, BF16), out(SSM_WIDTH, F32), out(LANES, F32)],
        scratch_shapes=[pltpu.VMEM((CONV_CHANNELS // LANES, rows + CARRY_ROWS, LANES), F32)],
        compiler_params=pltpu.CompilerParams(dimension_semantics=("parallel", "arbitrary"),
                                             vmem_limit_bytes=56 * MIB),
        name="in_proj",
    )(x3d, pos3d, invf, gain, wq, w_in, w_in, w_in, w_in, wdt, convw, convb, dtb)


STATE_ARRAYS = 3
MASKED = -2.0 ** 100
STATE_U, STATE_M, STATE_L = range(STATE_ARRAYS)


def _attn_blocks(blocks):
    blk = ATTN_BLOCK
    low = lax.broadcasted_iota(jnp.int32, (blk, LANES), 1) < HEAD_DIM
    zero = jnp.zeros((), BF16)

    scores = []
    for q_slabs, k_blk, _, (query_onehot, key_mask), _ in blocks:
        n_keys = k_blk.shape[0]
        kb = k_blk.astype(BF16)
        key_low = lax.broadcasted_iota(jnp.int32, kb.shape, 1) < HEAD_DIM
        keys = jnp.concatenate([jnp.concatenate([jnp.where(key_low, kb, zero), key_mask], axis=1),
                                jnp.concatenate([jnp.where(key_low, zero, kb), key_mask], axis=1)], axis=0)
        lhs = jnp.concatenate([jnp.concatenate([q.astype(BF16) for q in q_slabs], axis=0), query_onehot], axis=1)
        s = lax.dot_general(lhs, keys, (((1,), (1,)), ((), ())), preferred_element_type=F32)
        scores += [s[:, :n_keys], s[:, n_keys:]]
    maxes = [jnp.max(s, axis=-1, keepdims=True) for s in scores]
    probs = [jnp.exp2((s - m).astype(BF16)) for s, m in zip(scores, maxes)]
    results = []
    for i, (_, _, v_blk, _, _) in enumerate(blocks):
        vb = v_blk.astype(BF16)
        key_low = lax.broadcasted_iota(jnp.int32, vb.shape, 1) < HEAD_DIM
        ones_low = jnp.where(key_low, 1.0, 0.0).astype(BF16)
        ones_high = jnp.where(key_low, 0.0, 1.0).astype(BF16)
        rhs = jnp.concatenate([
            jnp.concatenate([jnp.where(key_low, vb, zero), ones_low], axis=1),
            jnp.concatenate([jnp.where(key_low, zero, vb), ones_high], axis=1)], axis=0)
        lhs = jnp.concatenate([probs[2 * i], probs[2 * i + 1]], axis=1)
        results.append(jnp.dot(lhs, rhs, preferred_element_type=F32))
    states = []
    for i, (q_slabs, _, _, _, load_old) in enumerate(blocks):
        result, m_a, m_b = results[i], maxes[2 * i], maxes[2 * i + 1]
        old = None if load_old is None else load_old()
        state = []
        for g in range(len(q_slabs)):
            rows = slice(g * blk, (g + 1) * blk)
            u_new = result[rows, :LANES]
            l_new = result[rows, LANES:]
            m_new = jnp.where(low, m_a[rows], m_b[rows])
            if old is not None:
                u_run, m_run, l_run = old[g]
                m_tot = jnp.maximum(m_run, m_new)
                a = jnp.exp2(m_run - m_tot)
                b = jnp.exp2(m_new - m_tot)
                u_new = u_run * a + u_new * b
                l_new = l_run * a + l_new * b
                m_new = m_tot
            state.append((u_new, m_new, l_new))
        states.append(state)
    return states


def _attn_kernel(q_ref, k_ref, v_ref, o_ref, st4_ref, stt_ref):
    blk = ATTN_BLOCK
    seq = q_ref.shape[2]
    n_slabs = q_ref.shape[1]
    sub4 = seq // 4

    def key_mask(n_keys, shift):
        si = lax.broadcasted_iota(jnp.int32, (n_keys, blk), 0)
        qi = lax.broadcasted_iota(jnp.int32, (n_keys, blk), 1) + shift
        return jnp.where((si <= qi) & (si >= qi - blk), 0.0, MASKED).astype(BF16)

    stacked = n_slabs * blk
    query_onehot = (lax.broadcasted_iota(jnp.int32, (stacked, blk), 0) % blk
                    == lax.broadcasted_iota(jnp.int32, (stacked, blk), 1)).astype(BF16)
    band_bias = (query_onehot, key_mask(2 * blk, blk))
    causal_bias = (query_onehot, key_mask(blk, 0))

    def load_q(rows):
        return [q_ref[0, g, rows, :] for g in range(n_slabs)]

    def load_state(ref, rows):
        return [tuple(ref[a * n_slabs + g, rows, :] for a in range(STATE_ARRAYS)) for g in range(n_slabs)]

    def store_state(ref, rows, state):
        for g in range(n_slabs):
            for a in range(STATE_ARRAYS):
                ref[a * n_slabs + g, rows, :] = state[g][a]

    def store_out(rows, state):
        for g in range(n_slabs):
            o_ref[0, g, rows, :] = (state[g][STATE_U] / state[g][STATE_L]).astype(o_ref.dtype)

    def run(plans):
        states = _attn_blocks([(load_q(rows), k_ref[0, 0, keys, :], v_ref[0, 0, keys, :], bias, load_old)
                               for rows, keys, bias, load_old, _ in plans])
        for (rows, _, _, _, store), state in zip(plans, states):
            store(rows, state)

    def paired(n_blocks, plan, per_trip=2):
        assert n_blocks % per_trip == 0
        for i in range(n_blocks // per_trip):
            run([plan(per_trip * i + j) for j in range(per_trip)])

    def d16_plan(r16):
        rows = pl.ds(r16, blk, stride=16)
        dest = pl.ds((r16 % 4) * sub4 + r16 // 4, blk, stride=4)
        return rows, rows, causal_bias, None, lambda _, state: store_state(st4_ref, dest, state)

    paired(16, d16_plan, per_trip=16)

    def to_token_order(rows, state):
        store_state(stt_ref, rows, state)

    def d4_first_plan(r4):
        rows = pl.ds(r4, blk, stride=4)
        src = pl.ds(r4 * sub4, blk)
        return rows, rows, causal_bias, lambda: load_state(st4_ref, src), to_token_order

    paired(4, d4_first_plan, per_trip=4)
    later_blocks = sub4 // blk - 1

    def d4_plan(idx):
        r4 = idx // later_blocks
        n = 1 + idx % later_blocks
        rows = pl.ds(r4 + 4 * blk * n, blk, stride=4)
        keys = pl.ds(r4 + 4 * blk * (n - 1), 2 * blk, stride=4)
        src = pl.ds(r4 * sub4 + n * blk, blk)
        return rows, keys, band_bias, lambda: load_state(st4_ref, src), to_token_order

    paired(4 * later_blocks, d4_plan, per_trip=3)

    def d1_plan(n):
        rows = pl.ds(n * blk, blk)
        if n == 0:
            return rows, rows, causal_bias, lambda: load_state(stt_ref, rows), store_out
        return rows, pl.ds((n - 1) * blk, 2 * blk), band_bias, lambda: load_state(stt_ref, rows), store_out

    last = seq // blk - 1
    run([d1_plan(0)])
    paired(last, lambda i: d1_plan(i + 1), per_trip=3)


def _attention(q, k, v):
    batch, _, seq, _ = q.shape
    slabs_per_pair = Q_SLABS // KV_PAIRS
    q_spec = pl.BlockSpec((1, slabs_per_pair, seq, LANES), lambda b, p: (b, p, 0, 0))
    kv_spec = pl.BlockSpec((1, 1, seq, LANES), lambda b, p: (b, p, 0, 0))
    state_shape = (STATE_ARRAYS * slabs_per_pair, seq, LANES)
    return pl.pallas_call(
        _attn_kernel,
        grid=(batch, KV_PAIRS),
        in_specs=[q_spec, kv_spec, kv_spec],
        out_specs=q_spec,
        out_shape=jax.ShapeDtypeStruct(q.shape, BF16),
        scratch_shapes=[pltpu.VMEM(state_shape, F32), pltpu.VMEM(state_shape, F32)],
        compiler_params=pltpu.CompilerParams(dimension_semantics=("parallel", "parallel"),
                                             vmem_limit_bytes=56 * MIB),
        name="attention",
    )(q, k, v)


def _split_bf16(x, pieces):
    parts = []
    for _ in range(pieces):
        part = x.astype(BF16)
        parts.append(part)
        x = x - part.astype(F32)
    return parts


def _ssd_kernel(xs_ref, b_ref, c_ref, dt_ref, z_ref, arow_ref, dskip_ref, norm_ref, tri_ref, expand_ref,
                y_ref, state_ref):
    chunk = SSD_CHUNK
    lanes = range(xs_ref.shape[0])
    heads_per_group = SSM_HEADS // SSM_GROUPS

    @pl.when(pl.program_id(1) == 0)
    def _():
        state_ref[...] = jnp.zeros(state_ref.shape, F32)

    row = lax.broadcasted_iota(jnp.int32, (chunk, chunk), 0)
    col = lax.broadcasted_iota(jnp.int32, (chunk, chunk), 1)
    causal = row >= col
    low = lax.broadcasted_iota(jnp.int32, (chunk, LANES), 1) < SSM_HEAD_DIM
    zero = jnp.zeros((), BF16)

    def group_cols(g):
        return slice(g * SSM_STATE, (g + 1) * SSM_STATE), slice(g * GROUP_WIDTH, (g + 1) * GROUP_WIDTH)

    dts = [dt_ref[i] for i in lanes]
    a3s = [jnp.dot(tri_ref[...], jnp.concatenate(_split_bf16(dt * arow_ref[...], 3), axis=1),
                   preferred_element_type=F32) for dt in dts]
    a_css = [a3[:, :LANES] + a3[:, LANES:2 * LANES] + a3[:, 2 * LANES:] for a3 in a3s]

    def expand(stat):
        return jnp.dot(jnp.concatenate(_split_bf16(stat, 2), axis=1), expand_ref[...], preferred_element_type=F32)

    w_states = [expand(jnp.exp2(a_cs[chunk - 1:chunk, :] - a_cs) * dt) for a_cs, dt in zip(a_css, dts)]
    scale_offs = [expand(jnp.exp2(a_cs)) for a_cs in a_css]
    source_ts = [(a_cs - jnp.log(dt) * LOG2_E).T for a_cs, dt in zip(a_css, dts)]

    cbs, y_offs, new_states = {}, {}, {}
    for i in lanes:
        xw_bf = (xs_ref[i] * w_states[i]).astype(BF16)
        for g in range(SSM_GROUPS):
            gcols, wide = group_cols(g)
            b_g = b_ref[i, :, gcols]
            c_g = c_ref[i, :, gcols]
            cbs[i, g] = lax.dot_general(c_g, b_g, (((1,), (1,)), ((), ())), preferred_element_type=F32)
            y_offs[i, g] = jnp.dot(c_g, state_ref[i, :, wide].astype(BF16), preferred_element_type=F32)
            b_t = b_g.astype(F32).T.astype(BF16)
            new_states[i, g] = jnp.dot(b_t, xw_bf[:, wide], preferred_element_type=F32)

    y_diags = {}
    for i in lanes:
        xs_bf = xs_ref[i].astype(BF16)
        for g in range(SSM_GROUPS):
            parts = []
            for pair in range(heads_per_group // 2):
                mats = []
                for e in range(2):
                    h = g * heads_per_group + 2 * pair + e
                    seg = a_css[i][:, h:h + 1] - source_ts[i][h:h + 1, :]
                    mats.append((cbs[i, g] * jnp.exp2(jnp.where(causal, seg, -jnp.inf))).astype(BF16))
                first = (g * heads_per_group + 2 * pair) * SSM_HEAD_DIM
                x_pair = xs_bf[:, first:first + 2 * SSM_HEAD_DIM]
                rhs = jnp.concatenate([jnp.where(low, x_pair, zero), jnp.where(low, zero, x_pair)], axis=0)
                parts.append(jnp.dot(jnp.concatenate(mats, axis=1), rhs, preferred_element_type=F32))
            y_diags[i, g] = jnp.concatenate(parts, axis=1)

    for i in lanes:
        chunk_decay = scale_offs[i][chunk - 1:chunk, :]
        y_parts = []
        for g in range(SSM_GROUPS):
            _, wide = group_cols(g)
            y_parts.append(y_diags[i, g] + y_offs[i, g] * scale_offs[i][:, wide])
            state_ref[i, :, wide] = state_ref[i, :, wide] * chunk_decay[:, wide] + new_states[i, g]
        y = jnp.concatenate(y_parts, axis=1) + dskip_ref[...] * xs_ref[i]
        y = y * jax.nn.silu(z_ref[i])
        normed = []
        for g in range(SSM_GROUPS):
            y_g = y[:, g * GROUP_WIDTH:(g + 1) * GROUP_WIDTH]
            normed.append(y_g * lax.rsqrt(jnp.mean(y_g * y_g, axis=-1, keepdims=True) + NORM_EPS))
        y_ref[i] = (jnp.concatenate(normed, axis=1) * norm_ref[...]).astype(y_ref.dtype)


def _ssd(xs, bm, cm, dt, z, arow, dskip, norm, tri, expand):
    batch, seq, _ = xs.shape

    def tile(width):
        return pl.BlockSpec((SSD_BATCH, SSD_CHUNK, width), lambda b, c: (b, c, 0))

    return pl.pallas_call(
        _ssd_kernel,
        grid=(batch // SSD_BATCH, seq // SSD_CHUNK),
        in_specs=[tile(SSM_WIDTH), tile(BC_WIDTH), tile(BC_WIDTH), tile(LANES), tile(SSM_WIDTH),
                  _resident((1, LANES)), _resident((1, SSM_WIDTH)), _resident((1, SSM_WIDTH)),
                  _resident((SSD_CHUNK, SSD_CHUNK)), _resident((2 * LANES, SSM_WIDTH))],
        out_specs=tile(SSM_WIDTH),
        out_shape=jax.ShapeDtypeStruct((batch, seq, SSM_WIDTH), BF16),
        scratch_shapes=[pltpu.VMEM((SSD_BATCH, SSM_STATE, SSM_WIDTH), F32)],
        compiler_params=pltpu.CompilerParams(dimension_semantics=("parallel", "arbitrary"),
                                             vmem_limit_bytes=56 * MIB),
        name="ssd",
    )(xs, bm, cm, dt, z, arow, dskip, norm, tri, expand)


def _out_ffn_kernel(x_ref, attn_ref, y_ref, wo_attn_ref, wo_ssm_ref, mixpost_ref, pre_ref, wg_ref, wu_ref, wd_ref,
                    post_ref, o_ref):
    def mixer_residual(t):
        rows = slice(t * FFN_ROWS, (t + 1) * FFN_ROWS)
        attn = jnp.concatenate([attn_ref[0, j, rows, :] for j in range(Q_SLABS)], axis=1)
        mixed = (jnp.dot(attn, wo_attn_ref[...], preferred_element_type=F32)
                 + jnp.dot(y_ref[0, rows, :], wo_ssm_ref[...], preferred_element_type=F32))
        return x_ref[0, rows, :] + _rmsnorm(mixed, mixpost_ref[...])

    def emit(t, out):
        o_ref[0, t * FFN_ROWS:(t + 1) * FFN_ROWS, :] = out

    tiles = [functools.partial(mixer_residual, t) for t in range(OUT_FFN_TILES_PER_STEP)]
    _swiglu_half_steps(tiles, pre_ref[...], wg_ref, wu_ref, wd_ref, post_ref[...], emit)


def _out_ffn(x3d, attn, y3d, wo_attn, w_out, mixpost, pre, wg, wu, wd, post):
    batch, seq, _ = x3d.shape
    rows = FFN_ROWS * OUT_FFN_TILES_PER_STEP
    row_spec = pl.BlockSpec((1, rows, D_MODEL), lambda b, j: (b, j, 0))
    slab_spec = pl.BlockSpec((1, Q_SLABS, rows, LANES), lambda b, j: (b, 0, j, 0))
    assert ATTN_WIDTH % SSM_WIDTH == 0
    ssm_rows = pl.BlockSpec((SSM_WIDTH, D_MODEL), lambda *_: (ATTN_WIDTH // SSM_WIDTH, 0),
                            pipeline_mode=pl.Buffered(1))
    return pl.pallas_call(
        _out_ffn_kernel,
        grid=(batch, seq // rows),
        in_specs=[row_spec, slab_spec, row_spec, _resident((ATTN_WIDTH, D_MODEL)), ssm_rows,
                  _resident((1, D_MODEL)), _resident((1, D_MODEL)), _resident((D_MODEL, D_FF)),
                  _resident((D_MODEL, D_FF)), _resident((D_FF, D_MODEL)), _resident((1, D_MODEL))],
        out_specs=row_spec,
        out_shape=jax.ShapeDtypeStruct((batch, seq, D_MODEL), F32),
        compiler_params=pltpu.CompilerParams(dimension_semantics=("parallel", "parallel"),
                                             vmem_limit_bytes=58 * MIB),
        name="out_ffn",
    )(x3d, attn, y3d, wo_attn, w_out, mixpost, pre, wg, wu, wd, post)


def _to_slab_head_order(w, axis):
    shape = w.shape
    split = shape[:axis] + (KV_PAIRS, 2, Q_PER_KV, HEAD_DIM) + shape[axis + 1:]
    return jnp.swapaxes(w.reshape(split), axis + 1, axis + 2).reshape(shape)


def _pad_lanes(v):
    return jnp.pad(v, [(0, 0)] * (v.ndim - 1) + [(0, LANES - v.shape[-1])])


def _layer(x, pos3d, invf, tri, expand, p):
    batch, seq, _ = x.shape
    tokens = batch * seq
    row = lambda v: v.reshape(1, -1)
    bf = lambda w: w.astype(BF16)

    x1 = _ffn(x.reshape(tokens, D_MODEL), row(p["ffn1_pre_norm"]), bf(p["ffn1_w_gate"]), bf(p["ffn1_w_up"]),
              bf(p["ffn1_w_down"]), row(p["ffn1_post_norm"])).reshape(batch, seq, D_MODEL)

    w_in = bf(p["w_in"])
    o_dt = ATTN_WIDTH + 2 * KV_WIDTH + CONV_CHANNELS + SSM_WIDTH
    q, k, v, xs, bm, cm, z, dt = _in_proj(
        x1, pos3d, invf, row(p["mix_pre_norm"]),
        _to_slab_head_order(w_in[:, :ATTN_WIDTH], 1), w_in, _pad_lanes(w_in[:, o_dt:]),
        p["conv_w"], row(p["conv_b"]), _pad_lanes(row(p["dt_bias"])))

    attn = _attention(q, k, v)

    arow = _pad_lanes(row(-jnp.exp(p["a_log"]) * LOG2_E))
    dskip = row(jnp.repeat(p["d_skip"], SSM_HEAD_DIM))
    y = _ssd(xs, bm, cm, dt, z, arow, dskip, row(p["ssm_norm"]), tri, expand)

    w_out = bf(p["w_out"])
    return _out_ffn(x1, attn, y, _to_slab_head_order(w_out[:ATTN_WIDTH], 0), w_out,
                    row(p["mix_post_norm"]), row(p["ffn2_pre_norm"]),
                    bf(p["ffn2_w_gate"]), bf(p["ffn2_w_up"]), bf(p["ffn2_w_down"]), row(p["ffn2_post_norm"]))


def kernel(x, positions, ffn1_pre_norm, ffn1_w_gate, ffn1_w_up, ffn1_w_down, ffn1_post_norm, mix_pre_norm, w_in, conv_w, conv_b, dt_bias, a_log, d_skip, ssm_norm, w_out, mix_post_norm, ffn2_pre_norm, ffn2_w_gate, ffn2_w_up, ffn2_w_down, ffn2_post_norm):
    params = dict(ffn1_pre_norm=ffn1_pre_norm, ffn1_w_gate=ffn1_w_gate, ffn1_w_up=ffn1_w_up,
                  ffn1_w_down=ffn1_w_down, ffn1_post_norm=ffn1_post_norm, mix_pre_norm=mix_pre_norm, w_in=w_in,
                  conv_w=conv_w, conv_b=conv_b, dt_bias=dt_bias, a_log=a_log, d_skip=d_skip, ssm_norm=ssm_norm,
                  w_out=w_out, mix_post_norm=mix_post_norm, ffn2_pre_norm=ffn2_pre_norm, ffn2_w_gate=ffn2_w_gate,
                  ffn2_w_up=ffn2_w_up, ffn2_w_down=ffn2_w_down, ffn2_post_norm=ffn2_post_norm)
    depth = w_in.shape[0]
    batch, seq, _ = x.shape
    inv_freq = ROPE_THETA ** (-jnp.arange(0, HEAD_DIM, 2, dtype=F32) / HEAD_DIM)
    invf = inv_freq.reshape(HEAD_DIM // 2, 1)
    pos3d = positions.reshape(batch, 1, seq)
    idx = jnp.arange(SSD_CHUNK)
    tri = (idx[:, None] >= idx[None, :]).astype(BF16)
    expand = (jnp.arange(2 * LANES)[:, None] % LANES == jnp.arange(SSM_WIDTH)[None, :] // SSM_HEAD_DIM).astype(BF16)
    for i in range(depth):
        x = _layer(x, pos3d, invf, tri, expand, {name: w[i] for name, w in params.items()})
    return x
```

```python
import functools
import math

import jax
import jax.numpy as jnp
from jax import lax
from jax.experimental import pallas as pl
from jax.experimental.pallas import tpu as pltpu

F32 = jnp.float32
BF16 = jnp.bfloat16

D_MODEL = 1024
D_FF = 2816
HEAD_DIM = 64
N_Q_HEADS = 16
N_KV_HEADS = 4
Q_PER_KV = N_Q_HEADS // N_KV_HEADS
ATTN_WIDTH = N_Q_HEADS * HEAD_DIM
KV_WIDTH = N_KV_HEADS * HEAD_DIM
ATTN_BLOCK = 128
ROPE_THETA = 10000.0
SSM_HEADS = 16
SSM_HEAD_DIM = 64
SSM_WIDTH = SSM_HEADS * SSM_HEAD_DIM
SSM_STATE = 128
SSM_GROUPS = 2
GROUP_WIDTH = SSM_WIDTH // SSM_GROUPS
BC_WIDTH = SSM_GROUPS * SSM_STATE
CONV_WIDTH = 4
CONV_CHANNELS = SSM_WIDTH + 2 * BC_WIDTH
SSD_CHUNK = 128
MACARON_WEIGHT = 0.5
NORM_EPS = 1e-6
LOG2_E = math.log2(math.e)

LANES = 128
CARRY_ROWS = 8
MIB = 1024 * 1024

FFN_ROWS = 512
FFN_TILES_PER_STEP = 2
OUT_FFN_TILES_PER_STEP = 2
FF_CHUNK = 256
assert D_FF % FF_CHUNK == 0
PROJ_ROWS = 512
SSD_BATCH = 8

Q_SLABS = ATTN_WIDTH // LANES
KV_SLABS = KV_WIDTH // LANES
KV_PAIRS = N_KV_HEADS // 2


def _rmsnorm(x, gain):
    return x * lax.rsqrt(jnp.mean(x * x, axis=-1, keepdims=True) + NORM_EPS) * gain


def _resident(shape):
    return pl.BlockSpec(shape, lambda *_: (0,) * len(shape), pipeline_mode=pl.Buffered(1))


def _swiglu_half_steps(load_tiles, pre, wg_ref, wu_ref, wd_ref, post, emit):
    finish = None
    x = load_tiles[0]()
    xn = _rmsnorm(x, pre).astype(BF16)
    for t in range(len(load_tiles)):
        acc = jnp.zeros(x.shape, F32)
        x_next = xn_next = None
        for c in range(D_FF // FF_CHUNK):
            cols = slice(c * FF_CHUNK, (c + 1) * FF_CHUNK)
            gate = jnp.dot(xn, wg_ref[:, cols], preferred_element_type=F32)
            up = jnp.dot(xn, wu_ref[:, cols], preferred_element_type=F32)
            hidden = (jax.nn.silu(gate) * up).astype(BF16)
            acc = acc + jnp.dot(hidden, wd_ref[cols, :], preferred_element_type=F32)
            if c == 1 and finish is not None:
                finish()
            if c == 2 and t + 1 < len(load_tiles):
                x_next = load_tiles[t + 1]()
                xn_next = _rmsnorm(x_next, pre).astype(BF16)
        finish = functools.partial(lambda t_, x_t, acc_t: emit(t_, x_t + MACARON_WEIGHT * _rmsnorm(acc_t, post)),
                                   t, x, acc)
        x, xn = x_next, xn_next
    finish()


def _ffn_kernel(x_ref, pre_ref, wg_ref, wu_ref, wd_ref, post_ref, o_ref):
    tiles = [functools.partial(lambda t: x_ref[t * FFN_ROWS:(t + 1) * FFN_ROWS, :], t)
             for t in range(FFN_TILES_PER_STEP)]

    def emit(t, out):
        o_ref[t * FFN_ROWS:(t + 1) * FFN_ROWS, :] = out

    _swiglu_half_steps(tiles, pre_ref[...], wg_ref, wu_ref, wd_ref, post_ref[...], emit)


def _ffn(x2d, pre, wg, wu, wd, post):
    tokens = x2d.shape[0]
    step_rows = FFN_ROWS * FFN_TILES_PER_STEP
    row_spec = pl.BlockSpec((step_rows, D_MODEL), lambda i: (i, 0))
    return pl.pallas_call(
        _ffn_kernel,
        grid=(tokens // step_rows,),
        in_specs=[row_spec, _resident((1, D_MODEL)), _resident((D_MODEL, D_FF)), _resident((D_MODEL, D_FF)),
                  _resident((D_FF, D_MODEL)), _resident((1, D_MODEL))],
        out_specs=row_spec,
        out_shape=jax.ShapeDtypeStruct((tokens, D_MODEL), F32),
        compiler_params=pltpu.CompilerParams(dimension_semantics=("parallel",), vmem_limit_bytes=48 * MIB),
        name="ffn",
    )(x2d, pre, wg, wu, wd, post)


def _softplus(x):
    return jnp.maximum(x, 0.0) + jnp.log1p(jnp.exp(-jnp.abs(x)))


def _inproj_kernel(x_ref, pos_ref, invf_ref, gain_ref, wq_ref, wk_ref, wv_ref, wx_ref, wz_ref, wdt_ref,
                   convw_ref, convb_ref, dtb_ref,
                   q_ref, k_ref, v_ref, xs_ref, b_ref, c_ref, z_ref, dt_ref, ubuf_ref):
    rows = x_ref.shape[1]
    n_slabs = CONV_CHANNELS // LANES

    @pl.when(pl.program_id(1) == 0)
    def _():
        ubuf_ref[:, 0:CARRY_ROWS, :] = jnp.zeros((n_slabs, CARRY_ROWS, LANES), F32)

    xn = _rmsnorm(x_ref[0], gain_ref[...]).astype(BF16)
    wide = 2 * LANES

    def project(w_ref, c):
        return jnp.dot(xn, w_ref[:, c * wide:(c + 1) * wide], preferred_element_type=F32)

    def conv_chunk(c, u):
        for j in (2 * c, 2 * c + 1):
            cols = slice(j * LANES, (j + 1) * LANES)
            ubuf_ref[j, CARRY_ROWS:CARRY_ROWS + rows, :] = u[:, (j % 2) * LANES:(j % 2 + 1) * LANES]
            conv = (convb_ref[:, cols]
                    + ubuf_ref[j, CARRY_ROWS:CARRY_ROWS + rows, :] * convw_ref[CONV_WIDTH - 1:CONV_WIDTH, cols])
            for tap in range(CONV_WIDTH - 1):
                start = CARRY_ROWS - (CONV_WIDTH - 1) + tap
                conv = conv + ubuf_ref[j, start:start + rows, :] * convw_ref[tap:tap + 1, cols]
            act = jax.nn.silu(conv)
            if j < SSM_WIDTH // LANES:
                xs_ref[0, :, cols] = act
            elif j < (SSM_WIDTH + BC_WIDTH) // LANES:
                b_ref[0, :, j * LANES - SSM_WIDTH:(j + 1) * LANES - SSM_WIDTH] = act.astype(BF16)
            else:
                first = SSM_WIDTH + BC_WIDTH
                c_ref[0, :, j * LANES - first:(j + 1) * LANES - first] = act.astype(BF16)
            ubuf_ref[j, 0:CARRY_ROWS, :] = ubuf_ref[j, rows:rows + CARRY_ROWS, :]

    def z_chunk(c):
        z_ref[0, :, c * wide:(c + 1) * wide] = project(wz_ref, c)

    u = project(wx_ref, 0)
    z_chunk(0)

    ang_t = invf_ref[...] * pos_ref[0].astype(F32)
    cos_t = jnp.cos(ang_t)
    sin_t = jnp.sin(ang_t)
    copies = LANES // HEAD_DIM
    cos = jnp.concatenate([cos_t, cos_t] * copies, axis=0).T
    sin = jnp.concatenate([-sin_t, sin_t] * copies, axis=0).T
    lane = lax.broadcasted_iota(jnp.int32, (rows, LANES), 1)
    first_half = (lane % HEAD_DIM) < (HEAD_DIM // 2)

    def rope(t):
        back = pltpu.roll(t, HEAD_DIM // 2, 1)
        fwd = pltpu.roll(t, LANES - HEAD_DIM // 2, 1)
        return t * cos + jnp.where(first_half, fwd, back) * sin

    def q_chunk(c):
        q = project(wq_ref, c)
        for half in range(2):
            q_ref[0, 2 * c + half] = rope(q[:, half * LANES:(half + 1) * LANES]) * (HEAD_DIM ** -0.5 * LOG2_E)

    def kv_chunk(c):
        k = project(wk_ref, c)
        v = project(wv_ref, c)
        for half in range(2):
            k_ref[0, 2 * c + half] = rope(k[:, half * LANES:(half + 1) * LANES])
            v_ref[0, 2 * c + half] = v[:, half * LANES:(half + 1) * LANES]

    dt_ref[0] = _softplus(jnp.dot(xn, wdt_ref[...], preferred_element_type=F32) + dtb_ref[...])
    n_conv = CONV_CHANNELS // wide
    fillers = ([functools.partial(q_chunk, c) for c in range(ATTN_WIDTH // wide)]
               + [functools.partial(kv_chunk, c) for c in range(KV_WIDTH // wide)]
               + [functools.partial(z_chunk, c) for c in range(1, SSM_WIDTH // wide)])
    for c in range(n_conv):
        u_next = project(wx_ref, c + 1) if c + 1 < n_conv else None
        conv_chunk(c, u)
        fillers.pop(0)()
        u = u_next
    for filler in fillers:
        filler()


def _in_proj(x3d, pos3d, invf, gain, wq, w_in, wdt, convw, convb, dtb):
    batch, seq, _ = x3d.shape
    rows = PROJ_ROWS

    def columns(start, width):
        assert start % width == 0
        return pl.BlockSpec((D_MODEL, width), lambda *_: (0, start // width), pipeline_mode=pl.Buffered(1))

    o_k = ATTN_WIDTH
    o_v = o_k + KV_WIDTH
    o_x = o_v + KV_WIDTH
    o_z = o_x + CONV_CHANNELS

    def tile(width):
        return pl.BlockSpec((1, rows, width), lambda b, j: (b, j, 0))

    def slabs(n):
        return pl.BlockSpec((1, n, rows, LANES), lambda b, j: (b, 0, j, 0))

    def out(width, dtype):
        return jax.ShapeDtypeStruct((batch, seq, width), dtype)

    def out_slabs(n):
        return jax.ShapeDtypeStruct((batch, n, seq, LANES), F32)

    return pl.pallas_call(
        _inproj_kernel,
        grid=(batch, seq // rows),
        in_specs=[tile(D_MODEL), pl.BlockSpec((1, 1, rows), lambda b, j: (b, 0, j)),
                  _resident((HEAD_DIM // 2, 1)), _resident((1, D_MODEL)),
                  _resident((D_MODEL, ATTN_WIDTH)), columns(o_k, KV_WIDTH), columns(o_v, KV_WIDTH),
                  columns(o_x, CONV_CHANNELS), columns(o_z, SSM_WIDTH),
                  _resident((D_MODEL, LANES)), _resident((CONV_WIDTH, CONV_CHANNELS)),
                  _resident((1, CONV_CHANNELS)), _resident((1, LANES))],
        out_specs=[slabs(Q_SLABS), slabs(KV_SLABS), slabs(KV_SLABS), tile(SSM_WIDTH), tile(BC_WIDTH),
                   tile(BC_WIDTH), tile(SSM_WIDTH), tile(LANES)],
        out_shape=[out_slabs(Q_SLABS), out_slabs(KV_SLABS), out_slabs(KV_SLABS), out(SSM_WIDTH, F32),
                   out(BC_WIDTH, BF16), out(BC_WIDTH, BF16), out(SSM_WIDTH, F32), out(LANES, F32)],
        scratch_shapes=[pltpu.VMEM((CONV_CHANNELS // LANES, rows + CARRY_ROWS, LANES), F32)],
        compiler_params=pltpu.CompilerParams(dimension_semantics=("parallel", "arbitrary"),
                                             vmem_limit_bytes=56 * MIB),
        name="in_proj",
    )(x3d, pos3d, invf, gain, wq, w_in, w_in, w_in, w_in, wdt, convw, convb, dtb)


STATE_ARRAYS = 3
MASKED = -2.0 ** 100
STATE_U, STATE_M, STATE_L = range(STATE_ARRAYS)


def _attn_blocks(blocks):
    blk = ATTN_BLOCK
    low = lax.broadcasted_iota(jnp.int32, (blk, LANES), 1) < HEAD_DIM
    zero = jnp.zeros((), BF16)

    scores = []
    for q_slabs, k_blk, _, (query_onehot, key_mask), _ in blocks:
        n_keys = k_blk.shape[0]
        kb = k_blk.astype(BF16)
        key_low = lax.broadcasted_iota(jnp.int32, kb.shape, 1) < HEAD_DIM
        keys = jnp.concatenate([jnp.concatenate([jnp.where(key_low, kb, zero), key_mask], axis=1),
                                jnp.concatenate([jnp.where(key_low, zero, kb), key_mask], axis=1)], axis=0)
        lhs = jnp.concatenate([jnp.concatenate([q.astype(BF16) for q in q_slabs], axis=0), query_onehot], axis=1)
        s = lax.dot_general(lhs, keys, (((1,), (1,)), ((), ())), preferred_element_type=F32)
        scores += [s[:, :n_keys], s[:, n_keys:]]
    maxes = [jnp.max(s, axis=-1, keepdims=True) for s in scores]
    probs = [jnp.exp2((s - m).astype(BF16)) for s, m in zip(scores, maxes)]
    results = []
    for i, (_, _, v_blk, _, _) in enumerate(blocks):
        vb = v_blk.astype(BF16)
        key_low = lax.broadcasted_iota(jnp.int32, vb.shape, 1) < HEAD_DIM
        ones_low = jnp.where(key_low, 1.0, 0.0).astype(BF16)
        ones_high = jnp.where(key_low, 0.0, 1.0).astype(BF16)
        rhs = jnp.concatenate([
            jnp.concatenate([jnp.where(key_low, vb, zero), ones_low], axis=1),
            jnp.concatenate([jnp.where(key_low, zero, vb), ones_high], axis=1)], axis=0)
        lhs = jnp.concatenate([probs[2 * i], probs[2 * i + 1]], axis=1)
        results.append(jnp.dot(lhs, rhs, preferred_element_type=F32))
    states = []
    for i, (q_slabs, _, _, _, load_old) in enumerate(blocks):
        result, m_a, m_b = results[i], maxes[2 * i], maxes[2 * i + 1]
        old = None if load_old is None else load_old()
        state = []
        for g in range(len(q_slabs)):
            rows = slice(g * blk, (g + 1) * blk)
            u_new = result[rows, :LANES]
            l_new = result[rows, LANES:]
            m_new = jnp.where(low, m_a[rows], m_b[rows])
            if old is not None:
                u_run, m_run, l_run = old[g]
                m_tot = jnp.maximum(m_run, m_new)
                a = jnp.exp2(m_run - m_tot)
                b = jnp.exp2(m_new - m_tot)
                u_new = u_run * a + u_new * b
                l_new = l_run * a + l_new * b
                m_new = m_tot
            state.append((u_new, m_new, l_new))
        states.append(state)
    return states


def _attn_kernel(q_ref, k_ref, v_ref, o_ref, st4_ref, stt_ref):
    blk = ATTN_BLOCK
    seq = q_ref.shape[2]
    n_slabs = q_ref.shape[1]
    sub4 = seq // 4

    def key_mask(n_keys, shift):
        si = lax.broadcasted_iota(jnp.int32, (n_keys, blk), 0)
        qi = lax.broadcasted_iota(jnp.int32, (n_keys, blk), 1) + shift
        return jnp.where((si <= qi) & (si >= qi - blk), 0.0, MASKED).astype(BF16)

    stacked = n_slabs * blk
    query_onehot = (lax.broadcasted_iota(jnp.int32, (stacked, blk), 0) % blk
                    == lax.broadcasted_iota(jnp.int32, (stacked, blk), 1)).astype(BF16)
    band_bias = (query_onehot, key_mask(2 * blk, blk))
    causal_bias = (query_onehot, key_mask(blk, 0))

    def load_q(rows):
        return [q_ref[0, g, rows, :] for g in range(n_slabs)]

    def load_state(ref, rows):
        return [tuple(ref[a * n_slabs + g, rows, :] for a in range(STATE_ARRAYS)) for g in range(n_slabs)]

    def store_state(ref, rows, state):
        for g in range(n_slabs):
            for a in range(STATE_ARRAYS):
                ref[a * n_slabs + g, rows, :] = state[g][a]

    def store_out(rows, state):
        for g in range(n_slabs):
            o_ref[0, g, rows, :] = (state[g][STATE_U] / state[g][STATE_L]).astype(o_ref.dtype)

    def run(plans):
        states = _attn_blocks([(load_q(rows), k_ref[0, 0, keys, :], v_ref[0, 0, keys, :], bias, load_old)
                               for rows, keys, bias, load_old, _ in plans])
        for (rows, _, _, _, store), state in zip(plans, states):
            store(rows, state)

    def paired(n_blocks, plan, per_trip=2):
        assert n_blocks % per_trip == 0
        for i in range(n_blocks // per_trip):
            run([plan(per_trip * i + j) for j in range(per_trip)])

    def d16_plan(r16):
        rows = pl.ds(r16, blk, stride=16)
        dest = pl.ds((r16 % 4) * sub4 + r16 // 4, blk, stride=4)
        return rows, rows, causal_bias, None, lambda _, state: store_state(st4_ref, dest, state)

    paired(16, d16_plan, per_trip=16)

    def to_token_order(rows, state):
        store_state(stt_ref, rows, state)

    def d4_first_plan(r4):
        rows = pl.ds(r4, blk, stride=4)
        src = pl.ds(r4 * sub4, blk)
        return rows, rows, causal_bias, lambda: load_state(st4_ref, src), to_token_order

    paired(4, d4_first_plan, per_trip=4)
    later_blocks = sub4 // blk - 1

    def d4_plan(idx):
        r4 = idx // later_blocks
        n = 1 + idx % later_blocks
        rows = pl.ds(r4 + 4 * blk * n, blk, stride=4)
        keys = pl.ds(r4 + 4 * blk * (n - 1), 2 * blk, stride=4)
        src = pl.ds(r4 * sub4 + n * blk, blk)
        return rows, keys, band_bias, lambda: load_state(st4_ref, src), to_token_order

    paired(4 * later_blocks, d4_plan, per_trip=3)

    def d1_plan(n):
        rows = pl.ds(n * blk, blk)
        if n == 0:
            return rows, rows, causal_bias, lambda: load_state(stt_ref, rows), store_out
        return rows, pl.ds((n - 1) * blk, 2 * blk), band_bias, lambda: load_state(stt_ref, rows), store_out

    last = seq // blk - 1
    run([d1_plan(0)])
    paired(last, lambda i: d1_plan(i + 1), per_trip=3)


def _attention(q, k, v):
    batch, _, seq, _ = q.shape
    slabs_per_pair = Q_SLABS // KV_PAIRS
    q_spec = pl.BlockSpec((1, slabs_per_pair, seq, LANES), lambda b, p: (b, p, 0, 0))
    kv_spec = pl.BlockSpec((1, 1, seq, LANES), lambda b, p: (b, p, 0, 0))
    state_shape = (STATE_ARRAYS * slabs_per_pair, seq, LANES)
    return pl.pallas_call(
        _attn_kernel,
        grid=(batch, KV_PAIRS),
        in_specs=[q_spec, kv_spec, kv_spec],
        out_specs=q_spec,
        out_shape=jax.ShapeDtypeStruct(q.shape, BF16),
        scratch_shapes=[pltpu.VMEM(state_shape, F32), pltpu.VMEM(state_shape, F32)],
        compiler_params=pltpu.CompilerParams(dimension_semantics=("parallel", "parallel"),
                                             vmem_limit_bytes=56 * MIB),
        name="attention",
    )(q, k, v)


def _split_bf16(x, pieces):
    parts = []
    for _ in range(pieces):
        part = x.astype(BF16)
        parts.append(part)
        x = x - part.astype(F32)
    return parts


def _ssd_kernel(xs_ref, b_ref, c_ref, dt_ref, z_ref, arow_ref, dskip_ref, norm_ref, tri_ref, expand_ref,
                y_ref, state_ref):
    chunk = SSD_CHUNK
    lanes = range(xs_ref.shape[0])
    heads_per_group = SSM_HEADS // SSM_GROUPS

    @pl.when(pl.program_id(1) == 0)
    def _():
        state_ref[...] = jnp.zeros(state_ref.shape, F32)

    row = lax.broadcasted_iota(jnp.int32, (chunk, chunk), 0)
    col = lax.broadcasted_iota(jnp.int32, (chunk, chunk), 1)
    causal = row >= col
    low = lax.broadcasted_iota(jnp.int32, (chunk, LANES), 1) < SSM_HEAD_DIM
    zero = jnp.zeros((), BF16)

    def group_cols(g):
        return slice(g * SSM_STATE, (g + 1) * SSM_STATE), slice(g * GROUP_WIDTH, (g + 1) * GROUP_WIDTH)

    dts = [dt_ref[i] for i in lanes]
    a3s = [jnp.dot(tri_ref[...], jnp.concatenate(_split_bf16(dt * arow_ref[...], 3), axis=1),
                   preferred_element_type=F32) for dt in dts]
    a_css = [a3[:, :LANES] + a3[:, LANES:2 * LANES] + a3[:, 2 * LANES:] for a3 in a3s]

    def expand(stat):
        return jnp.dot(jnp.concatenate(_split_bf16(stat, 2), axis=1), expand_ref[...], preferred_element_type=F32)

    w_states = [expand(jnp.exp2(a_cs[chunk - 1:chunk, :] - a_cs) * dt) for a_cs, dt in zip(a_css, dts)]
    scale_offs = [expand(jnp.exp2(a_cs)) for a_cs in a_css]
    source_ts = [(a_cs - jnp.log(dt) * LOG2_E).T for a_cs, dt in zip(a_css, dts)]

    cbs, y_offs, new_states = {}, {}, {}
    for i in lanes:
        xw_bf = (xs_ref[i] * w_states[i]).astype(BF16)
        for g in range(SSM_GROUPS):
            gcols, wide = group_cols(g)
            b_g = b_ref[i, :, gcols]
            c_g = c_ref[i, :, gcols]
            cbs[i, g] = lax.dot_general(c_g, b_g, (((1,), (1,)), ((), ())), preferred_element_type=F32)
            y_offs[i, g] = jnp.dot(c_g, state_ref[i, :, wide].astype(BF16), preferred_element_type=F32)
            b_t = b_g.astype(F32).T.astype(BF16)
            new_states[i, g] = jnp.dot(b_t, xw_bf[:, wide], preferred_element_type=F32)

    y_diags = {}
    for i in lanes:
        xs_bf = xs_ref[i].astype(BF16)
        for g in range(SSM_GROUPS):
            parts = []
            for pair in range(heads_per_group // 2):
                mats = []
                for e in range(2):
                    h = g * heads_per_group + 2 * pair + e
                    seg = a_css[i][:, h:h + 1] - source_ts[i][h:h + 1, :]
                    mats.append((cbs[i, g] * jnp.exp2(jnp.where(causal, seg, -jnp.inf))).astype(BF16))
                first = (g * heads_per_group + 2 * pair) * SSM_HEAD_DIM
                x_pair = xs_bf[:, first:first + 2 * SSM_HEAD_DIM]
                rhs = jnp.concatenate([jnp.where(low, x_pair, zero), jnp.where(low, zero, x_pair)], axis=0)
                parts.append(jnp.dot(jnp.concatenate(mats, axis=1), rhs, preferred_element_type=F32))
            y_diags[i, g] = jnp.concatenate(parts, axis=1)

    for i in lanes:
        chunk_decay = scale_offs[i][chunk - 1:chunk, :]
        y_parts = []
        for g in range(SSM_GROUPS):
            _, wide = group_cols(g)
            y_parts.append(y_diags[i, g] + y_offs[i, g] * scale_offs[i][:, wide])
            state_ref[i, :, wide] = state_ref[i, :, wide] * chunk_decay[:, wide] + new_states[i, g]
        y = jnp.concatenate(y_parts, axis=1) + dskip_ref[...] * xs_ref[i]
        y = y * jax.nn.silu(z_ref[i])
        normed = []
        for g in range(SSM_GROUPS):
            y_g = y[:, g * GROUP_WIDTH:(g + 1) * GROUP_WIDTH]
            normed.append(y_g * lax.rsqrt(jnp.mean(y_g * y_g, axis=-1, keepdims=True) + NORM_EPS))
        y_ref[i] = (jnp.concatenate(normed, axis=1) * norm_ref[...]).astype(y_ref.dtype)


def _ssd(xs, bm, cm, dt, z, arow, dskip, norm, tri, expand):
    batch, seq, _ = xs.shape

    def tile(width):
        return pl.BlockSpec((SSD_BATCH, SSD_CHUNK, width), lambda b, c: (b, c, 0))

    return pl.pallas_call(
        _ssd_kernel,
        grid=(batch // SSD_BATCH, seq // SSD_CHUNK),
        in_specs=[tile(SSM_WIDTH), tile(BC_WIDTH), tile(BC_WIDTH), tile(LANES), tile(SSM_WIDTH),
                  _resident((1, LANES)), _resident((1, SSM_WIDTH)), _resident((1, SSM_WIDTH)),
                  _resident((SSD_CHUNK, SSD_CHUNK)), _resident((2 * LANES, SSM_WIDTH))],
        out_specs=tile(SSM_WIDTH),
        out_shape=jax.ShapeDtypeStruct((batch, seq, SSM_WIDTH), BF16),
        scratch_shapes=[pltpu.VMEM((SSD_BATCH, SSM_STATE, SSM_WIDTH), F32)],
        compiler_params=pltpu.CompilerParams(dimension_semantics=("parallel", "arbitrary"),
                                             vmem_limit_bytes=56 * MIB),
        name="ssd",
    )(xs, bm, cm, dt, z, arow, dskip, norm, tri, expand)


def _out_ffn_kernel(x_ref, attn_ref, y_ref, wo_attn_ref, wo_ssm_ref, mixpost_ref, pre_ref, wg_ref, wu_ref, wd_ref,
                    post_ref, o_ref):
    def mixer_residual(t):
        rows = slice(t * FFN_ROWS, (t + 1) * FFN_ROWS)
        attn = jnp.concatenate([attn_ref[0, j, rows, :] for j in range(Q_SLABS)], axis=1)
        mixed = (jnp.dot(attn, wo_attn_ref[...], preferred_element_type=F32)
                 + jnp.dot(y_ref[0, rows, :], wo_ssm_ref[...], preferred_element_type=F32))
        return x_ref[0, rows, :] + _rmsnorm(mixed, mixpost_ref[...])

    def emit(t, out):
        o_ref[0, t * FFN_ROWS:(t + 1) * FFN_ROWS, :] = out

    tiles = [functools.partial(mixer_residual, t) for t in range(OUT_FFN_TILES_PER_STEP)]
    _swiglu_half_steps(tiles, pre_ref[...], wg_ref, wu_ref, wd_ref, post_ref[...], emit)


def _out_ffn(x3d, attn, y3d, wo_attn, w_out, mixpost, pre, wg, wu, wd, post):
    batch, seq, _ = x3d.shape
    rows = FFN_ROWS * OUT_FFN_TILES_PER_STEP
    row_spec = pl.BlockSpec((1, rows, D_MODEL), lambda b, j: (b, j, 0))
    slab_spec = pl.BlockSpec((1, Q_SLABS, rows, LANES), lambda b, j: (b, 0, j, 0))
    assert ATTN_WIDTH % SSM_WIDTH == 0
    ssm_rows = pl.BlockSpec((SSM_WIDTH, D_MODEL), lambda *_: (ATTN_WIDTH // SSM_WIDTH, 0),
                            pipeline_mode=pl.Buffered(1))
    return pl.pallas_call(
        _out_ffn_kernel,
        grid=(batch, seq // rows),
        in_specs=[row_spec, slab_spec, row_spec, _resident((ATTN_WIDTH, D_MODEL)), ssm_rows,
                  _resident((1, D_MODEL)), _resident((1, D_MODEL)), _resident((D_MODEL, D_FF)),
                  _resident((D_MODEL, D_FF)), _resident((D_FF, D_MODEL)), _resident((1, D_MODEL))],
        out_specs=row_spec,
        out_shape=jax.ShapeDtypeStruct((batch, seq, D_MODEL), F32),
        compiler_params=pltpu.CompilerParams(dimension_semantics=("parallel", "parallel"),
                                             vmem_limit_bytes=58 * MIB),
        name="out_ffn",
    )(x3d, attn, y3d, wo_attn, w_out, mixpost, pre, wg, wu, wd, post)


def _to_slab_head_order(w, axis):
    shape = w.shape
    split = shape[:axis] + (KV_PAIRS, 2, Q_PER_KV, HEAD_DIM) + shape[axis + 1:]
    return jnp.swapaxes(w.reshape(split), axis + 1, axis + 2).reshape(shape)


def _pad_lanes(v):
    return jnp.pad(v, [(0, 0)] * (v.ndim - 1) + [(0, LANES - v.shape[-1])])


def _layer(x, pos3d, invf, tri, expand, p):
    batch, seq, _ = x.shape
    tokens = batch * seq
    row = lambda v: v.reshape(1, -1)
    bf = lambda w: w.astype(BF16)

    x1 = _ffn(x.reshape(tokens, D_MODEL), row(p["ffn1_pre_norm"]), bf(p["ffn1_w_gate"]), bf(p["ffn1_w_up"]),
              bf(p["ffn1_w_down"]), row(p["ffn1_post_norm"])).reshape(batch, seq, D_MODEL)

    w_in = bf(p["w_in"])
    o_dt = ATTN_WIDTH + 2 * KV_WIDTH + CONV_CHANNELS + SSM_WIDTH
    q, k, v, xs, bm, cm, z, dt = _in_proj(
        x1, pos3d, invf, row(p["mix_pre_norm"]),
        _to_slab_head_order(w_in[:, :ATTN_WIDTH], 1), w_in, _pad_lanes(w_in[:, o_dt:]),
        p["conv_w"], row(p["conv_b"]), _pad_lanes(row(p["dt_bias"])))

    attn = _attention(q, k, v)

    arow = _pad_lanes(row(-jnp.exp(p["a_log"]) * LOG2_E))
    dskip = row(jnp.repeat(p["d_skip"], SSM_HEAD_DIM))
    y = _ssd(xs, bm, cm, dt, z, arow, dskip, row(p["ssm_norm"]), tri, expand)

    w_out = bf(p["w_out"])
    return _out_ffn(x1, attn, y, _to_slab_head_order(w_out[:ATTN_WIDTH], 0), w_out,
                    row(p["mix_post_norm"]), row(p["ffn2_pre_norm"]),
                    bf(p["ffn2_w_gate"]), bf(p["ffn2_w_up"]), bf(p["ffn2_w_down"]), row(p["ffn2_post_norm"]))


def kernel(x, positions, ffn1_pre_norm, ffn1_w_gate, ffn1_w_up, ffn1_w_down, ffn1_post_norm, mix_pre_norm, w_in, conv_w, conv_b, dt_bias, a_log, d_skip, ssm_norm, w_out, mix_post_norm, ffn2_pre_norm, ffn2_w_gate, ffn2_w_up, ffn2_w_down, ffn2_post_norm):
    params = dict(ffn1_pre_norm=ffn1_pre_norm, ffn1_w_gate=ffn1_w_gate, ffn1_w_up=ffn1_w_up,
                  ffn1_w_down=ffn1_w_down, ffn1_post_norm=ffn1_post_norm, mix_pre_norm=mix_pre_norm, w_in=w_in,
                  conv_w=conv_w, conv_b=conv_b, dt_bias=dt_bias, a_log=a_log, d_skip=d_skip, ssm_norm=ssm_norm,
                  w_out=w_out, mix_post_norm=mix_post_norm, ffn2_pre_norm=ffn2_pre_norm, ffn2_w_gate=ffn2_w_gate,
                  ffn2_w_up=ffn2_w_up, ffn2_w_down=ffn2_w_down, ffn2_post_norm=ffn2_post_norm)
    depth = w_in.shape[0]
    batch, seq, _ = x.shape
    inv_freq = ROPE_THETA ** (-jnp.arange(0, HEAD_DIM, 2, dtype=F32) / HEAD_DIM)
    invf = inv_freq.reshape(HEAD_DIM // 2, 1)
    pos3d = positions.reshape(batch, 1, seq)
    idx = jnp.arange(SSD_CHUNK)
    tri = (idx[:, None] >= idx[None, :]).astype(BF16)
    expand = (jnp.arange(2 * LANES)[:, None] % LANES == jnp.arange(SSM_WIDTH)[None, :] // SSM_HEAD_DIM).astype(BF16)
    for i in range(depth):
        x = _layer(x, pos3d, invf, tri, expand, {name: w[i] for name, w in params.items()})
    return x
```

```python
import functools
import math

import jax
import jax.numpy as jnp
from jax import lax
from jax.experimental import pallas as pl
from jax.experimental.pallas import tpu as pltpu

F32 = jnp.float32
BF16 = jnp.bfloat16

D_MODEL = 1024
D_FF = 2816
HEAD_DIM = 64
N_Q_HEADS = 16
N_KV_HEADS = 4
Q_PER_KV = N_Q_HEADS // N_KV_HEADS
ATTN_WIDTH = N_Q_HEADS * HEAD_DIM
KV_WIDTH = N_KV_HEADS * HEAD_DIM
ATTN_BLOCK = 128
ROPE_THETA = 10000.0
SSM_HEADS = 16
SSM_HEAD_DIM = 64
SSM_WIDTH = SSM_HEADS * SSM_HEAD_DIM
SSM_STATE = 128
SSM_GROUPS = 2
GROUP_WIDTH = SSM_WIDTH // SSM_GROUPS
BC_WIDTH = SSM_GROUPS * SSM_STATE
CONV_WIDTH = 4
CONV_CHANNELS = SSM_WIDTH + 2 * BC_WIDTH
SSD_CHUNK = 128
MACARON_WEIGHT = 0.5
NORM_EPS = 1e-6
LOG2_E = math.log2(math.e)

LANES = 128
CARRY_ROWS = 8
MIB = 1024 * 1024

FFN_ROWS = 512
FFN_TILES_PER_STEP = 2
OUT_FFN_TILES_PER_STEP = 2
FF_CHUNK = 256
assert D_FF % FF_CHUNK == 0
PROJ_ROWS = 512
SSD_BATCH = 8

Q_SLABS = ATTN_WIDTH // LANES
KV_SLABS = KV_WIDTH // LANES
KV_PAIRS = N_KV_HEADS // 2


def _rmsnorm(x, gain):
    return x * lax.rsqrt(jnp.mean(x * x, axis=-1, keepdims=True) + NORM_EPS) * gain


def _resident(shape):
    return pl.BlockSpec(shape, lambda *_: (0,) * len(shape), pipeline_mode=pl.Buffered(1))


def _swiglu_half_steps(load_tiles, pre, wg_ref, wu_ref, wd_ref, post, emit):
    finish = None
    x = load_tiles[0]()
    xn = _rmsnorm(x, pre).astype(BF16)
    for t in range(len(load_tiles)):
        acc = jnp.zeros(x.shape, F32)
        x_next = xn_next = None
        for c in range(D_FF // FF_CHUNK):
            cols = slice(c * FF_CHUNK, (c + 1) * FF_CHUNK)
            gate = jnp.dot(xn, wg_ref[:, cols], preferred_element_type=F32)
            up = jnp.dot(xn, wu_ref[:, cols], preferred_element_type=F32)
            hidden = (jax.nn.silu(gate) * up).astype(BF16)
            acc = acc + jnp.dot(hidden, wd_ref[cols, :], preferred_element_type=F32)
            if c == 1 and finish is not None:
                finish()
            if c == 2 and t + 1 < len(load_tiles):
                x_next = load_tiles[t + 1]()
                xn_next = _rmsnorm(x_next, pre).astype(BF16)
        finish = functools.partial(lambda t_, x_t, acc_t: emit(t_, x_t + MACARON_WEIGHT * _rmsnorm(acc_t, post)),
                                   t, x, acc)
        x, xn = x_next, xn_next
    finish()


def _ffn_kernel(x_ref, pre_ref, wg_ref, wu_ref, wd_ref, post_ref, o_ref):
    tiles = [functools.partial(lambda t: x_ref[t * FFN_ROWS:(t + 1) * FFN_ROWS, :], t)
             for t in range(FFN_TILES_PER_STEP)]

    def emit(t, out):
        o_ref[t * FFN_ROWS:(t + 1) * FFN_ROWS, :] = out

    _swiglu_half_steps(tiles, pre_ref[...], wg_ref, wu_ref, wd_ref, post_ref[...], emit)


def _ffn(x2d, pre, wg, wu, wd, post):
    tokens = x2d.shape[0]
    step_rows = FFN_ROWS * FFN_TILES_PER_STEP
    row_spec = pl.BlockSpec((step_rows, D_MODEL), lambda i: (i, 0))
    return pl.pallas_call(
        _ffn_kernel,
        grid=(tokens // step_rows,),
        in_specs=[row_spec, _resident((1, D_MODEL)), _resident((D_MODEL, D_FF)), _resident((D_MODEL, D_FF)),
                  _resident((D_FF, D_MODEL)), _resident((1, D_MODEL))],
        out_specs=row_spec,
        out_shape=jax.ShapeDtypeStruct((tokens, D_MODEL), F32),
        compiler_params=pltpu.CompilerParams(dimension_semantics=("parallel",), vmem_limit_bytes=48 * MIB),
        name="ffn",
    )(x2d, pre, wg, wu, wd, post)


def _softplus(x):
    return jnp.maximum(x, 0.0) + jnp.log1p(jnp.exp(-jnp.abs(x)))


def _inproj_kernel(x_ref, pos_ref, invf_ref, gain_ref, wq_ref, wk_ref, wv_ref, wx_ref, wz_ref, wdt_ref,
                   convw_ref, convb_ref, dtb_ref,
                   q_ref, k_ref, v_ref, xs_ref, b_ref, c_ref, z_ref, dt_ref, ubuf_ref):
    rows = x_ref.shape[1]
    n_slabs = CONV_CHANNELS // LANES

    @pl.when(pl.program_id(1) == 0)
    def _():
        ubuf_ref[:, 0:CARRY_ROWS, :] = jnp.zeros((n_slabs, CARRY_ROWS, LANES), F32)

    xn = _rmsnorm(x_ref[0], gain_ref[...]).astype(BF16)
    wide = 2 * LANES

    def project(w_ref, c):
        return jnp.dot(xn, w_ref[:, c * wide:(c + 1) * wide], preferred_element_type=F32)

    def conv_chunk(c, u):
        for j in (2 * c, 2 * c + 1):
            cols = slice(j * LANES, (j + 1) * LANES)
            ubuf_ref[j, CARRY_ROWS:CARRY_ROWS + rows, :] = u[:, (j % 2) * LANES:(j % 2 + 1) * LANES]
            conv = (convb_ref[:, cols]
                    + ubuf_ref[j, CARRY_ROWS:CARRY_ROWS + rows, :] * convw_ref[CONV_WIDTH - 1:CONV_WIDTH, cols])
            for tap in range(CONV_WIDTH - 1):
                start = CARRY_ROWS - (CONV_WIDTH - 1) + tap
                conv = conv + ubuf_ref[j, start:start + rows, :] * convw_ref[tap:tap + 1, cols]
            act = jax.nn.silu(conv)
            if j < SSM_WIDTH // LANES:
                xs_ref[0, :, cols] = act
            elif j < (SSM_WIDTH + BC_WIDTH) // LANES:
                b_ref[0, :, j * LANES - SSM_WIDTH:(j + 1) * LANES - SSM_WIDTH] = act.astype(BF16)
            else:
                first = SSM_WIDTH + BC_WIDTH
                c_ref[0, :, j * LANES - first:(j + 1) * LANES - first] = act.astype(BF16)
            ubuf_ref[j, 0:CARRY_ROWS, :] = ubuf_ref[j, rows:rows + CARRY_ROWS, :]

    def z_chunk(c):
        z_ref[0, :, c * wide:(c + 1) * wide] = project(wz_ref, c)

    u = project(wx_ref, 0)
    z_chunk(0)

    ang_t = invf_ref[...] * pos_ref[0].astype(F32)
    cos_t = jnp.cos(ang_t)
    sin_t = jnp.sin(ang_t)
    copies = LANES // HEAD_DIM
    cos = jnp.concatenate([cos_t, cos_t] * copies, axis=0).T
    sin = jnp.concatenate([-sin_t, sin_t] * copies, axis=0).T
    lane = lax.broadcasted_iota(jnp.int32, (rows, LANES), 1)
    first_half = (lane % HEAD_DIM) < (HEAD_DIM // 2)

    def rope(t):
        back = pltpu.roll(t, HEAD_DIM // 2, 1)
        fwd = pltpu.roll(t, LANES - HEAD_DIM // 2, 1)
        return t * cos + jnp.where(first_half, fwd, back) * sin

    def q_chunk(c):
        q = project(wq_ref, c)
        for half in range(2):
            q_ref[0, 2 * c + half] = rope(q[:, half * LANES:(half + 1) * LANES]) * (HEAD_DIM ** -0.5 * LOG2_E)

    def kv_chunk(c):
        k = project(wk_ref, c)
        v = project(wv_ref, c)
        for half in range(2):
            k_ref[0, 2 * c + half] = rope(k[:, half * LANES:(half + 1) * LANES])
            v_ref[0, 2 * c + half] = v[:, half * LANES:(half + 1) * LANES]

    dt_ref[0] = _softplus(jnp.dot(xn, wdt_ref[...], preferred_element_type=F32) + dtb_ref[...])
    n_conv = CONV_CHANNELS // wide
    fillers = ([functools.partial(q_chunk, c) for c in range(ATTN_WIDTH // wide)]
               + [functools.partial(kv_chunk, c) for c in range(KV_WIDTH // wide)]
               + [functools.partial(z_chunk, c) for c in range(1, SSM_WIDTH // wide)])
    for c in range(n_conv):
        u_next = project(wx_ref, c + 1) if c + 1 < n_conv else None
        conv_chunk(c, u)
        fillers.pop(0)()
        u = u_next
    for filler in fillers:
        filler()


def _in_proj(x3d, pos3d, invf, gain, wq, w_in, wdt, convw, convb, dtb):
    batch, seq, _ = x3d.shape
    rows = PROJ_ROWS

    def columns(start, width):
        assert start % width == 0
        return pl.BlockSpec((D_MODEL, width), lambda *_: (0, start // width), pipeline_mode=pl.Buffered(1))

    o_k = ATTN_WIDTH
    o_v = o_k + KV_WIDTH
    o_x = o_v + KV_WIDTH
    o_z = o_x + CONV_CHANNELS

    def tile(width):
        return pl.BlockSpec((1, rows, width), lambda b, j: (b, j, 0))

    def slabs(n):
        return pl.BlockSpec((1, n, rows, LANES), lambda b, j: (b, 0, j, 0))

    def out(width, dtype):
        return jax.ShapeDtypeStruct((batch, seq, width), dtype)

    def out_slabs(n):
        return jax.ShapeDtypeStruct((batch, n, seq, LANES), F32)

    return pl.pallas_call(
        _inproj_kernel,
        grid=(batch, seq // rows),
        in_specs=[tile(D_MODEL), pl.BlockSpec((1, 1, rows), lambda b, j: (b, 0, j)),
                  _resident((HEAD_DIM // 2, 1)), _resident((1, D_MODEL)),
                  _resident((D_MODEL, ATTN_WIDTH)), columns(o_k, KV_WIDTH), columns(o_v, KV_WIDTH),
                  columns(o_x, CONV_CHANNELS), columns(o_z, SSM_WIDTH),
                  _resident((D_MODEL, LANES)), _resident((CONV_WIDTH, CONV_CHANNELS)),
                  _resident((1, CONV_CHANNELS)), _resident((1, LANES))],
        out_specs=[slabs(Q_SLABS), slabs(KV_SLABS), slabs(KV_SLABS), tile(SSM_WIDTH), tile(BC_WIDTH),
                   tile(BC_WIDTH), tile(SSM_WIDTH), tile(LANES)],
        out_shape=[out_slabs(Q_SLABS), out_slabs(KV_SLABS), out_slabs(KV_SLABS), out(SSM_WIDTH, F32),
                   out(BC_WIDTH, BF16), out(BC_WIDTH, BF16), out(SSM_WIDTH, F32), out(LANES, F32)],
        scratch_shapes=[pltpu.VMEM((CONV_CHANNELS // LANES, rows + CARRY_ROWS, LANES), F32)],
        compiler_params=pltpu.CompilerParams(dimension_semantics=("parallel", "arbitrary"),
                                             vmem_limit_bytes=56 * MIB),
        name="in_proj",
    )(x3d, pos3d, invf, gain, wq, w_in, w_in, w_in, w_in, wdt, convw, convb, dtb)


STATE_ARRAYS = 3
MASKED = -2.0 ** 100
STATE_U, STATE_M, STATE_L = range(STATE_ARRAYS)


def _attn_blocks(blocks):
    blk = ATTN_BLOCK
    low = lax.broadcasted_iota(jnp.int32, (blk, LANES), 1) < HEAD_DIM
    zero = jnp.zeros((), BF16)

    scores = []
    for q_slabs, k_blk, _, (query_onehot, key_mask), _ in blocks:
        n_keys = k_blk.shape[0]
        kb = k_blk.astype(BF16)
        key_low = lax.broadcasted_iota(jnp.int32, kb.shape, 1) < HEAD_DIM
        keys = jnp.concatenate([jnp.concatenate([jnp.where(key_low, kb, zero), key_mask], axis=1),
                                jnp.concatenate([jnp.where(key_low, zero, kb), key_mask], axis=1)], axis=0)
        lhs = jnp.concatenate([jnp.concatenate([q.astype(BF16) for q in q_slabs], axis=0), query_onehot], axis=1)
        s = lax.dot_general(lhs, keys, (((1,), (1,)), ((), ())), preferred_element_type=F32)
        scores += [s[:, :n_keys], s[:, n_keys:]]
    maxes = [jnp.max(s, axis=-1, keepdims=True) for s in scores]
    probs = [jnp.exp2((s - m).astype(BF16)) for s, m in zip(scores, maxes)]
    results = []
    for i, (_, _, v_blk, _, _) in enumerate(blocks):
        vb = v_blk.astype(BF16)
        key_low = lax.broadcasted_iota(jnp.int32, vb.shape, 1) < HEAD_DIM
        ones_low = jnp.where(key_low, 1.0, 0.0).astype(BF16)
        ones_high = jnp.where(key_low, 0.0, 1.0).astype(BF16)
        rhs = jnp.concatenate([
            jnp.concatenate([jnp.where(key_low, vb, zero), ones_low], axis=1),
            jnp.concatenate([jnp.where(key_low, zero, vb), ones_high], axis=1)], axis=0)
        lhs = jnp.concatenate([probs[2 * i], probs[2 * i + 1]], axis=1)
        results.append(jnp.dot(lhs, rhs, preferred_element_type=F32))
    states = []
    for i, (q_slabs, _, _, _, load_old) in enumerate(blocks):
        result, m_a, m_b = results[i], maxes[2 * i], maxes[2 * i + 1]
        old = None if load_old is None else load_old()
        state = []
        for g in range(len(q_slabs)):
            rows = slice(g * blk, (g + 1) * blk)
            u_new = result[rows, :LANES]
            l_new = result[rows, LANES:]
            m_new = jnp.where(low, m_a[rows], m_b[rows])
            if old is not None:
                u_run, m_run, l_run = old[g]
                m_tot = jnp.maximum(m_run, m_new)
                a = jnp.exp2(m_run - m_tot)
                b = jnp.exp2(m_new - m_tot)
                u_new = u_run * a + u_new * b
                l_new = l_run * a + l_new * b
                m_new = m_tot
            state.append((u_new, m_new, l_new))
        states.append(state)
    return states


def _attn_kernel(q_ref, k_ref, v_ref, o_ref, st4_ref, stt_ref):
    blk = ATTN_BLOCK
    seq = q_ref.shape[2]
    n_slabs = q_ref.shape[1]
    sub4 = seq // 4

    def key_mask(n_keys, shift):
        si = lax.broadcasted_iota(jnp.int32, (n_keys, blk), 0)
        qi = lax.broadcasted_iota(jnp.int32, (n_keys, blk), 1) + shift
        return jnp.where((si <= qi) & (si >= qi - blk), 0.0, MASKED).astype(BF16)

    stacked = n_slabs * blk
    query_onehot = (lax.broadcasted_iota(jnp.int32, (stacked, blk), 0) % blk
                    == lax.broadcasted_iota(jnp.int32, (stacked, blk), 1)).astype(BF16)
    band_bias = (query_onehot, key_mask(2 * blk, blk))
    causal_bias = (query_onehot, key_mask(blk, 0))

    def load_q(rows):
        return [q_ref[0, g, rows, :] for g in range(n_slabs)]

    def load_state(ref, rows):
        return [tuple(ref[a * n_slabs + g, rows, :] for a in range(STATE_ARRAYS)) for g in range(n_slabs)]

    def store_state(ref, rows, state):
        for g in range(n_slabs):
            for a in range(STATE_ARRAYS):
                ref[a * n_slabs + g, rows, :] = state[g][a]

    def store_out(rows, state):
        for g in range(n_slabs):
            o_ref[0, g, rows, :] = (state[g][STATE_U] / state[g][STATE_L]).astype(o_ref.dtype)

    def run(plans):
        states = _attn_blocks([(load_q(rows), k_ref[0, 0, keys, :], v_ref[0, 0, keys, :], bias, load_old)
                               for rows, keys, bias, load_old, _ in plans])
        for (rows, _, _, _, store), state in zip(plans, states):
            store(rows, state)

    def paired(n_blocks, plan, per_trip=2):
        assert n_blocks % per_trip == 0
        for i in range(n_blocks // per_trip):
            run([plan(per_trip * i + j) for j in range(per_trip)])

    def d16_plan(r16):
        rows = pl.ds(r16, blk, stride=16)
        dest = pl.ds((r16 % 4) * sub4 + r16 // 4, blk, stride=4)
        return rows, rows, causal_bias, None, lambda _, state: store_state(st4_ref, dest, state)

    paired(16, d16_plan, per_trip=16)

    def to_token_order(rows, state):
        store_state(stt_ref, rows, state)

    def d4_first_plan(r4):
        rows = pl.ds(r4, blk, stride=4)
        src = pl.ds(r4 * sub4, blk)
        return rows, rows, causal_bias, lambda: load_state(st4_ref, src), to_token_order

    paired(4, d4_first_plan, per_trip=4)
    later_blocks = sub4 // blk - 1

    def d4_plan(idx):
        r4 = idx // later_blocks
        n = 1 + idx % later_blocks
        rows = pl.ds(r4 + 4 * blk * n, blk, stride=4)
        keys = pl.ds(r4 + 4 * blk * (n - 1), 2 * blk, stride=4)
        src = pl.ds(r4 * sub4 + n * blk, blk)
        return rows, keys, band_bias, lambda: load_state(st4_ref, src), to_token_order

    paired(4 * later_blocks, d4_plan, per_trip=3)

    def d1_plan(n):
        rows = pl.ds(n * blk, blk)
        if n == 0:
            return rows, rows, causal_bias, lambda: load_state(stt_ref, rows), store_out
        return rows, pl.ds((n - 1) * blk, 2 * blk), band_bias, lambda: load_state(stt_ref, rows), store_out

    last = seq // blk - 1
    run([d1_plan(0)])
    paired(last, lambda i: d1_plan(i + 1), per_trip=3)


def _attention(q, k, v):
    batch, _, seq, _ = q.shape
    slabs_per_pair = Q_SLABS // KV_PAIRS
    q_spec = pl.BlockSpec((1, slabs_per_pair, seq, LANES), lambda b, p: (b, p, 0, 0))
    kv_spec = pl.BlockSpec((1, 1, seq, LANES), lambda b, p: (b, p, 0, 0))
    state_shape = (STATE_ARRAYS * slabs_per_pair, seq, LANES)
    return pl.pallas_call(
        _attn_kernel,
        grid=(batch, KV_PAIRS),
        in_specs=[q_spec, kv_spec, kv_spec],
        out_specs=q_spec,
        out_shape=jax.ShapeDtypeStruct(q.shape, BF16),
        scratch_shapes=[pltpu.VMEM(state_shape, F32), pltpu.VMEM(state_shape, F32)],
        compiler_params=pltpu.CompilerParams(dimension_semantics=("parallel", "parallel"),
                                             vmem_limit_bytes=56 * MIB),
        name="attention",
    )(q, k, v)


def _split_bf16(x, pieces):
    parts = []
    for _ in range(pieces):
        part = x.astype(BF16)
        parts.append(part)
        x = x - part.astype(F32)
    return parts


def _ssd_kernel(xs_ref, b_ref, c_ref, dt_ref, z_ref, arow_ref, dskip_ref, norm_ref, tri_ref, expand_ref,
                y_ref, state_ref):
    chunk = SSD_CHUNK
    lanes = range(xs_ref.shape[0])
    heads_per_group = SSM_HEADS // SSM_GROUPS

    @pl.when(pl.program_id(1) == 0)
    def _():
        state_ref[...] = jnp.zeros(state_ref.shape, F32)

    row = lax.broadcasted_iota(jnp.int32, (chunk, chunk), 0)
    col = lax.broadcasted_iota(jnp.int32, (chunk, chunk), 1)
    causal = row >= col
    low = lax.broadcasted_iota(jnp.int32, (chunk, LANES), 1) < SSM_HEAD_DIM
    zero = jnp.zeros((), BF16)

    def group_cols(g):
        return slice(g * SSM_STATE, (g + 1) * SSM_STATE), slice(g * GROUP_WIDTH, (g + 1) * GROUP_WIDTH)

    dts = [dt_ref[i] for i in lanes]
    a3s = [jnp.dot(tri_ref[...], jnp.concatenate(_split_bf16(dt * arow_ref[...], 3), axis=1),
                   preferred_element_type=F32) for dt in dts]
    a_css = [a3[:, :LANES] + a3[:, LANES:2 * LANES] + a3[:, 2 * LANES:] for a3 in a3s]

    def expand(stat):
        return jnp.dot(jnp.concatenate(_split_bf16(stat, 2), axis=1), expand_ref[...], preferred_element_type=F32)

    w_states = [expand(jnp.exp2(a_cs[chunk - 1:chunk, :] - a_cs) * dt) for a_cs, dt in zip(a_css, dts)]
    scale_offs = [expand(jnp.exp2(a_cs)) for a_cs in a_css]
    source_ts = [(a_cs - jnp.log(dt) * LOG2_E).T for a_cs, dt in zip(a_css, dts)]

    cbs, y_offs, new_states = {}, {}, {}
    for i in lanes:
        xw_bf = (xs_ref[i] * w_states[i]).astype(BF16)
        for g in range(SSM_GROUPS):
            gcols, wide = group_cols(g)
            b_g = b_ref[i, :, gcols]
            c_g = c_ref[i, :, gcols]
            cbs[i, g] = lax.dot_general(c_g, b_g, (((1,), (1,)), ((), ())), preferred_element_type=F32)
            y_offs[i, g] = jnp.dot(c_g, state_ref[i, :, wide].astype(BF16), preferred_element_type=F32)
            b_t = b_g.astype(F32).T.astype(BF16)
            new_states[i, g] = jnp.dot(b_t, xw_bf[:, wide], preferred_element_type=F32)

    y_diags = {}
    for i in lanes:
        xs_bf = xs_ref[i].astype(BF16)
        for g in range(SSM_GROUPS):
            parts = []
            for pair in range(heads_per_group // 2):
                mats = []
                for e in range(2):
                    h = g * heads_per_group + 2 * pair + e
                    seg = a_css[i][:, h:h + 1] - source_ts[i][h:h + 1, :]
                    mats.append((cbs[i, g] * jnp.exp2(jnp.where(causal, seg, -jnp.inf))).astype(BF16))
                first = (g * heads_per_group + 2 * pair) * SSM_HEAD_DIM
                x_pair = xs_bf[:, first:first + 2 * SSM_HEAD_DIM]
                rhs = jnp.concatenate([jnp.where(low, x_pair, zero), jnp.where(low, zero, x_pair)], axis=0)
                parts.append(jnp.dot(jnp.concatenate(mats, axis=1), rhs, preferred_element_type=F32))
            y_diags[i, g] = jnp.concatenate(parts, axis=1)

    for i in lanes:
        chunk_decay = scale_offs[i][chunk - 1:chunk, :]
        y_parts = []
        for g in range(SSM_GROUPS):
            _, wide = group_cols(g)
            y_parts.append(y_diags[i, g] + y_offs[i, g] * scale_offs[i][:, wide])
            state_ref[i, :, wide] = state_ref[i, :, wide] * chunk_decay[:, wide] + new_states[i, g]
        y = jnp.concatenate(y_parts, axis=1) + dskip_ref[...] * xs_ref[i]
        y = y * jax.nn.silu(z_ref[i])
        normed = []
        for g in range(SSM_GROUPS):
            y_g = y[:, g * GROUP_WIDTH:(g + 1) * GROUP_WIDTH]
            normed.append(y_g * lax.rsqrt(jnp.mean(y_g * y_g, axis=-1, keepdims=True) + NORM_EPS))
        y_ref[i] = (jnp.concatenate(normed, axis=1) * norm_ref[...]).astype(y_ref.dtype)


def _ssd(xs, bm, cm, dt, z, arow, dskip, norm, tri, expand):
    batch, seq, _ = xs.shape

    def tile(width):
        return pl.BlockSpec((SSD_BATCH, SSD_CHUNK, width), lambda b, c: (b, c, 0))

    return pl.pallas_call(
        _ssd_kernel,
        grid=(batch // SSD_BATCH, seq // SSD_CHUNK),
        in_specs=[tile(SSM_WIDTH), tile(BC_WIDTH), tile(BC_WIDTH), tile(LANES), tile(SSM_WIDTH),
                  _resident((1, LANES)), _resident((1, SSM_WIDTH)), _resident((1, SSM_WIDTH)),
                  _resident((SSD_CHUNK, SSD_CHUNK)), _resident((2 * LANES, SSM_WIDTH))],
        out_specs=tile(SSM_WIDTH),
        out_shape=jax.ShapeDtypeStruct((batch, seq, SSM_WIDTH), BF16),
        scratch_shapes=[pltpu.VMEM((SSD_BATCH, SSM_STATE, SSM_WIDTH), F32)],
        compiler_params=pltpu.CompilerParams(dimension_semantics=("parallel", "arbitrary"),
                                             vmem_limit_bytes=56 * MIB),
        name="ssd",
    )(xs, bm, cm, dt, z, arow, dskip, norm, tri, expand)


def _out_ffn_kernel(x_ref, attn_ref, y_ref, wo_attn_ref, wo_ssm_ref, mixpost_ref, pre_ref, wg_ref, wu_ref, wd_ref,
                    post_ref, o_ref):
    def project_out(t):
        rows = slice(t * FFN_ROWS, (t + 1) * FFN_ROWS)
        attn = jnp.concatenate([attn_ref[0, j, rows, :] for j in range(Q_SLABS)], axis=1)
        return (jnp.dot(attn, wo_attn_ref[...], preferred_element_type=F32)
                + jnp.dot(y_ref[0, rows, :], wo_ssm_ref[...], preferred_element_type=F32))

    mixed = [project_out(t) for t in range(OUT_FFN_TILES_PER_STEP)]

    def mixer_residual(t):
        return x_ref[0, t * FFN_ROWS:(t + 1) * FFN_ROWS, :] + _rmsnorm(mixed[t], mixpost_ref[...])

    def emit(t, out):
        o_ref[0, t * FFN_ROWS:(t + 1) * FFN_ROWS, :] = out

    tiles = [functools.partial(mixer_residual, t) for t in range(OUT_FFN_TILES_PER_STEP)]
    _swiglu_half_steps(tiles, pre_ref[...], wg_ref, wu_ref, wd_ref, post_ref[...], emit)


def _out_ffn(x3d, attn, y3d, wo_attn, w_out, mixpost, pre, wg, wu, wd, post):
    batch, seq, _ = x3d.shape
    rows = FFN_ROWS * OUT_FFN_TILES_PER_STEP
    row_spec = pl.BlockSpec((1, rows, D_MODEL), lambda b, j: (b, j, 0))
    slab_spec = pl.BlockSpec((1, Q_SLABS, rows, LANES), lambda b, j: (b, 0, j, 0))
    assert ATTN_WIDTH % SSM_WIDTH == 0
    ssm_rows = pl.BlockSpec((SSM_WIDTH, D_MODEL), lambda *_: (ATTN_WIDTH // SSM_WIDTH, 0),
                            pipeline_mode=pl.Buffered(1))
    return pl.pallas_call(
        _out_ffn_kernel,
        grid=(batch, seq // rows),
        in_specs=[row_spec, slab_spec, row_spec, _resident((ATTN_WIDTH, D_MODEL)), ssm_rows,
                  _resident((1, D_MODEL)), _resident((1, D_MODEL)), _resident((D_MODEL, D_FF)),
                  _resident((D_MODEL, D_FF)), _resident((D_FF, D_MODEL)), _resident((1, D_MODEL))],
        out_specs=row_spec,
        out_shape=jax.ShapeDtypeStruct((batch, seq, D_MODEL), F32),
        compiler_params=pltpu.CompilerParams(dimension_semantics=("parallel", "parallel"),
                                             vmem_limit_bytes=58 * MIB),
        name="out_ffn",
    )(x3d, attn, y3d, wo_attn, w_out, mixpost, pre, wg, wu, wd, post)


def _to_slab_head_order(w, axis):
    shape = w.shape
    split = shape[:axis] + (KV_PAIRS, 2, Q_PER_KV, HEAD_DIM) + shape[axis + 1:]
    return jnp.swapaxes(w.reshape(split), axis + 1, axis + 2).reshape(shape)


def _pad_lanes(v):
    return jnp.pad(v, [(0, 0)] * (v.ndim - 1) + [(0, LANES - v.shape[-1])])


def _layer(x, pos3d, invf, tri, expand, p):
    batch, seq, _ = x.shape
    tokens = batch * seq
    row = lambda v: v.reshape(1, -1)
    bf = lambda w: w.astype(BF16)

    x1 = _ffn(x.reshape(tokens, D_MODEL), row(p["ffn1_pre_norm"]), bf(p["ffn1_w_gate"]), bf(p["ffn1_w_up"]),
              bf(p["ffn1_w_down"]), row(p["ffn1_post_norm"])).reshape(batch, seq, D_MODEL)

    w_in = bf(p["w_in"])
    o_dt = ATTN_WIDTH + 2 * KV_WIDTH + CONV_CHANNELS + SSM_WIDTH
    q, k, v, xs, bm, cm, z, dt = _in_proj(
        x1, pos3d, invf, row(p["mix_pre_norm"]),
        _to_slab_head_order(w_in[:, :ATTN_WIDTH], 1), w_in, _pad_lanes(w_in[:, o_dt:]),
        p["conv_w"], row(p["conv_b"]), _pad_lanes(row(p["dt_bias"])))

    attn = _attention(q, k, v)

    arow = _pad_lanes(row(-jnp.exp(p["a_log"]) * LOG2_E))
    dskip = row(jnp.repeat(p["d_skip"], SSM_HEAD_DIM))
    y = _ssd(xs, bm, cm, dt, z, arow, dskip, row(p["ssm_norm"]), tri, expand)

    w_out = bf(p["w_out"])
    return _out_ffn(x1, attn, y, _to_slab_head_order(w_out[:ATTN_WIDTH], 0), w_out,
                    row(p["mix_post_norm"]), row(p["ffn2_pre_norm"]),
                    bf(p["ffn2_w_gate"]), bf(p["ffn2_w_up"]), bf(p["ffn2_w_down"]), row(p["ffn2_post_norm"]))


def kernel(x, positions, ffn1_pre_norm, ffn1_w_gate, ffn1_w_up, ffn1_w_down, ffn1_post_norm, mix_pre_norm, w_in, conv_w, conv_b, dt_bias, a_log, d_skip, ssm_norm, w_out, mix_post_norm, ffn2_pre_norm, ffn2_w_gate, ffn2_w_up, ffn2_w_down, ffn2_post_norm):
    params = dict(ffn1_pre_norm=ffn1_pre_norm, ffn1_w_gate=ffn1_w_gate, ffn1_w_up=ffn1_w_up,
                  ffn1_w_down=ffn1_w_down, ffn1_post_norm=ffn1_post_norm, mix_pre_norm=mix_pre_norm, w_in=w_in,
                  conv_w=conv_w, conv_b=conv_b, dt_bias=dt_bias, a_log=a_log, d_skip=d_skip, ssm_norm=ssm_norm,
                  w_out=w_out, mix_post_norm=mix_post_norm, ffn2_pre_norm=ffn2_pre_norm, ffn2_w_gate=ffn2_w_gate,
                  ffn2_w_up=ffn2_w_up, ffn2_w_down=ffn2_w_down, ffn2_post_norm=ffn2_post_norm)
    depth = w_in.shape[0]
    batch, seq, _ = x.shape
    inv_freq = ROPE_THETA ** (-jnp.arange(0, HEAD_DIM, 2, dtype=F32) / HEAD_DIM)
    invf = inv_freq.reshape(HEAD_DIM // 2, 1)
    pos3d = positions.reshape(batch, 1, seq)
    idx = jnp.arange(SSD_CHUNK)
    tri = (idx[:, None] >= idx[None, :]).astype(BF16)
    expand = (jnp.arange(2 * LANES)[:, None] % LANES == jnp.arange(SSM_WIDTH)[None, :] // SSM_HEAD_DIM).astype(BF16)
    for i in range(depth):
        x = _layer(x, pos3d, invf, tri, expand, {name: w[i] for name, w in params.items()})
    return x
```

```python
import functools
import math

import jax
import jax.numpy as jnp
from jax import lax
from jax.experimental import pallas as pl
from jax.experimental.pallas import tpu as pltpu

F32 = jnp.float32
BF16 = jnp.bfloat16

D_MODEL = 1024
D_FF = 2816
HEAD_DIM = 64
N_Q_HEADS = 16
N_KV_HEADS = 4
Q_PER_KV = N_Q_HEADS // N_KV_HEADS
ATTN_WIDTH = N_Q_HEADS * HEAD_DIM
KV_WIDTH = N_KV_HEADS * HEAD_DIM
ATTN_BLOCK = 128
ROPE_THETA = 10000.0
SSM_HEADS = 16
SSM_HEAD_DIM = 64
SSM_WIDTH = SSM_HEADS * SSM_HEAD_DIM
SSM_STATE = 128
SSM_GROUPS = 2
GROUP_WIDTH = SSM_WIDTH // SSM_GROUPS
BC_WIDTH = SSM_GROUPS * SSM_STATE
CONV_WIDTH = 4
CONV_CHANNELS = SSM_WIDTH + 2 * BC_WIDTH
SSD_CHUNK = 128
MACARON_WEIGHT = 0.5
NORM_EPS = 1e-6
LOG2_E = math.log2(math.e)

LANES = 128
CARRY_ROWS = 8
MIB = 1024 * 1024

FFN_ROWS = 512
FFN_TILES_PER_STEP = 2
OUT_FFN_TILES_PER_STEP = 2
FF_CHUNK = 256
assert D_FF % FF_CHUNK == 0
PROJ_ROWS = 512
SSD_BATCH = 8
SSD_GROUP = 4

Q_SLABS = ATTN_WIDTH // LANES
KV_SLABS = KV_WIDTH // LANES
KV_PAIRS = N_KV_HEADS // 2


def _rmsnorm(x, gain):
    return x * lax.rsqrt(jnp.mean(x * x, axis=-1, keepdims=True) + NORM_EPS) * gain


def _resident(shape):
    return pl.BlockSpec(shape, lambda *_: (0,) * len(shape), pipeline_mode=pl.Buffered(1))


def _swiglu_half_steps(load_tiles, pre, wg_ref, wu_ref, wd_ref, post, emit):
    finish = None
    x = load_tiles[0]()
    xn = _rmsnorm(x, pre).astype(BF16)
    for t in range(len(load_tiles)):
        acc = jnp.zeros(x.shape, F32)
        x_next = xn_next = None
        for c in range(D_FF // FF_CHUNK):
            cols = slice(c * FF_CHUNK, (c + 1) * FF_CHUNK)
            gate = jnp.dot(xn, wg_ref[:, cols], preferred_element_type=F32)
            up = jnp.dot(xn, wu_ref[:, cols], preferred_element_type=F32)
            hidden = (jax.nn.silu(gate) * up).astype(BF16)
            acc = acc + jnp.dot(hidden, wd_ref[cols, :], preferred_element_type=F32)
            if c == 1 and finish is not None:
                finish()
            if c == 2 and t + 1 < len(load_tiles):
                x_next = load_tiles[t + 1]()
                xn_next = _rmsnorm(x_next, pre).astype(BF16)
        finish = functools.partial(lambda t_, x_t, acc_t: emit(t_, x_t + MACARON_WEIGHT * _rmsnorm(acc_t, post)),
                                   t, x, acc)
        x, xn = x_next, xn_next
    finish()


def _ffn_kernel(x_ref, pre_ref, wg_ref, wu_ref, wd_ref, post_ref, o_ref):
    tiles = [functools.partial(lambda t: x_ref[t * FFN_ROWS:(t + 1) * FFN_ROWS, :], t)
             for t in range(FFN_TILES_PER_STEP)]

    def emit(t, out):
        o_ref[t * FFN_ROWS:(t + 1) * FFN_ROWS, :] = out

    _swiglu_half_steps(tiles, pre_ref[...], wg_ref, wu_ref, wd_ref, post_ref[...], emit)


def _ffn(x2d, pre, wg, wu, wd, post):
    tokens = x2d.shape[0]
    step_rows = FFN_ROWS * FFN_TILES_PER_STEP
    row_spec = pl.BlockSpec((step_rows, D_MODEL), lambda i: (i, 0))
    return pl.pallas_call(
        _ffn_kernel,
        grid=(tokens // step_rows,),
        in_specs=[row_spec, _resident((1, D_MODEL)), _resident((D_MODEL, D_FF)), _resident((D_MODEL, D_FF)),
                  _resident((D_FF, D_MODEL)), _resident((1, D_MODEL))],
        out_specs=row_spec,
        out_shape=jax.ShapeDtypeStruct((tokens, D_MODEL), F32),
        compiler_params=pltpu.CompilerParams(dimension_semantics=("parallel",), vmem_limit_bytes=48 * MIB),
        name="ffn",
    )(x2d, pre, wg, wu, wd, post)


def _softplus(x):
    return jnp.maximum(x, 0.0) + jnp.log1p(jnp.exp(-jnp.abs(x)))


def _inproj_kernel(x_ref, pos_ref, invf_ref, gain_ref, wq_ref, wk_ref, wv_ref, wx_ref, wz_ref, wdt_ref,
                   convw_ref, convb_ref, dtb_ref,
                   q_ref, k_ref, v_ref, xs_ref, b_ref, c_ref, z_ref, dt_ref, ubuf_ref):
    rows = x_ref.shape[1]
    n_slabs = CONV_CHANNELS // LANES

    @pl.when(pl.program_id(1) == 0)
    def _():
        ubuf_ref[:, 0:CARRY_ROWS, :] = jnp.zeros((n_slabs, CARRY_ROWS, LANES), F32)

    xn = _rmsnorm(x_ref[0], gain_ref[...]).astype(BF16)
    wide = 2 * LANES

    def project(w_ref, c):
        return jnp.dot(xn, w_ref[:, c * wide:(c + 1) * wide], preferred_element_type=F32)

    def conv_chunk(c, u):
        for j in (2 * c, 2 * c + 1):
            cols = slice(j * LANES, (j + 1) * LANES)
            ubuf_ref[j, CARRY_ROWS:CARRY_ROWS + rows, :] = u[:, (j % 2) * LANES:(j % 2 + 1) * LANES]
            conv = (convb_ref[:, cols]
                    + ubuf_ref[j, CARRY_ROWS:CARRY_ROWS + rows, :] * convw_ref[CONV_WIDTH - 1:CONV_WIDTH, cols])
            for tap in range(CONV_WIDTH - 1):
                start = CARRY_ROWS - (CONV_WIDTH - 1) + tap
                conv = conv + ubuf_ref[j, start:start + rows, :] * convw_ref[tap:tap + 1, cols]
            act = jax.nn.silu(conv)
            if j < SSM_WIDTH // LANES:
                xs_ref[0, :, cols] = act
            elif j < (SSM_WIDTH + BC_WIDTH) // LANES:
                b_ref[0, :, j * LANES - SSM_WIDTH:(j + 1) * LANES - SSM_WIDTH] = act.astype(BF16)
            else:
                first = SSM_WIDTH + BC_WIDTH
                c_ref[0, :, j * LANES - first:(j + 1) * LANES - first] = act.astype(BF16)
            ubuf_ref[j, 0:CARRY_ROWS, :] = ubuf_ref[j, rows:rows + CARRY_ROWS, :]

    def z_chunk(c):
        z_ref[0, :, c * wide:(c + 1) * wide] = project(wz_ref, c)

    u = project(wx_ref, 0)
    z_chunk(0)

    ang_t = invf_ref[...] * pos_ref[0].astype(F32)
    cos_t = jnp.cos(ang_t)
    sin_t = jnp.sin(ang_t)
    copies = LANES // HEAD_DIM
    cos = jnp.concatenate([cos_t, cos_t] * copies, axis=0).T
    sin = jnp.concatenate([-sin_t, sin_t] * copies, axis=0).T
    lane = lax.broadcasted_iota(jnp.int32, (rows, LANES), 1)
    first_half = (lane % HEAD_DIM) < (HEAD_DIM // 2)

    def rope(t):
        back = pltpu.roll(t, HEAD_DIM // 2, 1)
        fwd = pltpu.roll(t, LANES - HEAD_DIM // 2, 1)
        return t * cos + jnp.where(first_half, fwd, back) * sin

    def q_chunk(c):
        q = project(wq_ref, c)
        for half in range(2):
            q_ref[0, 2 * c + half] = rope(q[:, half * LANES:(half + 1) * LANES]) * (HEAD_DIM ** -0.5 * LOG2_E)

    def kv_chunk(c):
        k = project(wk_ref, c)
        v = project(wv_ref, c)
        for half in range(2):
            k_ref[0, 2 * c + half] = rope(k[:, half * LANES:(half + 1) * LANES])
            v_ref[0, 2 * c + half] = v[:, half * LANES:(half + 1) * LANES]

    dt_ref[0] = _softplus(jnp.dot(xn, wdt_ref[...], preferred_element_type=F32) + dtb_ref[...])
    n_conv = CONV_CHANNELS // wide
    fillers = ([functools.partial(q_chunk, c) for c in range(ATTN_WIDTH // wide)]
               + [functools.partial(kv_chunk, c) for c in range(KV_WIDTH // wide)]
               + [functools.partial(z_chunk, c) for c in range(1, SSM_WIDTH // wide)])
    for c in range(n_conv):
        u_next = project(wx_ref, c + 1) if c + 1 < n_conv else None
        conv_chunk(c, u)
        fillers.pop(0)()
        u = u_next
    for filler in fillers:
        filler()


def _in_proj(x3d, pos3d, invf, gain, wq, w_in, wdt, convw, convb, dtb):
    batch, seq, _ = x3d.shape
    rows = PROJ_ROWS

    def columns(start, width):
        assert start % width == 0
        return pl.BlockSpec((D_MODEL, width), lambda *_: (0, start // width), pipeline_mode=pl.Buffered(1))

    o_k = ATTN_WIDTH
    o_v = o_k + KV_WIDTH
    o_x = o_v + KV_WIDTH
    o_z = o_x + CONV_CHANNELS

    def tile(width):
        return pl.BlockSpec((1, rows, width), lambda b, j: (b, j, 0))

    def slabs(n):
        return pl.BlockSpec((1, n, rows, LANES), lambda b, j: (b, 0, j, 0))

    def out(width, dtype):
        return jax.ShapeDtypeStruct((batch, seq, width), dtype)

    def out_slabs(n):
        return jax.ShapeDtypeStruct((batch, n, seq, LANES), F32)

    return pl.pallas_call(
        _inproj_kernel,
        grid=(batch, seq // rows),
        in_specs=[tile(D_MODEL), pl.BlockSpec((1, 1, rows), lambda b, j: (b, 0, j)),
                  _resident((HEAD_DIM // 2, 1)), _resident((1, D_MODEL)),
                  _resident((D_MODEL, ATTN_WIDTH)), columns(o_k, KV_WIDTH), columns(o_v, KV_WIDTH),
                  columns(o_x, CONV_CHANNELS), columns(o_z, SSM_WIDTH),
                  _resident((D_MODEL, LANES)), _resident((CONV_WIDTH, CONV_CHANNELS)),
                  _resident((1, CONV_CHANNELS)), _resident((1, LANES))],
        out_specs=[slabs(Q_SLABS), slabs(KV_SLABS), slabs(KV_SLABS), tile(SSM_WIDTH), tile(BC_WIDTH),
                   tile(BC_WIDTH), tile(SSM_WIDTH), tile(LANES)],
        out_shape=[out_slabs(Q_SLABS), out_slabs(KV_SLABS), out_slabs(KV_SLABS), out(SSM_WIDTH, F32),
                   out(BC_WIDTH, BF16), out(BC_WIDTH, BF16), out(SSM_WIDTH, F32), out(LANES, F32)],
        scratch_shapes=[pltpu.VMEM((CONV_CHANNELS // LANES, rows + CARRY_ROWS, LANES), F32)],
        compiler_params=pltpu.CompilerParams(dimension_semantics=("parallel", "arbitrary"),
                                             vmem_limit_bytes=56 * MIB),
        name="in_proj",
    )(x3d, pos3d, invf, gain, wq, w_in, w_in, w_in, w_in, wdt, convw, convb, dtb)


STATE_ARRAYS = 3
MASKED = -2.0 ** 100
STATE_U, STATE_M, STATE_L = range(STATE_ARRAYS)


def _attn_blocks(blocks):
    blk = ATTN_BLOCK
    low = lax.broadcasted_iota(jnp.int32, (blk, LANES), 1) < HEAD_DIM
    zero = jnp.zeros((), BF16)

    scores = []
    for q_slabs, k_blk, _, (query_onehot, key_mask), _ in blocks:
        n_keys = k_blk.shape[0]
        kb = k_blk.astype(BF16)
        key_low = lax.broadcasted_iota(jnp.int32, kb.shape, 1) < HEAD_DIM
        keys = jnp.concatenate([jnp.concatenate([jnp.where(key_low, kb, zero), key_mask], axis=1),
                                jnp.concatenate([jnp.where(key_low, zero, kb), key_mask], axis=1)], axis=0)
        lhs = jnp.concatenate([jnp.concatenate([q.astype(BF16) for q in q_slabs], axis=0), query_onehot], axis=1)
        s = lax.dot_general(lhs, keys, (((1,), (1,)), ((), ())), preferred_element_type=F32)
        scores += [s[:, :n_keys], s[:, n_keys:]]
    maxes = [jnp.max(s, axis=-1, keepdims=True) for s in scores]
    probs = [jnp.exp2((s - m).astype(BF16)) for s, m in zip(scores, maxes)]
    results = []
    for i, (_, _, v_blk, _, _) in enumerate(blocks):
        vb = v_blk.astype(BF16)
        key_low = lax.broadcasted_iota(jnp.int32, vb.shape, 1) < HEAD_DIM
        ones_low = jnp.where(key_low, 1.0, 0.0).astype(BF16)
        ones_high = jnp.where(key_low, 0.0, 1.0).astype(BF16)
        rhs = jnp.concatenate([
            jnp.concatenate([jnp.where(key_low, vb, zero), ones_low], axis=1),
            jnp.concatenate([jnp.where(key_low, zero, vb), ones_high], axis=1)], axis=0)
        lhs = jnp.concatenate([probs[2 * i], probs[2 * i + 1]], axis=1)
        results.append(jnp.dot(lhs, rhs, preferred_element_type=F32))
    states = []
    for i, (q_slabs, _, _, _, load_old) in enumerate(blocks):
        result, m_a, m_b = results[i], maxes[2 * i], maxes[2 * i + 1]
        old = None if load_old is None else load_old()
        state = []
        for g in range(len(q_slabs)):
            rows = slice(g * blk, (g + 1) * blk)
            u_new = result[rows, :LANES]
            l_new = result[rows, LANES:]
            m_new = jnp.where(low, m_a[rows], m_b[rows])
            if old is not None:
                u_run, m_run, l_run = old[g]
                m_tot = jnp.maximum(m_run, m_new)
                a = jnp.exp2(m_run - m_tot)
                b = jnp.exp2(m_new - m_tot)
                u_new = u_run * a + u_new * b
                l_new = l_run * a + l_new * b
                m_new = m_tot
            state.append((u_new, m_new, l_new))
        states.append(state)
    return states


def _attn_kernel(q_ref, k_ref, v_ref, o_ref, st4_ref, stt_ref):
    blk = ATTN_BLOCK
    seq = q_ref.shape[2]
    n_slabs = q_ref.shape[1]
    sub4 = seq // 4

    def key_mask(n_keys, shift):
        si = lax.broadcasted_iota(jnp.int32, (n_keys, blk), 0)
        qi = lax.broadcasted_iota(jnp.int32, (n_keys, blk), 1) + shift
        return jnp.where((si <= qi) & (si >= qi - blk), 0.0, MASKED).astype(BF16)

    stacked = n_slabs * blk
    query_onehot = (lax.broadcasted_iota(jnp.int32, (stacked, blk), 0) % blk
                    == lax.broadcasted_iota(jnp.int32, (stacked, blk), 1)).astype(BF16)
    band_bias = (query_onehot, key_mask(2 * blk, blk))
    causal_bias = (query_onehot, key_mask(blk, 0))

    def load_q(rows):
        return [q_ref[0, g, rows, :] for g in range(n_slabs)]

    def load_state(ref, rows):
        return [tuple(ref[a * n_slabs + g, rows, :] for a in range(STATE_ARRAYS)) for g in range(n_slabs)]

    def store_state(ref, rows, state):
        for g in range(n_slabs):
            for a in range(STATE_ARRAYS):
                ref[a * n_slabs + g, rows, :] = state[g][a]

    def store_out(rows, state):
        for g in range(n_slabs):
            o_ref[0, g, rows, :] = (state[g][STATE_U] / state[g][STATE_L]).astype(o_ref.dtype)

    def run(plans):
        states = _attn_blocks([(load_q(rows), k_ref[0, 0, keys, :], v_ref[0, 0, keys, :], bias, load_old)
                               for rows, keys, bias, load_old, _ in plans])
        for (rows, _, _, _, store), state in zip(plans, states):
            store(rows, state)

    def paired(n_blocks, plan, per_trip=2):
        assert n_blocks % per_trip == 0
        for i in range(n_blocks // per_trip):
            run([plan(per_trip * i + j) for j in range(per_trip)])

    def d16_plan(r16):
        rows = pl.ds(r16, blk, stride=16)
        dest = pl.ds((r16 % 4) * sub4 + r16 // 4, blk, stride=4)
        return rows, rows, causal_bias, None, lambda _, state: store_state(st4_ref, dest, state)

    paired(16, d16_plan, per_trip=16)

    def to_token_order(rows, state):
        store_state(stt_ref, rows, state)

    def d4_first_plan(r4):
        rows = pl.ds(r4, blk, stride=4)
        src = pl.ds(r4 * sub4, blk)
        return rows, rows, causal_bias, lambda: load_state(st4_ref, src), to_token_order

    paired(4, d4_first_plan, per_trip=4)
    later_blocks = sub4 // blk - 1

    def d4_plan(idx):
        r4 = idx // later_blocks
        n = 1 + idx % later_blocks
        rows = pl.ds(r4 + 4 * blk * n, blk, stride=4)
        keys = pl.ds(r4 + 4 * blk * (n - 1), 2 * blk, stride=4)
        src = pl.ds(r4 * sub4 + n * blk, blk)
        return rows, keys, band_bias, lambda: load_state(st4_ref, src), to_token_order

    paired(4 * later_blocks, d4_plan, per_trip=3)

    def d1_plan(n):
        rows = pl.ds(n * blk, blk)
        if n == 0:
            return rows, rows, causal_bias, lambda: load_state(stt_ref, rows), store_out
        return rows, pl.ds((n - 1) * blk, 2 * blk), band_bias, lambda: load_state(stt_ref, rows), store_out

    last = seq // blk - 1
    run([d1_plan(0)])
    paired(last, lambda i: d1_plan(i + 1), per_trip=3)


def _attention(q, k, v):
    batch, _, seq, _ = q.shape
    slabs_per_pair = Q_SLABS // KV_PAIRS
    q_spec = pl.BlockSpec((1, slabs_per_pair, seq, LANES), lambda b, p: (b, p, 0, 0))
    kv_spec = pl.BlockSpec((1, 1, seq, LANES), lambda b, p: (b, p, 0, 0))
    state_shape = (STATE_ARRAYS * slabs_per_pair, seq, LANES)
    return pl.pallas_call(
        _attn_kernel,
        grid=(batch, KV_PAIRS),
        in_specs=[q_spec, kv_spec, kv_spec],
        out_specs=q_spec,
        out_shape=jax.ShapeDtypeStruct(q.shape, BF16),
        scratch_shapes=[pltpu.VMEM(state_shape, F32), pltpu.VMEM(state_shape, F32)],
        compiler_params=pltpu.CompilerParams(dimension_semantics=("parallel", "parallel"),
                                             vmem_limit_bytes=56 * MIB),
        name="attention",
    )(q, k, v)


def _split_bf16(x, pieces):
    parts = []
    for _ in range(pieces):
        part = x.astype(BF16)
        parts.append(part)
        x = x - part.astype(F32)
    return parts


def _ssd_kernel(xs_ref, b_ref, c_ref, dt_ref, z_ref, arow_ref, dskip_ref, norm_ref, tri_ref, expand_ref,
                y_ref, state_ref):
    @pl.when(pl.program_id(1) == 0)
    def _():
        state_ref[...] = jnp.zeros(state_ref.shape, F32)

    for first in range(0, xs_ref.shape[0], SSD_GROUP):
        part = pl.ds(first, SSD_GROUP)
        _ssd_group(xs_ref.at[part], b_ref.at[part], c_ref.at[part], dt_ref.at[part], z_ref.at[part], arow_ref,
                   dskip_ref, norm_ref, tri_ref, expand_ref, y_ref.at[part], state_ref.at[part])


def _ssd_group(xs_ref, b_ref, c_ref, dt_ref, z_ref, arow_ref, dskip_ref, norm_ref, tri_ref, expand_ref,
               y_ref, state_ref):
    chunk = SSD_CHUNK
    lanes = range(xs_ref.shape[0])
    heads_per_group = SSM_HEADS // SSM_GROUPS

    row = lax.broadcasted_iota(jnp.int32, (chunk, chunk), 0)
    col = lax.broadcasted_iota(jnp.int32, (chunk, chunk), 1)
    causal = row >= col
    low = lax.broadcasted_iota(jnp.int32, (chunk, LANES), 1) < SSM_HEAD_DIM
    zero = jnp.zeros((), BF16)

    def group_cols(g):
        return slice(g * SSM_STATE, (g + 1) * SSM_STATE), slice(g * GROUP_WIDTH, (g + 1) * GROUP_WIDTH)

    dts = [dt_ref[i] for i in lanes]
    a3s = [jnp.dot(tri_ref[...], jnp.concatenate(_split_bf16(dt * arow_ref[...], 3), axis=1),
                   preferred_element_type=F32) for dt in dts]
    a_css = [a3[:, :LANES] + a3[:, LANES:2 * LANES] + a3[:, 2 * LANES:] for a3 in a3s]

    def expand(stat):
        return jnp.dot(jnp.concatenate(_split_bf16(stat, 2), axis=1), expand_ref[...], preferred_element_type=F32)

    w_states = [expand(jnp.exp2(a_cs[chunk - 1:chunk, :] - a_cs) * dt) for a_cs, dt in zip(a_css, dts)]
    scale_offs = [expand(jnp.exp2(a_cs)) for a_cs in a_css]
    source_ts = [(a_cs - jnp.log(dt) * LOG2_E).T for a_cs, dt in zip(a_css, dts)]

    cbs, y_offs, new_states = {}, {}, {}
    for i in lanes:
        xw_bf = (xs_ref[i] * w_states[i]).astype(BF16)
        for g in range(SSM_GROUPS):
            gcols, wide = group_cols(g)
            b_g = b_ref[i, :, gcols]
            c_g = c_ref[i, :, gcols]
            cbs[i, g] = lax.dot_general(c_g, b_g, (((1,), (1,)), ((), ())), preferred_element_type=F32)
            y_offs[i, g] = jnp.dot(c_g, state_ref[i, :, wide].astype(BF16), preferred_element_type=F32)
            b_t = b_g.astype(F32).T.astype(BF16)
            new_states[i, g] = jnp.dot(b_t, xw_bf[:, wide], preferred_element_type=F32)

    y_diags = {}
    for i in lanes:
        xs_bf = xs_ref[i].astype(BF16)
        for g in range(SSM_GROUPS):
            parts = []
            for pair in range(heads_per_group // 2):
                mats = []
                for e in range(2):
                    h = g * heads_per_group + 2 * pair + e
                    seg = a_css[i][:, h:h + 1] - source_ts[i][h:h + 1, :]
                    mats.append((cbs[i, g] * jnp.exp2(jnp.where(causal, seg, -jnp.inf))).astype(BF16))
                first = (g * heads_per_group + 2 * pair) * SSM_HEAD_DIM
                x_pair = xs_bf[:, first:first + 2 * SSM_HEAD_DIM]
                rhs = jnp.concatenate([jnp.where(low, x_pair, zero), jnp.where(low, zero, x_pair)], axis=0)
                parts.append(jnp.dot(jnp.concatenate(mats, axis=1), rhs, preferred_element_type=F32))
            y_diags[i, g] = jnp.concatenate(parts, axis=1)

    for i in lanes:
        chunk_decay = scale_offs[i][chunk - 1:chunk, :]
        y_parts = []
        for g in range(SSM_GROUPS):
            _, wide = group_cols(g)
            y_parts.append(y_diags[i, g] + y_offs[i, g] * scale_offs[i][:, wide])
            state_ref[i, :, wide] = state_ref[i, :, wide] * chunk_decay[:, wide] + new_states[i, g]
        y = jnp.concatenate(y_parts, axis=1) + dskip_ref[...] * xs_ref[i]
        y = y * jax.nn.silu(z_ref[i])
        normed = []
        for g in range(SSM_GROUPS):
            y_g = y[:, g * GROUP_WIDTH:(g + 1) * GROUP_WIDTH]
            normed.append(y_g * lax.rsqrt(jnp.mean(y_g * y_g, axis=-1, keepdims=True) + NORM_EPS))
        y_ref[i] = (jnp.concatenate(normed, axis=1) * norm_ref[...]).astype(y_ref.dtype)


def _ssd(xs, bm, cm, dt, z, arow, dskip, norm, tri, expand):
    batch, seq, _ = xs.shape

    def tile(width):
        return pl.BlockSpec((SSD_BATCH, SSD_CHUNK, width), lambda b, c: (b, c, 0))

    return pl.pallas_call(
        _ssd_kernel,
        grid=(batch // SSD_BATCH, seq // SSD_CHUNK),
        in_specs=[tile(SSM_WIDTH), tile(BC_WIDTH), tile(BC_WIDTH), tile(LANES), tile(SSM_WIDTH),
                  _resident((1, LANES)), _resident((1, SSM_WIDTH)), _resident((1, SSM_WIDTH)),
                  _resident((SSD_CHUNK, SSD_CHUNK)), _resident((2 * LANES, SSM_WIDTH))],
        out_specs=tile(SSM_WIDTH),
        out_shape=jax.ShapeDtypeStruct((batch, seq, SSM_WIDTH), BF16),
        scratch_shapes=[pltpu.VMEM((SSD_BATCH, SSM_STATE, SSM_WIDTH), F32)],
        compiler_params=pltpu.CompilerParams(dimension_semantics=("parallel", "arbitrary"),
                                             vmem_limit_bytes=56 * MIB),
        name="ssd",
    )(xs, bm, cm, dt, z, arow, dskip, norm, tri, expand)


def _out_ffn_kernel(x_ref, attn_ref, y_ref, wo_attn_ref, wo_ssm_ref, mixpost_ref, pre_ref, wg_ref, wu_ref, wd_ref,
                    post_ref, o_ref):
    def project_out(t):
        rows = slice(t * FFN_ROWS, (t + 1) * FFN_ROWS)
        attn = jnp.concatenate([attn_ref[0, j, rows, :] for j in range(Q_SLABS)], axis=1)
        return (jnp.dot(attn, wo_attn_ref[...], preferred_element_type=F32)
                + jnp.dot(y_ref[0, rows, :], wo_ssm_ref[...], preferred_element_type=F32))

    mixed = [project_out(t) for t in range(OUT_FFN_TILES_PER_STEP)]

    def mixer_residual(t):
        return x_ref[0, t * FFN_ROWS:(t + 1) * FFN_ROWS, :] + _rmsnorm(mixed[t], mixpost_ref[...])

    def emit(t, out):
        o_ref[0, t * FFN_ROWS:(t + 1) * FFN_ROWS, :] = out

    tiles = [functools.partial(mixer_residual, t) for t in range(OUT_FFN_TILES_PER_STEP)]
    _swiglu_half_steps(tiles, pre_ref[...], wg_ref, wu_ref, wd_ref, post_ref[...], emit)


def _out_ffn(x3d, attn, y3d, wo_attn, w_out, mixpost, pre, wg, wu, wd, post):
    batch, seq, _ = x3d.shape
    rows = FFN_ROWS * OUT_FFN_TILES_PER_STEP
    row_spec = pl.BlockSpec((1, rows, D_MODEL), lambda b, j: (b, j, 0))
    slab_spec = pl.BlockSpec((1, Q_SLABS, rows, LANES), lambda b, j: (b, 0, j, 0))
    assert ATTN_WIDTH % SSM_WIDTH == 0
    ssm_rows = pl.BlockSpec((SSM_WIDTH, D_MODEL), lambda *_: (ATTN_WIDTH // SSM_WIDTH, 0),
                            pipeline_mode=pl.Buffered(1))
    return pl.pallas_call(
        _out_ffn_kernel,
        grid=(batch, seq // rows),
        in_specs=[row_spec, slab_spec, row_spec, _resident((ATTN_WIDTH, D_MODEL)), ssm_rows,
                  _resident((1, D_MODEL)), _resident((1, D_MODEL)), _resident((D_MODEL, D_FF)),
                  _resident((D_MODEL, D_FF)), _resident((D_FF, D_MODEL)), _resident((1, D_MODEL))],
        out_specs=row_spec,
        out_shape=jax.ShapeDtypeStruct((batch, seq, D_MODEL), F32),
        compiler_params=pltpu.CompilerParams(dimension_semantics=("parallel", "parallel"),
                                             vmem_limit_bytes=58 * MIB),
        name="out_ffn",
    )(x3d, attn, y3d, wo_attn, w_out, mixpost, pre, wg, wu, wd, post)


def _to_slab_head_order(w, axis):
    shape = w.shape
    split = shape[:axis] + (KV_PAIRS, 2, Q_PER_KV, HEAD_DIM) + shape[axis + 1:]
    return jnp.swapaxes(w.reshape(split), axis + 1, axis + 2).reshape(shape)


def _pad_lanes(v):
    return jnp.pad(v, [(0, 0)] * (v.ndim - 1) + [(0, LANES - v.shape[-1])])


def _layer(x, pos3d, invf, tri, expand, p):
    batch, seq, _ = x.shape
    tokens = batch * seq
    row = lambda v: v.reshape(1, -1)
    bf = lambda w: w.astype(BF16)

    x1 = _ffn(x.reshape(tokens, D_MODEL), row(p["ffn1_pre_norm"]), bf(p["ffn1_w_gate"]), bf(p["ffn1_w_up"]),
              bf(p["ffn1_w_down"]), row(p["ffn1_post_norm"])).reshape(batch, seq, D_MODEL)

    w_in = bf(p["w_in"])
    o_dt = ATTN_WIDTH + 2 * KV_WIDTH + CONV_CHANNELS + SSM_WIDTH
    q, k, v, xs, bm, cm, z, dt = _in_proj(
        x1, pos3d, invf, row(p["mix_pre_norm"]),
        _to_slab_head_order(w_in[:, :ATTN_WIDTH], 1), w_in, _pad_lanes(w_in[:, o_dt:]),
        p["conv_w"], row(p["conv_b"]), _pad_lanes(row(p["dt_bias"])))

    attn = _attention(q, k, v)

    arow = _pad_lanes(row(-jnp.exp(p["a_log"]) * LOG2_E))
    dskip = row(jnp.repeat(p["d_skip"], SSM_HEAD_DIM))
    y = _ssd(xs, bm, cm, dt, z, arow, dskip, row(p["ssm_norm"]), tri, expand)

    w_out = bf(p["w_out"])
    return _out_ffn(x1, attn, y, _to_slab_head_order(w_out[:ATTN_WIDTH], 0), w_out,
                    row(p["mix_post_norm"]), row(p["ffn2_pre_norm"]),
                    bf(p["ffn2_w_gate"]), bf(p["ffn2_w_up"]), bf(p["ffn2_w_down"]), row(p["ffn2_post_norm"]))


def kernel(x, positions, ffn1_pre_norm, ffn1_w_gate, ffn1_w_up, ffn1_w_down, ffn1_post_norm, mix_pre_norm, w_in, conv_w, conv_b, dt_bias, a_log, d_skip, ssm_norm, w_out, mix_post_norm, ffn2_pre_norm, ffn2_w_gate, ffn2_w_up, ffn2_w_down, ffn2_post_norm):
    params = dict(ffn1_pre_norm=ffn1_pre_norm, ffn1_w_gate=ffn1_w_gate, ffn1_w_up=ffn1_w_up,
                  ffn1_w_down=ffn1_w_down, ffn1_post_norm=ffn1_post_norm, mix_pre_norm=mix_pre_norm, w_in=w_in,
                  conv_w=conv_w, conv_b=conv_b, dt_bias=dt_bias, a_log=a_log, d_skip=d_skip, ssm_norm=ssm_norm,
                  w_out=w_out, mix_post_norm=mix_post_norm, ffn2_pre_norm=ffn2_pre_norm, ffn2_w_gate=ffn2_w_gate,
                  ffn2_w_up=ffn2_w_up, ffn2_w_down=ffn2_w_down, ffn2_post_norm=ffn2_post_norm)
    depth = w_in.shape[0]
    batch, seq, _ = x.shape
    inv_freq = ROPE_THETA ** (-jnp.arange(0, HEAD_DIM, 2, dtype=F32) / HEAD_DIM)
    invf = inv_freq.reshape(HEAD_DIM // 2, 1)
    pos3d = positions.reshape(batch, 1, seq)
    idx = jnp.arange(SSD_CHUNK)
    tri = (idx[:, None] >= idx[None, :]).astype(BF16)
    expand = (jnp.arange(2 * LANES)[:, None] % LANES == jnp.arange(SSM_WIDTH)[None, :] // SSM_HEAD_DIM).astype(BF16)
    for i in range(depth):
        x = _layer(x, pos3d, invf, tri, expand, {name: w[i] for name, w in params.items()})
    return x
```
